```python
import jax, jax.numpy as jnp
from jax import lax
import numpy as np

D_MODEL = 1024
BATCH = 16
SEQ = 2048
DEPTH = 2

HEAD_DIM = 64
SB_HEADS = 8
NSA_HEADS = 8
NSA_KV_GROUPS = 2
NSA_REP = NSA_HEADS // NSA_KV_GROUPS
NSA_CMP_LEN = 32
NSA_CMP_STRIDE = 16
NSA_CMP_HIDDEN = 256
NSA_SLC_BLOCK = 64
NSA_SLC_TOPK = 16
NSA_WINDOW = 512
Q_BLOCK = 128
SLC_Q_BLOCK = 32
SB_W = SB_HEADS * HEAD_DIM
NSA_Q_W = NSA_HEADS * HEAD_DIM
NSA_KV_W = 6 * NSA_KV_GROUPS * HEAD_DIM
NSA_GATE_W = 3 * NSA_HEADS
MIX0_IN = 3 * SB_W + NSA_Q_W + NSA_KV_W + NSA_GATE_W
MIX0_OUT = SB_W + NSA_Q_W
CONV_WIDTH = 3
D_FF = 2816
N_EXPERTS = 8
TOP_K = 2
EXPERT_D_FF = 2816
MOE_ROW_BLOCK = 512
LN_EPS = 1e-5
DN_ALPHA = (2 * DEPTH) ** 0.25
DN_BETA = (8 * DEPTH) ** -0.25
NEG = -1e30
SEL_FORCE = 1e30

kernel_name = 'hybrid_sb_nsa_conv_moe_block'


def layer_norm(x, g, b):
    xf = x.astype(jnp.float32)
    mu = xf.mean(-1, keepdims=True)
    var = jnp.square(xf - mu).mean(-1, keepdims=True)
    return ((xf - mu) * lax.rsqrt(var + LN_EPS) * g + b).astype(x.dtype)


def masked_softmax(s, mask):
    return jax.nn.softmax(jnp.where(mask, s, NEG), axis=-1) * mask


def stick_breaking_attention(q, k, v):
    S = q.shape[2]
    scale = HEAD_DIM ** -0.5
    outs = []
    for start in range(0, S, Q_BLOCK):
        end = start + Q_BLOCK
        z = jnp.einsum('bhqd,bhkd->bhqk', q[:, :, start:end], k[:, :, :end]).astype(jnp.float32) * scale
        causal = jnp.arange(end)[None, :] < jnp.arange(start, end)[:, None]
        log_fail = jnp.where(causal, jax.nn.log_sigmoid(-z), 0.0)
        later = lax.cumsum(log_fail, axis=3, reverse=True) - log_fail
        w = jnp.where(causal, jnp.exp(jax.nn.log_sigmoid(z) + later), 0.0)
        outs.append(jnp.einsum('bhqk,bhkd->bhqd', w.astype(v.dtype), v[:, :, :end]))
    return jnp.concatenate(outs, axis=2)


def compress_blocks(x, pos, w1, w2):
    B_, G, S, dh = x.shape
    r = NSA_CMP_LEN // NSA_CMP_STRIDE
    n_chunk = S // NSA_CMP_STRIDE
    nc = n_chunk - r + 1
    ch = x.reshape(B_, G, n_chunk, NSA_CMP_STRIDE, dh)
    blocks = jnp.concatenate([ch[:, :, j:j + nc] for j in range(r)], axis=3) + pos
    h = jax.nn.silu(blocks.reshape(B_, G, nc, NSA_CMP_LEN * dh) @ w1)
    return h @ w2


def nsa_attention(q, kv, gates, cmp_pos, cmp_w1, cmp_w2):
    B_, G, R, S, dh = q.shape
    scale = HEAD_DIM ** -0.5
    t = jnp.arange(S)
    k_cmp, v_cmp, k_slc, v_slc, k_win, v_win = kv

    kc = compress_blocks(k_cmp, cmp_pos[0], cmp_w1[0], cmp_w2[0])
    vc = compress_blocks(v_cmp, cmp_pos[1], cmp_w1[1], cmp_w2[1])
    nc = kc.shape[2]
    cmp_start = jnp.arange(nc) * NSA_CMP_STRIDE
    cmp_last = cmp_start + NSA_CMP_LEN - 1
    s_c = jnp.einsum('bgrtd,bgnd->bgrtn', q, kc).astype(jnp.float32) * scale
    p_c = masked_softmax(s_c, cmp_last[None, :] <= t[:, None])
    o_cmp = jnp.einsum('bgrtn,bgnd->bgrtd', p_c.astype(q.dtype), vc)

    ns = S // NSA_SLC_BLOCK
    slc_start = jnp.arange(ns) * NSA_SLC_BLOCK
    overlap = ((cmp_start[:, None] < slc_start[None, :] + NSA_SLC_BLOCK)
               & (cmp_start[:, None] + NSA_CMP_LEN > slc_start[None, :])).astype(jnp.float32)
    imp = jnp.einsum('bgrtn,ns->bgts', p_c, overlap)
    blk = jnp.arange(ns)[None, :]
    cur = (t // NSA_SLC_BLOCK)[:, None]
    valid = slc_start[None, :] <= t[:, None]
    forced = (blk == 0) | (blk == cur) | (blk == cur - 1)
    imp = jnp.where(valid, jnp.where(forced, SEL_FORCE, imp), NEG)
    n_top = min(NSA_SLC_TOPK, ns)
    _, sel = lax.top_k(imp, n_top)

    ksb = k_slc.reshape(B_, G, ns, NSA_SLC_BLOCK, dh)
    vsb = v_slc.reshape(B_, G, ns, NSA_SLC_BLOCK, dh)
    nq = S // SLC_Q_BLOCK
    bi = jnp.arange(B_)[:, None, None, None]
    gi = jnp.arange(G)[None, :, None, None]
    n_keys = n_top * NSA_SLC_BLOCK

    def slc_block(args):
        qc, sc, tc = args
        kg = ksb[bi, gi, sc].reshape(B_, G, SLC_Q_BLOCK, n_keys, dh)
        vg = vsb[bi, gi, sc].reshape(B_, G, SLC_Q_BLOCK, n_keys, dh)
        kpos = (sc[..., None] * NSA_SLC_BLOCK + jnp.arange(NSA_SLC_BLOCK)).reshape(B_, G, SLC_Q_BLOCK, n_keys)
        s = jnp.einsum('bgrqd,bgqkd->bgrqk', qc, kg).astype(jnp.float32) * scale
        p = masked_softmax(s, (kpos <= tc[:, None])[:, :, None])
        return jnp.einsum('bgrqk,bgqkd->bgrqd', p.astype(qc.dtype), vg)

    q_ch = jnp.moveaxis(q.reshape(B_, G, R, nq, SLC_Q_BLOCK, dh), 3, 0)
    sel_ch = jnp.moveaxis(sel.reshape(B_, G, nq, SLC_Q_BLOCK, n_top), 2, 0)
    o_slc = lax.map(slc_block, (q_ch, sel_ch, t.reshape(nq, SLC_Q_BLOCK)))
    o_slc = jnp.moveaxis(o_slc, 0, 3).reshape(B_, G, R, S, dh)

    span = NSA_WINDOW + Q_BLOCK
    pad = ((0, 0), (0, 0), (NSA_WINDOW, 0), (0, 0))
    kp = jnp.pad(k_win, pad)
    vp = jnp.pad(v_win, pad)

    def win_block(i):
        start = i * Q_BLOCK
        qb = lax.dynamic_slice_in_dim(q, start, Q_BLOCK, axis=3)
        kb = lax.dynamic_slice_in_dim(kp, start, span, axis=2)
        vb = lax.dynamic_slice_in_dim(vp, start, span, axis=2)
        tq = start + jnp.arange(Q_BLOCK)
        kpos = start - NSA_WINDOW + jnp.arange(span)
        diff = tq[:, None] - kpos[None, :]
        mask = (diff >= 0) & (diff < NSA_WINDOW) & (kpos[None, :] >= 0)
        s = jnp.einsum('bgrqd,bgkd->bgrqk', qb, kb).astype(jnp.float32) * scale
        p = masked_softmax(s, mask)
        return jnp.einsum('bgrqk,bgkd->bgrqd', p.astype(q.dtype), vb)

    o_win = lax.map(win_block, jnp.arange(S // Q_BLOCK))
    o_win = jnp.moveaxis(o_win, 0, 3).reshape(B_, G, R, S, dh)

    g = jax.nn.sigmoid(gates)
    return g[..., 0:1] * o_cmp + g[..., 1:2] * o_slc + g[..., 2:3] * o_win


def parallel_attention_mixer(h, w_in, cmp_pos, cmp_w1, cmp_w2, w_out):
    B_, S, _ = h.shape
    proj = h @ w_in
    sizes = [SB_W, SB_W, SB_W, NSA_Q_W, NSA_KV_W]
    cuts = [int(v) for v in np.cumsum(sizes)]
    sb_q, sb_k, sb_v, nsa_q, nsa_kv, nsa_g = jnp.split(proj, cuts, axis=-1)
    heads = lambda a: a.reshape(B_, S, SB_HEADS, HEAD_DIM).transpose(0, 2, 1, 3)
    o_sb = stick_breaking_attention(heads(sb_q), heads(sb_k), heads(sb_v))
    o_sb = o_sb.transpose(0, 2, 1, 3).reshape(B_, S, SB_W)
    q = nsa_q.reshape(B_, S, NSA_KV_GROUPS, NSA_REP, HEAD_DIM).transpose(0, 2, 3, 1, 4)
    kv = nsa_kv.reshape(B_, S, 6, NSA_KV_GROUPS, HEAD_DIM).transpose(2, 0, 3, 1, 4)
    gates = nsa_g.reshape(B_, S, NSA_KV_GROUPS, NSA_REP, 3).transpose(0, 2, 3, 1, 4)
    o_nsa = nsa_attention(q, kv, gates, cmp_pos, cmp_w1, cmp_w2)
    o_nsa = o_nsa.transpose(0, 3, 1, 2, 4).reshape(B_, S, NSA_Q_W)
    return jnp.concatenate([o_sb, o_nsa], axis=-1) @ w_out


def short_conv_mixer(h, w_in, conv_taps, w_out):
    gate_b, gate_c, u = jnp.split(h @ w_in, 3, axis=-1)
    z = gate_c * u
    zc = lax.conv_general_dilated(z, conv_taps[:, None, :], window_strides=(1,),
                                  padding=[(CONV_WIDTH - 1, 0)],
                                  dimension_numbers=('NWC', 'WIO', 'NWC'),
                                  feature_group_count=z.shape[-1])
    return (gate_b * zc) @ w_out


def swiglu(h, w1, w3, w2):
    return (jax.nn.silu(h @ w1) * (h @ w3)) @ w2


def moe_swiglu(h, w_router, b_router, w1, w3, w2):
    B_, S, D = h.shape
    T = B_ * S
    xf = h.reshape(T, D)
    logits = (xf @ w_router + b_router).astype(jnp.float32)
    top_val, top_idx = lax.top_k(logits, TOP_K)
    gate = jax.nn.softmax(top_val, axis=-1)
    n_assign = T * TOP_K
    e_flat = top_idx.reshape(-1)
    tok_flat = jnp.arange(n_assign) // TOP_K
    g_flat = gate.reshape(-1)
    order = jnp.argsort(e_flat)
    e_sorted = e_flat[order]
    counts = jnp.bincount(e_flat, length=N_EXPERTS)
    starts = jnp.cumsum(counts) - counts
    padded = (counts + MOE_ROW_BLOCK - 1) // MOE_ROW_BLOCK * MOE_ROW_BLOCK
    pends = jnp.cumsum(padded)
    pstarts = pends - padded
    dest = pstarts[e_sorted] + (jnp.arange(n_assign) - starts[e_sorted])
    n_blocks = -(-n_assign // MOE_ROW_BLOCK) + N_EXPERTS
    rows = n_blocks * MOE_ROW_BLOCK
    buf = jnp.zeros((rows, D), h.dtype).at[dest].set(xf[tok_flat[order]])
    row_tok = jnp.zeros((rows,), jnp.int32).at[dest].set(tok_flat[order].astype(jnp.int32))
    row_w = jnp.zeros((rows,), jnp.float32).at[dest].set(g_flat[order])
    blk_e = jnp.minimum(jnp.searchsorted(pends, jnp.arange(n_blocks) * MOE_ROW_BLOCK, side='right'), N_EXPERTS - 1)

    def expert_rows(args):
        xb, e = args
        return swiglu(xb, w1[e], w3[e], w2[e])

    yb = lax.map(expert_rows, (buf.reshape(n_blocks, MOE_ROW_BLOCK, D), blk_e))
    y = yb.reshape(rows, D) * row_w[:, None].astype(yb.dtype)
    return jax.ops.segment_sum(y, row_tok, num_segments=T).reshape(B_, S, D)


def setup_inputs(seed: int = 0) -> dict:
    key = jax.random.key(seed)
    ks = iter(jax.random.split(key, 40))
    nrm = lambda shape, scale: jax.random.normal(next(ks), shape, jnp.float32) * scale
    n_even = (DEPTH + 1) // 2
    n_odd = DEPTH // 2
    D = D_MODEL
    return {
        'x': nrm((BATCH, SEQ, D), 1.0),
        'c': nrm((BATCH, D), 1.0),
        'ada_w': nrm((DEPTH, D, 6 * D), 0.02),
        'ada_b': nrm((DEPTH, 6 * D), 0.02),
        'ln_g': 1.0 + nrm((DEPTH, 2, D), 0.02),
        'ln_b': nrm((DEPTH, 2, D), 0.02),
        'mix_w_in': nrm((n_even, D, MIX0_IN), D ** -0.5),
        'cmp_pos': nrm((n_even, 2, NSA_CMP_LEN, HEAD_DIM), 0.1),
        'cmp_w1': nrm((n_even, 2, NSA_CMP_LEN * HEAD_DIM, NSA_CMP_HIDDEN), (NSA_CMP_LEN * HEAD_DIM) ** -0.5),
        'cmp_w2': nrm((n_even, 2, NSA_CMP_HIDDEN, HEAD_DIM), NSA_CMP_HIDDEN ** -0.5),
        'mix_w_out': nrm((n_even, MIX0_OUT, D), MIX0_OUT ** -0.5 * DN_BETA),
        'ffn_w1': nrm((n_even, D, D_FF), D ** -0.5),
        'ffn_w3': nrm((n_even, D, D_FF), D ** -0.5),
        'ffn_w2': nrm((n_even, D_FF, D), D_FF ** -0.5 * DN_BETA),
        'conv_w_in': nrm((n_odd, D, 3 * D), D ** -0.5),
        'conv_taps': nrm((n_odd, CONV_WIDTH, D), CONV_WIDTH ** -0.5),
        'conv_w_out': nrm((n_odd, D, D), D ** -0.5 * DN_BETA),
        'router_w': nrm((n_odd, D, N_EXPERTS), D ** -0.5),
        'router_b': nrm((n_odd, N_EXPERTS), 0.01),
        'exp_w1': nrm((n_odd, N_EXPERTS, D, EXPERT_D_FF), D ** -0.5),
        'exp_w3': nrm((n_odd, N_EXPERTS, D, EXPERT_D_FF), D ** -0.5),
        'exp_w2': nrm((n_odd, N_EXPERTS, EXPERT_D_FF, D), EXPERT_D_FF ** -0.5 * DN_BETA),
    }


def reference(x, c, ada_w, ada_b, ln_g, ln_b, mix_w_in, cmp_pos, cmp_w1, cmp_w2, mix_w_out,
              ffn_w1, ffn_w3, ffn_w2, conv_w_in, conv_taps, conv_w_out, router_w, router_b,
              exp_w1, exp_w3, exp_w2):
    cond = jax.nn.silu(c)
    for i in range(DEPTH):
        j = i // 2
        mod = cond @ ada_w[i] + ada_b[i]
        sh1, sc1, g1, sh2, sc2, g2 = jnp.split(mod[:, None, :], 6, axis=-1)
        h = x * (1 + sc1) + sh1
        if i % 2 == 0:
            y = parallel_attention_mixer(h, mix_w_in[j], cmp_pos[j], cmp_w1[j], cmp_w2[j], mix_w_out[j])
        else:
            y = short_conv_mixer(h, conv_w_in[j], conv_taps[j], conv_w_out[j])
        x = layer_norm(DN_ALPHA * x + (1 + g1) * y, ln_g[i, 0], ln_b[i, 0])
        h = x * (1 + sc2) + sh2
        if i % 2 == 0:
            y = swiglu(h, ffn_w1[j], ffn_w3[j], ffn_w2[j])
        else:
            y = moe_swiglu(h, router_w[j], router_b[j], exp_w1[j], exp_w3[j], exp_w2[j])
        x = layer_norm(DN_ALPHA * x + (1 + g2) * y, ln_g[i, 1], ln_b[i, 1])
    return x
```

```python
import functools

import numpy as np
import jax
import jax.numpy as jnp
from jax import lax
from jax.experimental import pallas as pl
from jax.experimental.pallas import tpu as pltpu

HEAD_DIM = 64
SB_HEADS = 8
NSA_HEADS = 8
NSA_KV_GROUPS = 2
NSA_REP = NSA_HEADS // NSA_KV_GROUPS
NSA_CMP_LEN = 32
NSA_CMP_STRIDE = 16
NSA_SLC_BLOCK = 64
NSA_SLC_TOPK = 16
NSA_WINDOW = 512
SB_W = SB_HEADS * HEAD_DIM
NSA_Q_W = NSA_HEADS * HEAD_DIM
N_EXPERTS = 8
MOE_ROW_BLOCK = 512
DEPTH = 2
LN_EPS = 1e-5
DN_ALPHA = (2 * DEPTH) ** 0.25
NEG = -1e30
SEL_FORCE = 1e30
QK_SCALE = HEAD_DIM ** -0.5

LANES = 128
VMEM_LIMIT = 56 * 1024 * 1024

C_SB = 0
C_NQ = 3 * SB_W
C_CMP = C_NQ + NSA_Q_W
C_SW = C_CMP + 4 * HEAD_DIM
C_GATE = C_SW + 4 * 2 * HEAD_DIM
C_END = C_GATE + 2 * LANES

_F32 = jnp.float32
_BF16 = jnp.bfloat16


def _cparams(sem, vmem=VMEM_LIMIT):
    return pltpu.CompilerParams(dimension_semantics=sem, vmem_limit_bytes=vmem)


def _dot(a, b):
    return jnp.dot(a, b, preferred_element_type=_F32)


def _dot_t(a, b):
    return lax.dot_general(a, b, (((1,), (1,)), ((), ())), preferred_element_type=_F32)


def _split_bf16(x):
    hi = x.astype(_BF16)
    lo = (x - hi.astype(_F32)).astype(_BF16)
    return hi, lo


def _silu(x):
    return x / (1.0 + jnp.exp(-x))


def _layer_norm(r, g, b):
    mu = jnp.mean(r, axis=-1, keepdims=True)
    d = r - mu
    var = jnp.mean(d * d, axis=-1, keepdims=True)
    return d * lax.rsqrt(var + LN_EPS) * g + b


def _ada_kernel(c_ref, w_ref, b_ref, o_ref):
    cond = _silu(c_ref[...])
    o_ref[0] = jnp.dot(cond, w_ref[0], preferred_element_type=_F32,
                       precision=lax.Precision.HIGHEST) + b_ref[0]


def _ada_mod(c, ada_w, ada_b):
    depth, d, n = ada_w.shape
    bsz = c.shape[0]
    tn = n // 6
    return pl.pallas_call(
        _ada_kernel,
        grid=(depth, n // tn),
        in_specs=[pl.BlockSpec((bsz, d), lambda i, j: (0, 0)),
                  pl.BlockSpec((1, d, tn), lambda i, j: (i, 0, j)),
                  pl.BlockSpec((1, 1, tn), lambda i, j: (i, 0, j))],
        out_specs=pl.BlockSpec((1, bsz, tn), lambda i, j: (i, 0, j)),
        out_shape=jax.ShapeDtypeStruct((depth, bsz, n), _F32),
        compiler_params=_cparams(("parallel", "parallel")),
        name="ada_mod",
    )(c, ada_w, ada_b.reshape(depth, 1, n))


def _inproj_kernel(x_ref, sc_ref, sh_ref, w_ref, sb_ref, nq_ref, cmp_ref, sw_ref, gate_ref):
    h = (x_ref[0] * (1.0 + sc_ref[0]) + sh_ref[0]).astype(_BF16)
    sb_ref[0] = _dot(h, w_ref[:, C_SB:C_NQ]).astype(_BF16)
    nq_ref[0] = _dot(h, w_ref[:, C_NQ:C_CMP]).astype(_BF16)
    kvc = _dot(h, w_ref[:, C_CMP:C_SW])
    for j in range(4):
        cmp_ref[0, j] = kvc[:, j * HEAD_DIM:(j + 1) * HEAD_DIM]
    lane = lax.broadcasted_iota(jnp.int32, (1, LANES), 1)
    first = lane < HEAD_DIM
    for kind in range(4):
        a = _dot(h, w_ref[:, C_SW + kind * LANES:C_SW + (kind + 1) * LANES])
        r = pltpu.roll(a, HEAD_DIM, 1)
        g0 = jnp.where(first, a, r).astype(_BF16)
        g1 = jnp.where(first, r, a).astype(_BF16)
        sw_ref[0, :, (2 * kind) * 256:(2 * kind + 1) * 256] = jnp.concatenate([g0, g0], axis=1)
        sw_ref[0, :, (2 * kind + 1) * 256:(2 * kind + 2) * 256] = jnp.concatenate([g1, g1], axis=1)
    gate_ref[0] = _dot(h, w_ref[:, C_GATE:C_END])


def _inproj(x, sc, sh, w):
    bsz, s, d = x.shape
    tm = 512
    row = lambda b, i: (b, i, 0)
    return pl.pallas_call(
        _inproj_kernel,
        grid=(bsz, s // tm),
        in_specs=[pl.BlockSpec((1, tm, d), row),
                  pl.BlockSpec((1, 1, d), lambda b, i: (b, 0, 0)),
                  pl.BlockSpec((1, 1, d), lambda b, i: (b, 0, 0)),
                  pl.BlockSpec((d, C_END), lambda b, i: (0, 0))],
        out_specs=[pl.BlockSpec((1, tm, 3 * SB_W), row),
                   pl.BlockSpec((1, tm, NSA_Q_W), row),
                   pl.BlockSpec((1, 4, tm, HEAD_DIM), lambda b, i: (b, 0, i, 0)),
                   pl.BlockSpec((1, tm, 2048), row),
                   pl.BlockSpec((1, tm, 2 * LANES), row)],
        out_shape=[jax.ShapeDtypeStruct((bsz, s, 3 * SB_W), _BF16),
                   jax.ShapeDtypeStruct((bsz, s, NSA_Q_W), _BF16),
                   jax.ShapeDtypeStruct((bsz, 4, s, HEAD_DIM), _F32),
                   jax.ShapeDtypeStruct((bsz, s, 2048), _BF16),
                   jax.ShapeDtypeStruct((bsz, s, 2 * LANES), _F32)],
        compiler_params=_cparams(("parallel", "parallel")),
        name="l0_inproj",
    )(x, sc, sh, w)


SB_TQ = 128


def _sb_kernel(q_ref, k_ref, v_ref, tri_ref, o_ref):
    tq = SB_TQ
    qi = pl.program_id(2)
    q = q_ref[0]
    lane = lax.broadcasted_iota(jnp.int32, (1, LANES), 1)
    first = lane < HEAD_DIM
    zero = jnp.zeros_like(q)
    qh = (jnp.where(first, q, zero), jnp.where(first, zero, q))
    tri = tri_ref[...]
    row = lax.broadcasted_iota(jnp.int32, (tq, tq), 0)
    col = lax.broadcasted_iota(jnp.int32, (tq, tq), 1)
    causal = col < row

    def block(j, carry, diagonal):
        kb = k_ref[0, pl.ds(pl.multiple_of(j * tq, tq), tq), :]
        vb = v_ref[0, pl.ds(pl.multiple_of(j * tq, tq), tq), :]
        out = []
        for h in range(2):
            acc, run = carry[h]
            z = _dot_t(qh[h], kb)
            lf = -(jnp.maximum(z, 0.0) + jnp.log1p(jnp.exp(-jnp.abs(z))))
            if diagonal:
                lf = jnp.where(causal, lf, 0.0)
            hi, lo = _split_bf16(lf)
            incl = _dot(jnp.concatenate([hi, lo], axis=1), tri)
            w = jnp.exp(z + incl + run)
            if diagonal:
                w = jnp.where(causal, w, 0.0)
            acc = acc + _dot(w.astype(_BF16), vb)
            run = run + jnp.sum(lf, axis=1, keepdims=True)
            out.append((acc, run))
        return tuple(out)

    init = tuple((jnp.zeros((tq, LANES), _F32), jnp.zeros((tq, 1), _F32)) for _ in range(2))
    carry = block(qi, init, True)
    carry = lax.fori_loop(0, qi, lambda jj, c: block(qi - 1 - jj, c, False), carry)
    o_ref[0] = jnp.where(first, carry[0][0], carry[1][0]).astype(o_ref.dtype)


def _sb_attention(sb):
    bsz, s, _ = sb.shape
    tq = SB_TQ
    npair = SB_W // LANES
    tri = np.tile((np.arange(tq)[:, None] >= np.arange(tq)[None, :]).astype(np.float32), (2, 1))
    return pl.pallas_call(
        _sb_kernel,
        grid=(bsz, npair, s // tq),
        in_specs=[pl.BlockSpec((1, tq, LANES), lambda b, p, i: (b, i, p)),
                  pl.BlockSpec((1, s, LANES), lambda b, p, i: (b, 0, npair + p)),
                  pl.BlockSpec((1, s, LANES), lambda b, p, i: (b, 0, 2 * npair + p)),
                  pl.BlockSpec((2 * tq, tq), lambda b, p, i: (0, 0))],
        out_specs=pl.BlockSpec((1, tq, LANES), lambda b, p, i: (b, i, p)),
        out_shape=jax.ShapeDtypeStruct((bsz, s, SB_W), _BF16),
        compiler_params=_cparams(("parallel", "parallel", "arbitrary")),
        name="sb_attention",
    )(sb, sb, sb, jnp.asarray(tri, _BF16))


def _compress_kernel(x_ref, pos_ref, w1_ref, w2_ref, o_ref):
    half = w1_ref.shape[1] // 2
    ch = x_ref[0, 0]
    top = _dot((ch + pos_ref[0, 0:1, :]).astype(_BF16), w1_ref[0, :half, :])
    bot = _dot((ch + pos_ref[0, 1:2, :]).astype(_BF16), w1_ref[0, half:, :])
    n = ch.shape[0]
    pre = top + pltpu.roll(bot, n - 1, 0)
    o_ref[0, 0] = _dot(_silu(pre).astype(_BF16), w2_ref[0]).astype(o_ref.dtype)


def _compress(kvc, pos, w1, w2rep):
    bsz, _, s, dh = kvc.shape
    n_chunk = s // NSA_CMP_STRIDE
    cw = NSA_CMP_STRIDE * dh
    x = kvc.reshape(bsz, 4, n_chunk, cw)
    hid = w1.shape[-1]
    return pl.pallas_call(
        _compress_kernel,
        grid=(bsz, 4),
        in_specs=[pl.BlockSpec((1, 1, n_chunk, cw), lambda b, j: (b, j, 0, 0)),
                  pl.BlockSpec((1, 2, cw), lambda b, j: (j // 2, 0, 0)),
                  pl.BlockSpec((1, 2 * cw, hid), lambda b, j: (j // 2, 0, 0)),
                  pl.BlockSpec((1, hid, 4 * dh), lambda b, j: (j // 2, 0, 0))],
        out_specs=pl.BlockSpec((1, 1, n_chunk, 4 * dh), lambda b, j: (b, j, 0, 0)),
        out_shape=jax.ShapeDtypeStruct((bsz, 4, n_chunk, 4 * dh), _BF16),
        compiler_params=_cparams(("parallel", "parallel")),
        name="nsa_compress",
    )(x, pos, w1, w2rep)


NSA_TQ = 128
NSA_TK = 256
WIN_SPAN = NSA_WINDOW + NSA_TQ


def _nsa_kernel(q_ref, kc_ref, vc_ref, ks_ref, vs_ref, kw_ref, vw_ref, gate_ref, ov_ref, ex_ref, o_ref):
    tq, tk = NSA_TQ, NSA_TK
    qi = pl.program_id(2)
    t0 = qi * tq
    q = q_ref[0]
    head_of_lane = lax.broadcasted_iota(jnp.int32, (1, 4 * HEAD_DIM), 1) // HEAD_DIM
    zero = jnp.zeros_like(q)
    qs = jnp.concatenate([jnp.where(head_of_lane == r, q, zero) for r in range(NSA_REP)], axis=0)
    t_row = t0 + lax.broadcasted_iota(jnp.int32, (tq, 1), 0)
    t_row4 = jnp.concatenate([t_row] * NSA_REP, axis=0)

    def unstack(o):
        out = jnp.zeros((tq, 4 * HEAD_DIM), _F32)
        for r in range(NSA_REP):
            out = jnp.where(head_of_lane == r, o[r * tq:(r + 1) * tq], out)
        return out

    n_cmp = kc_ref.shape[2]
    s_c = _dot_t(qs, kc_ref[0, 0])
    cmp_last = lax.broadcasted_iota(jnp.int32, (1, n_cmp), 1) * NSA_CMP_STRIDE + (NSA_CMP_LEN - 1)
    mask_c = cmp_last <= t_row4
    sm = jnp.where(mask_c, s_c, NEG)
    e_c = jnp.where(mask_c, jnp.exp(sm - jnp.max(sm, axis=1, keepdims=True)), 0.0)
    l_c = jnp.sum(e_c, axis=1, keepdims=True)
    p_c = e_c / jnp.where(l_c == 0.0, 1.0, l_c)
    o_cmp = unstack(_dot(p_c.astype(_BF16), vc_ref[0, 0]))

    p_sum = p_c[0:tq] + p_c[tq:2 * tq] + p_c[2 * tq:3 * tq] + p_c[3 * tq:4 * tq]
    hi, lo = _split_bf16(p_sum)
    imp = _dot(jnp.concatenate([hi, lo], axis=1), ov_ref[...])
    n_slc = ex_ref.shape[1] // NSA_SLC_BLOCK
    blk = lax.broadcasted_iota(jnp.int32, (1, LANES), 1)
    cur = t_row // NSA_SLC_BLOCK
    valid = blk <= cur
    forced = (blk == 0) | (blk == cur) | (blk == cur - 1)
    val = jnp.where(valid, jnp.where(forced, SEL_FORCE, imp), NEG)
    val = jnp.where(blk < n_slc, val, -3e38)
    rank = jnp.zeros((tq, LANES), _F32)
    for i in range(n_slc):
        vi = val[:, i:i + 1]
        ahead = (vi > val) | ((vi == val) & (blk > i))
        rank = rank + jnp.where(ahead, 1.0, 0.0)
    sel = jnp.where(rank < float(min(NSA_SLC_TOPK, n_slc)), 1.0, 0.0).astype(_BF16)

    def slc_block(j, carry, diagonal):
        m, l, acc = carry
        off = pl.multiple_of(j * tk, tk)
        kb = ks_ref[0, pl.ds(off, tk), :]
        vb = vs_ref[0, pl.ds(off, tk), :]
        s = _dot_t(qs, kb)
        chosen = _dot(sel, ex_ref[:, pl.ds(off, tk)])
        bias = (chosen - 1.0) * 1e30
        if diagonal:
            kpos = j * tk + lax.broadcasted_iota(jnp.int32, (1, tk), 1)
            bias = jnp.where(kpos <= t_row, bias, NEG)
        s = s + jnp.concatenate([bias] * NSA_REP, axis=0)
        m_new = jnp.maximum(m, jnp.max(s, axis=1, keepdims=True))
        alpha = jnp.exp(m - m_new)
        p = jnp.exp(s - m_new)
        l = alpha * l + jnp.sum(p, axis=1, keepdims=True)
        acc = alpha * acc + _dot(p.astype(_BF16), vb)
        return m_new, l, acc

    jd = t0 // tk
    init = (jnp.full((4 * tq, 1), NEG, _F32), jnp.zeros((4 * tq, 1), _F32),
            jnp.zeros((4 * tq, 4 * HEAD_DIM), _F32))
    carry = slc_block(jd, init, True)
    m_s, l_s, acc_s = lax.fori_loop(0, jd, lambda j, c: slc_block(j, c, False), carry)
    o_slc = unstack(acc_s / l_s)

    start = pl.multiple_of(jnp.maximum(t0 - NSA_WINDOW, 0), tq)
    s_w = _dot_t(qs, kw_ref[0, pl.ds(start, WIN_SPAN), :])
    kpos = start + lax.broadcasted_iota(jnp.int32, (1, WIN_SPAN), 1)
    diff = t_row4 - kpos
    mask_w = (diff >= 0) & (diff < NSA_WINDOW)
    sm = jnp.where(mask_w, s_w, NEG)
    e_w = jnp.exp(sm - jnp.max(sm, axis=1, keepdims=True))
    p_w = e_w / jnp.sum(e_w, axis=1, keepdims=True)
    o_win = unstack(_dot(p_w.astype(_BF16), vw_ref[0, pl.ds(start, WIN_SPAN), :]))

    g = 1.0 / (1.0 + jnp.exp(-gate_ref[0]))
    out = jnp.zeros((tq, 4 * HEAD_DIM), _F32)
    for branch, o_b in enumerate((o_cmp, o_slc, o_win)):
        gm = jnp.zeros((tq, 4 * HEAD_DIM), _F32)
        for r in range(NSA_REP):
            c = 3 * r + branch
            gm = jnp.where(head_of_lane == r, g[:, c:c + 1], gm)
        out = out + gm * o_b
    o_ref[0] = out.astype(o_ref.dtype)


def _nsa_attention(nq, cmp_kv, sw, gates):
    bsz, s, _ = nq.shape
    tq = NSA_TQ
    n_cmp = cmp_kv.shape[2]
    n_slc = s // NSA_SLC_BLOCK
    gw = 4 * HEAD_DIM
    cmp_start = np.arange(n_cmp) * NSA_CMP_STRIDE
    slc_start = np.arange(LANES) * NSA_SLC_BLOCK
    ov = ((cmp_start[:, None] < slc_start[None, :] + NSA_SLC_BLOCK)
          & (cmp_start[:, None] + NSA_CMP_LEN > slc_start[None, :])
          & (np.arange(LANES)[None, :] < n_slc)).astype(np.float32)
    ov[n_cmp - 1:] = 0.0
    ov2 = np.concatenate([ov, ov], axis=0)
    ex = (np.arange(LANES)[:, None] == (np.arange(s)[None, :] // NSA_SLC_BLOCK)).astype(np.float32)
    kv_spec = lambda kind: pl.BlockSpec((1, s, gw), lambda b, g, i, kind=kind: (b, 0, 2 * kind + g))
    return pl.pallas_call(
        _nsa_kernel,
        grid=(bsz, NSA_KV_GROUPS, s // tq),
        in_specs=[pl.BlockSpec((1, tq, gw), lambda b, g, i: (b, i, g)),
                  pl.BlockSpec((1, 1, n_cmp, gw), lambda b, g, i: (b, g, 0, 0)),
                  pl.BlockSpec((1, 1, n_cmp, gw), lambda b, g, i: (b, 2 + g, 0, 0)),
                  kv_spec(0), kv_spec(1), kv_spec(2), kv_spec(3),
                  pl.BlockSpec((1, tq, LANES), lambda b, g, i: (b, i, g)),
                  pl.BlockSpec((2 * n_cmp, LANES), lambda b, g, i: (0, 0)),
                  pl.BlockSpec((LANES, s), lambda b, g, i: (0, 0))],
        out_specs=pl.BlockSpec((1, tq, gw), lambda b, g, i: (b, i, g)),
        out_shape=jax.ShapeDtypeStruct((bsz, s, NSA_Q_W), _BF16),
        compiler_params=_cparams(("parallel", "parallel", "arbitrary")),
        name="nsa_attention",
    )(nq, cmp_kv, cmp_kv, sw, sw, sw, sw, gates, jnp.asarray(ov2, _BF16), jnp.asarray(ex, _BF16))


def _outproj_kernel(a_ref, b_ref, wa_ref, wb_ref, x_ref, g_ref, lg_ref, lb_ref, o_ref):
    y = _dot(a_ref[0], wa_ref[...]) + _dot(b_ref[0], wb_ref[...])
    r = DN_ALPHA * x_ref[0] + (1.0 + g_ref[0]) * y
    o_ref[0] = _layer_norm(r, lg_ref[...], lb_ref[...])


def _outproj_ln(o_sb, o_nsa, w_out, x, gate, ln_g, ln_b):
    bsz, s, d = x.shape
    tm = 512
    row = lambda b, i: (b, i, 0)
    vec = pl.BlockSpec((1, d), lambda b, i: (0, 0))
    return pl.pallas_call(
        _outproj_kernel,
        grid=(bsz, s // tm),
        in_specs=[pl.BlockSpec((1, tm, SB_W), row), pl.BlockSpec((1, tm, NSA_Q_W), row),
                  pl.BlockSpec((SB_W, d), lambda b, i: (0, 0)),
                  pl.BlockSpec((NSA_Q_W, d), lambda b, i: (1, 0)),
                  pl.BlockSpec((1, tm, d), row),
                  pl.BlockSpec((1, 1, d), lambda b, i: (b, 0, 0)), vec, vec],
        out_specs=pl.BlockSpec((1, tm, d), row),
        out_shape=jax.ShapeDtypeStruct((bsz, s, d), _F32),
        compiler_params=_cparams(("parallel", "parallel")),
        name="l0_outproj_ln",
    )(o_sb, o_nsa, w_out, w_out, x, gate, ln_g.reshape(1, d), ln_b.reshape(1, d))


def _ffn_kernel(x_ref, sc_ref, sh_ref, g_ref, w1_ref, w3_ref, w2_ref, lg_ref, lb_ref, o_ref, h_ref, acc_ref):
    c = pl.program_id(2)

    @pl.when(c == 0)
    def _():
        h_ref[...] = (x_ref[0] * (1.0 + sc_ref[0]) + sh_ref[0]).astype(_BF16)
        acc_ref[...] = jnp.zeros_like(acc_ref)

    h = h_ref[...]
    a = _silu(_dot(h, w1_ref[...])) * _dot(h, w3_ref[...])
    acc_ref[...] += _dot(a.astype(_BF16), w2_ref[...])

    @pl.when(c == pl.num_programs(2) - 1)
    def _():
        r = DN_ALPHA * x_ref[0] + (1.0 + g_ref[0]) * acc_ref[...]
        o_ref[0] = _layer_norm(r, lg_ref[...], lb_ref[...])


def _ffn_ln(x, sc, sh, gate, w1, w3, w2, ln_g, ln_b):
    bsz, s, d = x.shape
    ff = w1.shape[1]
    tm, tf = 1024, 256
    row = lambda b, i, c: (b, i, 0)
    mod = pl.BlockSpec((1, 1, d), lambda b, i, c: (b, 0, 0))
    vec = pl.BlockSpec((1, d), lambda b, i, c: (0, 0))
    return pl.pallas_call(
        _ffn_kernel,
        grid=(bsz, s // tm, ff // tf),
        in_specs=[pl.BlockSpec((1, tm, d), row), mod, mod, mod,
                  pl.BlockSpec((d, tf), lambda b, i, c: (0, c)),
                  pl.BlockSpec((d, tf), lambda b, i, c: (0, c)),
                  pl.BlockSpec((tf, d), lambda b, i, c: (c, 0)), vec, vec],
        out_specs=pl.BlockSpec((1, tm, d), row),
        out_shape=jax.ShapeDtypeStruct((bsz, s, d), _F32),
        scratch_shapes=[pltpu.VMEM((tm, d), _BF16), pltpu.VMEM((tm, d), _F32)],
        compiler_params=_cparams(("parallel", "parallel", "arbitrary")),
        name="l0_ffn_ln",
    )(x, sc, sh, gate, w1, w3, w2, ln_g.reshape(1, d), ln_b.reshape(1, d))


def _conv_kernel(x_ref, sc_ref, sh_ref, g_ref, win_ref, taps_ref, wout_ref, lg_ref, lb_ref, o_ref, tail_ref):
    ts, d = x_ref.shape[1], x_ref.shape[2]

    @pl.when(pl.program_id(1) == 0)
    def _():
        tail_ref[...] = jnp.zeros_like(tail_ref)

    x = x_ref[0]
    h = (x * (1.0 + sc_ref[0]) + sh_ref[0]).astype(_BF16)
    z = _dot(h, win_ref[:, d:2 * d]) * _dot(h, win_ref[:, 2 * d:3 * d])
    row = lax.broadcasted_iota(jnp.int32, (ts, 1), 0)
    prev1 = tail_ref[7:8, :]
    prev2 = tail_ref[6:7, :]
    z1 = jnp.where(row == 0, prev1, pltpu.roll(z, 1, 0))
    z2 = jnp.where(row == 0, prev2, jnp.where(row == 1, prev1, pltpu.roll(z, 2, 0)))
    zc = taps_ref[0:1, :] * z2 + taps_ref[1:2, :] * z1 + taps_ref[2:3, :] * z
    tail_ref[...] = z[ts - 8:ts, :]
    y = _dot((_dot(h, win_ref[:, 0:d]) * zc).astype(_BF16), wout_ref[...])
    r = DN_ALPHA * x + (1.0 + g_ref[0]) * y
    o_ref[0] = _layer_norm(r, lg_ref[...], lb_ref[...])


def _conv_ln(x, sc, sh, gate, w_in, taps, w_out, ln_g, ln_b):
    bsz, s, d = x.shape
    ts = 512
    row = lambda b, i: (b, i, 0)
    mod = pl.BlockSpec((1, 1, d), lambda b, i: (b, 0, 0))
    vec = pl.BlockSpec((1, d), lambda b, i: (0, 0))
    return pl.pallas_call(
        _conv_kernel,
        grid=(bsz, s // ts),
        in_specs=[pl.BlockSpec((1, ts, d), row), mod, mod, mod,
                  pl.BlockSpec((d, 3 * d), lambda b, i: (0, 0)),
                  pl.BlockSpec((taps.shape[0], d), lambda b, i: (0, 0)),
                  pl.BlockSpec((d, d), lambda b, i: (0, 0)), vec, vec],
        out_specs=pl.BlockSpec((1, ts, d), row),
        out_shape=jax.ShapeDtypeStruct((bsz, s, d), _F32),
        scratch_shapes=[pltpu.VMEM((8, d), _F32)],
        compiler_params=_cparams(("arbitrary", "arbitrary")),
        name="l1_conv_ln",
    )(x, sc, sh, gate, w_in, taps, w_out, ln_g.reshape(1, d), ln_b.reshape(1, d))


ROUTE_TM = 512


def _router_kernel(x_ref, sc_ref, sh_ref, rw_ref, rb_ref, tri_ref, h_ref, route_ref, cnt_ref, run_ref):
    tm = ROUTE_TM

    @pl.when((pl.program_id(0) == 0) & (pl.program_id(1) == 0))
    def _():
        run_ref[...] = jnp.zeros_like(run_ref)

    h = x_ref[0] * (1.0 + sc_ref[0]) + sh_ref[0]
    h_ref[0] = h
    lane = lax.broadcasted_iota(jnp.int32, (1, LANES), 1)
    logits = jnp.dot(h, rw_ref[...], preferred_element_type=_F32,
                     precision=lax.Precision.HIGHEST) + rb_ref[...]
    logits = jnp.where(lane < N_EXPERTS, logits, -3e38)
    m1 = jnp.max(logits, axis=1, keepdims=True)
    e1 = jnp.min(jnp.where(logits == m1, lane, LANES), axis=1, keepdims=True)
    rest = jnp.where(lane == e1, -3e38, logits)
    m2 = jnp.max(rest, axis=1, keepdims=True)
    e2 = jnp.min(jnp.where(rest == m2, lane, LANES), axis=1, keepdims=True)
    ex = jnp.exp(m2 - m1)
    g1 = 1.0 / (1.0 + ex)
    g2 = ex / (1.0 + ex)
    oh1 = jnp.where(lane == e1, 1.0, 0.0)
    oh2 = jnp.where(lane == e2, 1.0, 0.0)
    oh = oh1 + oh2
    before = run_ref[...] + _dot(tri_ref[...], oh.astype(_BF16))
    p1 = jnp.sum(oh1 * before, axis=1, keepdims=True)
    p2 = jnp.sum(oh2 * before, axis=1, keepdims=True)
    run_ref[...] += jnp.sum(oh, axis=0, keepdims=True)
    cnt_ref[...] = run_ref[...]
    vals = (e1.astype(_F32), e2.astype(_F32), p1, p2, g1, g2)
    out = jnp.zeros((tm, LANES), _F32)
    for k, v in enumerate(vals):
        out = jnp.where(lane == k, v, out)
    route_ref[0] = out


def _router(x, sc, sh, rw, rb):
    bsz, s, d = x.shape
    tm = ROUTE_TM
    row = lambda b, i: (b, i, 0)
    mod = pl.BlockSpec((1, 1, d), lambda b, i: (b, 0, 0))
    tri = (np.arange(tm)[None, :] < np.arange(tm)[:, None]).astype(np.float32)
    return pl.pallas_call(
        _router_kernel,
        grid=(bsz, s // tm),
        in_specs=[pl.BlockSpec((1, tm, d), row), mod, mod,
                  pl.BlockSpec((d, LANES), lambda b, i: (0, 0)),
                  pl.BlockSpec((1, LANES), lambda b, i: (0, 0)),
                  pl.BlockSpec((tm, tm), lambda b, i: (0, 0))],
        out_specs=[pl.BlockSpec((1, tm, d), row),
                   pl.BlockSpec((1, tm, LANES), row),
                   pl.BlockSpec((1, LANES), lambda b, i: (0, 0))],
        out_shape=[jax.ShapeDtypeStruct((bsz, s, d), _F32),
                   jax.ShapeDtypeStruct((bsz, s, LANES), _F32),
                   jax.ShapeDtypeStruct((1, LANES), _F32)],
        scratch_shapes=[pltpu.VMEM((1, LANES), _F32)],
        compiler_params=_cparams(("arbitrary", "arbitrary")),
        name="l1_router",
    )(x, sc, sh, rw, rb, jnp.asarray(tri, _BF16))


EXP_TF = 256


def _row_gather(src_hbm, idx_ref, base, dst_ref, sem, n):
    def issue(r, carry):
        tok = idx_ref[base + r]
        pltpu.make_async_copy(src_hbm.at[pl.ds(tok, 1)], dst_ref.at[pl.ds(r, 1)], sem).start()
        return carry
    lax.fori_loop(0, n, issue, 0)


def _row_gather_wait(src_hbm, dst_ref, sem, n):
    pltpu.make_async_copy(src_hbm.at[pl.ds(0, n)], dst_ref, sem).wait()


def _expert_kernel(blk_e_ref, n_used_ref, row_tok_ref, h_hbm, w1_ref, w3_ref, w2_ref, y_ref, xbuf, sem):
    i = pl.program_id(0)
    rb = MOE_ROW_BLOCK
    ff = w1_ref.shape[2]

    @pl.when(i < n_used_ref[0])
    def _():
        _row_gather(h_hbm, row_tok_ref, i * rb, xbuf, sem, rb)
        _row_gather_wait(h_hbm, xbuf, sem, rb)
        xb = xbuf[...].astype(_BF16)
        acc = jnp.zeros(y_ref.shape, _F32)
        for c in range(ff // EXP_TF):
            cs = slice(c * EXP_TF, (c + 1) * EXP_TF)
            a = _silu(_dot(xb, w1_ref[0, :, cs])) * _dot(xb, w3_ref[0, :, cs])
            acc = acc + _dot(a.astype(_BF16), w2_ref[0, cs, :])
        y_ref[...] = acc

    @pl.when(i >= n_used_ref[0])
    def _():
        y_ref[...] = jnp.zeros_like(y_ref)


def _experts(blk_e, n_used, row_tok, h_flat, w1, w3, w2):
    t, d = h_flat.shape
    n_blocks = blk_e.shape[0]
    rb = MOE_ROW_BLOCK
    ff = w1.shape[2]
    grid_spec = pltpu.PrefetchScalarGridSpec(
        num_scalar_prefetch=3,
        grid=(n_blocks,),
        in_specs=[pl.BlockSpec(memory_space=pl.ANY),
                  pl.BlockSpec((1, d, ff), lambda i, be, nu, rt: (be[i], 0, 0)),
                  pl.BlockSpec((1, d, ff), lambda i, be, nu, rt: (be[i], 0, 0)),
                  pl.BlockSpec((1, ff, d), lambda i, be, nu, rt: (be[i], 0, 0))],
        out_specs=pl.BlockSpec((rb, d), lambda i, be, nu, rt: (i, 0)),
        scratch_shapes=[pltpu.VMEM((rb, d), _F32), pltpu.SemaphoreType.DMA(())],
    )
    return pl.pallas_call(
        _expert_kernel,
        grid_spec=grid_spec,
        out_shape=jax.ShapeDtypeStruct((n_blocks * rb, d), _F32),
        compiler_params=_cparams(("arbitrary",)),
        name="l1_experts",
    )(blk_e, n_used, row_tok, h_flat, w1, w3, w2)


COMB_TM = 256


def _combine_kernel(d1_ref, d2_ref, y_hbm, x_ref, route_ref, g_ref, lg_ref, lb_ref, o_ref, y1buf, y2buf, sems):
    tm = COMB_TM
    base = (pl.program_id(0) * pl.num_programs(1) + pl.program_id(1)) * tm
    _row_gather(y_hbm, d1_ref, base, y1buf, sems.at[0], tm)
    _row_gather(y_hbm, d2_ref, base, y2buf, sems.at[1], tm)
    _row_gather_wait(y_hbm, y1buf, sems.at[0], tm)
    _row_gather_wait(y_hbm, y2buf, sems.at[1], tm)
    route = route_ref[0]
    y = route[:, 4:5] * y1buf[...] + route[:, 5:6] * y2buf[...]
    r = DN_ALPHA * x_ref[0] + (1.0 + g_ref[0]) * y
    o_ref[0] = _layer_norm(r, lg_ref[...], lb_ref[...])


def _combine_ln(dest1, dest2, y, x, route, gate, ln_g, ln_b):
    bsz, s, d = x.shape
    tm = COMB_TM
    row = lambda b, i, d1, d2: (b, i, 0)
    vec = pl.BlockSpec((1, d), lambda b, i, d1, d2: (0, 0))
    grid_spec = pltpu.PrefetchScalarGridSpec(
        num_scalar_prefetch=2,
        grid=(bsz, s // tm),
        in_specs=[pl.BlockSpec(memory_space=pl.ANY),
                  pl.BlockSpec((1, tm, d), row),
                  pl.BlockSpec((1, tm, LANES), row),
                  pl.BlockSpec((1, 1, d), lambda b, i, d1, d2: (b, 0, 0)), vec, vec],
        out_specs=pl.BlockSpec((1, tm, d), row),
        scratch_shapes=[pltpu.VMEM((tm, d), _F32), pltpu.VMEM((tm, d), _F32), pltpu.SemaphoreType.DMA((2,))],
    )
    return pl.pallas_call(
        _combine_kernel,
        grid_spec=grid_spec,
        out_shape=jax.ShapeDtypeStruct((bsz, s, d), _F32),
        compiler_params=_cparams(("arbitrary", "arbitrary")),
        name="l1_combine_ln",
    )(dest1, dest2, y, x, route, gate, ln_g.reshape(1, d), ln_b.reshape(1, d))


def _arrange_inproj_weight(w):
    d = w.shape[0]
    kv0 = 3 * SB_W + NSA_Q_W
    g0 = kv0 + 6 * NSA_KV_GROUPS * HEAD_DIM
    per_group = NSA_REP * 3
    pad = jnp.zeros((d, LANES - per_group), w.dtype)
    cols = [w[:, :SB_W] * QK_SCALE, w[:, SB_W:3 * SB_W], w[:, 3 * SB_W:kv0] * QK_SCALE, w[:, kv0:g0]]
    for g in range(NSA_KV_GROUPS):
        cols += [w[:, g0 + g * per_group:g0 + (g + 1) * per_group], pad]
    return jnp.concatenate(cols, axis=1).astype(_BF16)


def _mods(mod, i, bsz, d):
    m = mod[i].reshape(bsz, 6, 1, d)
    return [m[:, k] for k in range(6)]


def kernel(x, c, ada_w, ada_b, ln_g, ln_b, mix_w_in, cmp_pos, cmp_w1, cmp_w2, mix_w_out, ffn_w1, ffn_w3, ffn_w2,
           conv_w_in, conv_taps, conv_w_out, router_w, router_b, exp_w1, exp_w3, exp_w2):
    bsz, s, d = x.shape
    t = bsz * s
    mod = _ada_mod(c, ada_w, ada_b)

    sh1, sc1, g1, sh2, sc2, g2 = _mods(mod, 0, bsz, d)
    sb, nq, kvc, sw, gates = _inproj(x, sc1, sh1, _arrange_inproj_weight(mix_w_in[0]))
    o_sb = _sb_attention(sb)
    pos = cmp_pos[0].reshape(2, 2, NSA_CMP_STRIDE * HEAD_DIM)
    w2rep = jnp.tile(cmp_w2[0], (1, 1, NSA_REP)).astype(_BF16)
    cmp_kv = _compress(kvc, pos, cmp_w1[0].astype(_BF16), w2rep)
    o_nsa = _nsa_attention(nq, cmp_kv, sw, gates)
    x = _outproj_ln(o_sb, o_nsa, mix_w_out[0].astype(_BF16), x, g1, ln_g[0, 0], ln_b[0, 0])
    x = _ffn_ln(x, sc2, sh2, g2, ffn_w1[0].astype(_BF16), ffn_w3[0].astype(_BF16), ffn_w2[0].astype(_BF16),
                ln_g[0, 1], ln_b[0, 1])

    sh1, sc1, g1, sh2, sc2, g2 = _mods(mod, 1, bsz, d)
    x = _conv_ln(x, sc1, sh1, g1, conv_w_in[0].astype(_BF16), conv_taps[0], conv_w_out[0].astype(_BF16),
                 ln_g[1, 0], ln_b[1, 0])
    rw = jnp.pad(router_w[0], ((0, 0), (0, LANES - N_EXPERTS)))
    rb = jnp.pad(router_b[0], (0, LANES - N_EXPERTS)).reshape(1, LANES)
    h2, route, counts = _router(x, sc2, sh2, rw, rb)

    route_flat = route.reshape(t, LANES)
    e1 = route_flat[:, 0].astype(jnp.int32)
    e2 = route_flat[:, 1].astype(jnp.int32)
    p1 = route_flat[:, 2].astype(jnp.int32)
    p2 = route_flat[:, 3].astype(jnp.int32)
    cnt = counts[0, :N_EXPERTS].astype(jnp.int32)
    rb_rows = MOE_ROW_BLOCK
    padded = (cnt + rb_rows - 1) // rb_rows * rb_rows
    pends = jnp.cumsum(padded)
    pstarts = pends - padded
    dest1 = pstarts[e1] + p1
    dest2 = pstarts[e2] + p2
    n_blocks = -(-(t * 2) // rb_rows) + N_EXPERTS
    tok = jnp.arange(t, dtype=jnp.int32)
    row_tok = jnp.zeros((n_blocks * rb_rows,), jnp.int32).at[dest1].set(tok).at[dest2].set(tok)
    blk_e = jnp.minimum(jnp.searchsorted(pends, jnp.arange(n_blocks, dtype=jnp.int32) * rb_rows, side='right'),
                        N_EXPERTS - 1).astype(jnp.int32)
    n_used = (pends[-1:] // rb_rows).astype(jnp.int32)

    y = _experts(blk_e, n_used, row_tok, h2.reshape(t, d), exp_w1[0].astype(_BF16), exp_w3[0].astype(_BF16),
                 exp_w2[0].astype(_BF16))
    return _combine_ln(dest1, dest2, y, x, route, g2, ln_g[1, 1], ln_b[1, 1])
```

```python
import functools

import numpy as np
import jax
import jax.numpy as jnp
from jax import lax
from jax.experimental import pallas as pl
from jax.experimental.pallas import tpu as pltpu

HEAD_DIM = 64
SB_HEADS = 8
NSA_HEADS = 8
NSA_KV_GROUPS = 2
NSA_REP = NSA_HEADS // NSA_KV_GROUPS
NSA_CMP_LEN = 32
NSA_CMP_STRIDE = 16
NSA_SLC_BLOCK = 64
NSA_SLC_TOPK = 16
NSA_WINDOW = 512
SB_W = SB_HEADS * HEAD_DIM
NSA_Q_W = NSA_HEADS * HEAD_DIM
N_EXPERTS = 8
MOE_ROW_BLOCK = 512
DEPTH = 2
LN_EPS = 1e-5
DN_ALPHA = (2 * DEPTH) ** 0.25
NEG = -1e30
SEL_FORCE = 1e30
QK_SCALE = HEAD_DIM ** -0.5

LANES = 128
VMEM_LIMIT = 56 * 1024 * 1024

C_SB = 0
C_NQ = 3 * SB_W
C_CMP = C_NQ + NSA_Q_W
C_SW = C_CMP + 4 * HEAD_DIM
C_GATE = C_SW + 4 * 2 * HEAD_DIM
C_END = C_GATE + 2 * LANES

_F32 = jnp.float32
_BF16 = jnp.bfloat16


def _cparams(sem, vmem=VMEM_LIMIT):
    return pltpu.CompilerParams(dimension_semantics=sem, vmem_limit_bytes=vmem)


def _dot(a, b):
    return jnp.dot(a, b, preferred_element_type=_F32)


def _dot_t(a, b):
    return lax.dot_general(a, b, (((1,), (1,)), ((), ())), preferred_element_type=_F32)


def _split_bf16(x):
    hi = x.astype(_BF16)
    lo = (x - hi.astype(_F32)).astype(_BF16)
    return hi, lo


def _silu(x):
    return x / (1.0 + jnp.exp(-x))


def _layer_norm(r, g, b):
    mu = jnp.mean(r, axis=-1, keepdims=True)
    d = r - mu
    var = jnp.mean(d * d, axis=-1, keepdims=True)
    return d * lax.rsqrt(var + LN_EPS) * g + b


def _ada_kernel(c_ref, w_ref, b_ref, o_ref):
    cond = _silu(c_ref[...])
    o_ref[0] = jnp.dot(cond, w_ref[0], preferred_element_type=_F32,
                       precision=lax.Precision.HIGHEST) + b_ref[0]


def _ada_mod(c, ada_w, ada_b):
    depth, d, n = ada_w.shape
    bsz = c.shape[0]
    tn = n // 6
    return pl.pallas_call(
        _ada_kernel,
        grid=(depth, n // tn),
        in_specs=[pl.BlockSpec((bsz, d), lambda i, j: (0, 0)),
                  pl.BlockSpec((1, d, tn), lambda i, j: (i, 0, j)),
                  pl.BlockSpec((1, 1, tn), lambda i, j: (i, 0, j))],
        out_specs=pl.BlockSpec((1, bsz, tn), lambda i, j: (i, 0, j)),
        out_shape=jax.ShapeDtypeStruct((depth, bsz, n), _F32),
        compiler_params=_cparams(("parallel", "parallel")),
        name="ada_mod",
    )(c, ada_w, ada_b.reshape(depth, 1, n))


def _inproj_kernel(x_ref, sc_ref, sh_ref, w_ref, sb_ref, nq_ref, cmp_ref, sw_ref, gate_ref):
    h = (x_ref[0] * (1.0 + sc_ref[0]) + sh_ref[0]).astype(_BF16)
    sb_ref[0] = _dot(h, w_ref[:, C_SB:C_NQ]).astype(_BF16)
    nq_ref[0] = _dot(h, w_ref[:, C_NQ:C_CMP]).astype(_BF16)
    kvc = _dot(h, w_ref[:, C_CMP:C_SW])
    for j in range(4):
        cmp_ref[0, j] = kvc[:, j * HEAD_DIM:(j + 1) * HEAD_DIM]
    lane = lax.broadcasted_iota(jnp.int32, (1, LANES), 1)
    first = lane < HEAD_DIM
    for kind in range(4):
        a = _dot(h, w_ref[:, C_SW + kind * LANES:C_SW + (kind + 1) * LANES])
        r = pltpu.roll(a, HEAD_DIM, 1)
        g0 = jnp.where(first, a, r).astype(_BF16)
        g1 = jnp.where(first, r, a).astype(_BF16)
        sw_ref[0, :, (2 * kind) * 256:(2 * kind + 1) * 256] = jnp.concatenate([g0, g0], axis=1)
        sw_ref[0, :, (2 * kind + 1) * 256:(2 * kind + 2) * 256] = jnp.concatenate([g1, g1], axis=1)
    gate_ref[0] = _dot(h, w_ref[:, C_GATE:C_END])


def _inproj(x, sc, sh, w):
    bsz, s, d = x.shape
    tm = 512
    row = lambda b, i: (b, i, 0)
    return pl.pallas_call(
        _inproj_kernel,
        grid=(bsz, s // tm),
        in_specs=[pl.BlockSpec((1, tm, d), row),
                  pl.BlockSpec((1, 1, d), lambda b, i: (b, 0, 0)),
                  pl.BlockSpec((1, 1, d), lambda b, i: (b, 0, 0)),
                  pl.BlockSpec((d, C_END), lambda b, i: (0, 0))],
        out_specs=[pl.BlockSpec((1, tm, 3 * SB_W), row),
                   pl.BlockSpec((1, tm, NSA_Q_W), row),
                   pl.BlockSpec((1, 4, tm, HEAD_DIM), lambda b, i: (b, 0, i, 0)),
                   pl.BlockSpec((1, tm, 2048), row),
                   pl.BlockSpec((1, tm, 2 * LANES), row)],
        out_shape=[jax.ShapeDtypeStruct((bsz, s, 3 * SB_W), _BF16),
                   jax.ShapeDtypeStruct((bsz, s, NSA_Q_W), _BF16),
                   jax.ShapeDtypeStruct((bsz, 4, s, HEAD_DIM), _F32),
                   jax.ShapeDtypeStruct((bsz, s, 2048), _BF16),
                   jax.ShapeDtypeStruct((bsz, s, 2 * LANES), _F32)],
        compiler_params=_cparams(("parallel", "parallel")),
        name="l0_inproj",
    )(x, sc, sh, w)


SB_TQ = 512
SB_TK = 128
SB_UNROLL = 2


def _sb_kernel(q_ref, k_ref, v_ref, tri_ref, o_ref, acc_ref, run_ref):
    tq, tk = SB_TQ, SB_TK
    qi = pl.program_id(2)
    q = q_ref[0]
    lane = lax.broadcasted_iota(jnp.int32, (1, LANES), 1)
    first = lane < HEAD_DIM
    zero = jnp.zeros_like(q)
    qh = (jnp.where(first, q, zero), jnp.where(first, zero, q))
    tri = tri_ref[...]
    acc_ref[...] = jnp.zeros_like(acc_ref)
    run_ref[...] = jnp.zeros_like(run_ref)

    def tile(key_off, r0, key_col0):
        rows = slice(r0, tq)
        kb = k_ref[0, pl.ds(key_off, tk), :]
        vb = v_ref[0, pl.ds(key_off, tk), :]
        causal = None
        if key_col0 is not None:
            row = r0 + lax.broadcasted_iota(jnp.int32, (tq - r0, tk), 0)
            col = key_col0 + lax.broadcasted_iota(jnp.int32, (tq - r0, tk), 1)
            causal = col < row
        for h in range(2):
            z = _dot_t(qh[h][rows], kb)
            nlf = jnp.maximum(z, 0.0) + jnp.log(1.0 + jnp.exp(-jnp.abs(z)))
            if causal is not None:
                nlf = jnp.where(causal, nlf, 0.0)
            hi_f = lax.bitcast_convert_type(
                lax.bitcast_convert_type(nlf, jnp.uint32) & jnp.uint32(0xFFFF0000), _F32)
            hi = hi_f.astype(_BF16)
            lo = (nlf - hi_f).astype(_BF16)
            ext = _dot(jnp.concatenate([hi, lo], axis=1), tri)
            run = run_ref[h, rows, :]
            w = jnp.exp(z + ext[:, :tk] + run)
            if causal is not None:
                w = jnp.where(causal, w, 0.0)
            acc_ref[h, rows, :] += _dot(w.astype(_BF16), vb)
            run_ref[h, rows, :] = run + ext[:, tk:]

    n_diag = tq // tk
    for kd in reversed(range(n_diag)):
        tile(pl.multiple_of(qi * tq + kd * tk, tk), kd * tk, kd * tk)

    def below(jj, carry):
        for u in range(SB_UNROLL):
            tile(pl.multiple_of((qi * n_diag - 1 - SB_UNROLL * jj - u) * tk, tk), 0, None)
        return carry
    lax.fori_loop(0, qi * (n_diag // SB_UNROLL), below, 0)
    o_ref[0] = jnp.where(first, acc_ref[0], acc_ref[1]).astype(o_ref.dtype)


def _sb_attention(sb):
    bsz, s, _ = sb.shape
    tq, tk = SB_TQ, SB_TK
    npair = SB_W // LANES
    suffix = (np.arange(tk)[:, None] >= np.arange(tk)[None, :]).astype(np.float32)
    tri = -np.tile(np.concatenate([suffix, np.ones((tk, tk), np.float32)], axis=1), (2, 1))
    return pl.pallas_call(
        _sb_kernel,
        grid=(bsz, npair, s // tq),
        in_specs=[pl.BlockSpec((1, tq, LANES), lambda b, p, i: (b, i, p)),
                  pl.BlockSpec((1, s, LANES), lambda b, p, i: (b, 0, npair + p)),
                  pl.BlockSpec((1, s, LANES), lambda b, p, i: (b, 0, 2 * npair + p)),
                  pl.BlockSpec((2 * tk, 2 * tk), lambda b, p, i: (0, 0))],
        out_specs=pl.BlockSpec((1, tq, LANES), lambda b, p, i: (b, i, p)),
        out_shape=jax.ShapeDtypeStruct((bsz, s, SB_W), _BF16),
        scratch_shapes=[pltpu.VMEM((2, tq, LANES), _F32), pltpu.VMEM((2, tq, LANES), _F32)],
        compiler_params=_cparams(("parallel", "parallel", "arbitrary")),
        name="sb_attention",
    )(sb, sb, sb, jnp.asarray(tri, _BF16))


def _compress_kernel(x_ref, pos_ref, w1_ref, w2_ref, o_ref):
    half = w1_ref.shape[1] // 2
    ch = x_ref[0, 0]
    top = _dot((ch + pos_ref[0, 0:1, :]).astype(_BF16), w1_ref[0, :half, :])
    bot = _dot((ch + pos_ref[0, 1:2, :]).astype(_BF16), w1_ref[0, half:, :])
    n = ch.shape[0]
    pre = top + pltpu.roll(bot, n - 1, 0)
    o_ref[0, 0] = _dot(_silu(pre).astype(_BF16), w2_ref[0]).astype(o_ref.dtype)


def _compress(kvc, pos, w1, w2rep):
    bsz, _, s, dh = kvc.shape
    n_chunk = s // NSA_CMP_STRIDE
    cw = NSA_CMP_STRIDE * dh
    x = kvc.reshape(bsz, 4, n_chunk, cw)
    hid = w1.shape[-1]
    return pl.pallas_call(
        _compress_kernel,
        grid=(bsz, 4),
        in_specs=[pl.BlockSpec((1, 1, n_chunk, cw), lambda b, j: (b, j, 0, 0)),
                  pl.BlockSpec((1, 2, cw), lambda b, j: (j // 2, 0, 0)),
                  pl.BlockSpec((1, 2 * cw, hid), lambda b, j: (j // 2, 0, 0)),
                  pl.BlockSpec((1, hid, 4 * dh), lambda b, j: (j // 2, 0, 0))],
        out_specs=pl.BlockSpec((1, 1, n_chunk, 4 * dh), lambda b, j: (b, j, 0, 0)),
        out_shape=jax.ShapeDtypeStruct((bsz, 4, n_chunk, 4 * dh), _BF16),
        compiler_params=_cparams(("parallel", "parallel")),
        name="nsa_compress",
    )(x, pos, w1, w2rep)


NSA_TQ = 128
NSA_TK = 512
WIN_SPAN = NSA_WINDOW + NSA_TQ


def _nsa_kernel(q_ref, kc_ref, vc_ref, ks_ref, vs_ref, kw_ref, vw_ref, gate_ref, ov_ref, ex_ref, o_ref):
    tq, tk = NSA_TQ, NSA_TK
    qi = pl.program_id(2)
    t0 = qi * tq
    q = q_ref[0]
    head_of_lane = lax.broadcasted_iota(jnp.int32, (1, 4 * HEAD_DIM), 1) // HEAD_DIM
    zero = jnp.zeros_like(q)
    qs = jnp.concatenate([jnp.where(head_of_lane == r, q, zero) for r in range(NSA_REP)], axis=0)
    t_row = t0 + lax.broadcasted_iota(jnp.int32, (tq, 1), 0)
    t_row4 = jnp.concatenate([t_row] * NSA_REP, axis=0)

    def unstack(o):
        out = jnp.zeros((tq, 4 * HEAD_DIM), _F32)
        for r in range(NSA_REP):
            out = jnp.where(head_of_lane == r, o[r * tq:(r + 1) * tq], out)
        return out

    n_cmp = kc_ref.shape[2]
    s_c = _dot_t(qs, kc_ref[0, 0])
    cmp_last = lax.broadcasted_iota(jnp.int32, (1, n_cmp), 1) * NSA_CMP_STRIDE + (NSA_CMP_LEN - 1)
    mask_c = cmp_last <= t_row4
    sm = jnp.where(mask_c, s_c, NEG)
    e_c = jnp.where(mask_c, jnp.exp(sm - jnp.max(sm, axis=1, keepdims=True)), 0.0)
    l_c = jnp.sum(e_c, axis=1, keepdims=True)
    p_c = e_c / jnp.where(l_c == 0.0, 1.0, l_c)
    o_cmp = unstack(_dot(p_c.astype(_BF16), vc_ref[0, 0]))

    p_sum = p_c[0:tq] + p_c[tq:2 * tq] + p_c[2 * tq:3 * tq] + p_c[3 * tq:4 * tq]
    hi, lo = _split_bf16(p_sum)
    n_slc = ex_ref.shape[1] // NSA_SLC_BLOCK
    imp = _dot_t(ov_ref[...], jnp.concatenate([hi, lo], axis=1))[:n_slc]
    blk = lax.broadcasted_iota(jnp.int32, (n_slc, 1), 0)
    cur = (t0 + lax.broadcasted_iota(jnp.int32, (1, tq), 1)) // NSA_SLC_BLOCK
    valid = blk <= cur
    forced = (blk == 0) | (blk == cur) | (blk == cur - 1)
    val = jnp.where(valid, jnp.where(forced, SEL_FORCE, imp), NEG)
    rank = jnp.zeros((n_slc, tq), _F32)
    for i in range(n_slc):
        vi = val[i:i + 1, :]
        ahead = (vi > val) | ((vi == val) & (blk > i))
        rank = rank + jnp.where(ahead, 1.0, 0.0)
    sel_t = jnp.where(rank < float(min(NSA_SLC_TOPK, n_slc)), 1.0, 0.0)
    sel_t = jnp.concatenate([sel_t, jnp.zeros((LANES - n_slc, tq), _F32)], axis=0)
    sel = sel_t.T.astype(_BF16)

    def slc_block(j, carry, diagonal):
        m, l, acc = carry
        off = pl.multiple_of(j * tk, tk)
        kb = ks_ref[0, pl.ds(off, tk), :]
        vb = vs_ref[0, pl.ds(off, tk), :]
        s = _dot_t(qs, kb)
        chosen = _dot(sel, ex_ref[:, pl.ds(off, tk)])
        bias = (chosen - 1.0) * 1e30
        if diagonal:
            kpos = j * tk + lax.broadcasted_iota(jnp.int32, (1, tk), 1)
            bias = jnp.where(kpos <= t_row, bias, NEG)
        s = s + jnp.concatenate([bias] * NSA_REP, axis=0)
        m_new = jnp.maximum(m, jnp.max(s, axis=1, keepdims=True))
        alpha = jnp.exp(m - m_new)
        p = jnp.exp(s - m_new)
        l = alpha * l + jnp.sum(p, axis=1, keepdims=True)
        acc = alpha * acc + _dot(p.astype(_BF16), vb)
        return m_new, l, acc

    jd = t0 // tk
    init = (jnp.full((4 * tq, 1), NEG, _F32), jnp.zeros((4 * tq, 1), _F32),
            jnp.zeros((4 * tq, 4 * HEAD_DIM), _F32))
    carry = slc_block(jd, init, True)
    m_s, l_s, acc_s = lax.fori_loop(0, jd, lambda j, c: slc_block(j, c, False), carry)
    o_slc = unstack(acc_s / l_s)

    start = pl.multiple_of(jnp.maximum(t0 - NSA_WINDOW, 0), tq)
    s_w = _dot_t(qs, kw_ref[0, pl.ds(start, WIN_SPAN), :])
    diff = t_row - (start + lax.broadcasted_iota(jnp.int32, (1, WIN_SPAN), 1))
    bias_w = jnp.where((diff >= 0) & (diff < NSA_WINDOW), 0.0, NEG)
    sm = s_w + jnp.concatenate([bias_w] * NSA_REP, axis=0)
    e_w = jnp.exp(sm - jnp.max(sm, axis=1, keepdims=True))
    p_w = e_w / jnp.sum(e_w, axis=1, keepdims=True)
    o_win = unstack(_dot(p_w.astype(_BF16), vw_ref[0, pl.ds(start, WIN_SPAN), :]))

    g = 1.0 / (1.0 + jnp.exp(-gate_ref[0]))
    out = jnp.zeros((tq, 4 * HEAD_DIM), _F32)
    for branch, o_b in enumerate((o_cmp, o_slc, o_win)):
        gm = jnp.zeros((tq, 4 * HEAD_DIM), _F32)
        for r in range(NSA_REP):
            c = 3 * r + branch
            gm = jnp.where(head_of_lane == r, g[:, c:c + 1], gm)
        out = out + gm * o_b
    o_ref[0] = out.astype(o_ref.dtype)


def _nsa_attention(nq, cmp_kv, sw, gates):
    bsz, s, _ = nq.shape
    tq = NSA_TQ
    n_cmp = cmp_kv.shape[2]
    n_slc = s // NSA_SLC_BLOCK
    gw = 4 * HEAD_DIM
    cmp_start = np.arange(n_cmp) * NSA_CMP_STRIDE
    slc_start = np.arange(LANES) * NSA_SLC_BLOCK
    ov = ((cmp_start[:, None] < slc_start[None, :] + NSA_SLC_BLOCK)
          & (cmp_start[:, None] + NSA_CMP_LEN > slc_start[None, :])
          & (np.arange(LANES)[None, :] < n_slc)).astype(np.float32)
    ov[n_cmp - 1:] = 0.0
    ov2 = np.concatenate([ov.T, ov.T], axis=1)
    ex = (np.arange(LANES)[:, None] == (np.arange(s)[None, :] // NSA_SLC_BLOCK)).astype(np.float32)
    kv_spec = lambda kind: pl.BlockSpec((1, s, gw), lambda b, g, i, kind=kind: (b, 0, 2 * kind + g))
    return pl.pallas_call(
        _nsa_kernel,
        grid=(bsz, NSA_KV_GROUPS, s // tq),
        in_specs=[pl.BlockSpec((1, tq, gw), lambda b, g, i: (b, i, g)),
                  pl.BlockSpec((1, 1, n_cmp, gw), lambda b, g, i: (b, g, 0, 0)),
                  pl.BlockSpec((1, 1, n_cmp, gw), lambda b, g, i: (b, 2 + g, 0, 0)),
                  kv_spec(0), kv_spec(1), kv_spec(2), kv_spec(3),
                  pl.BlockSpec((1, tq, LANES), lambda b, g, i: (b, i, g)),
                  pl.BlockSpec((LANES, 2 * n_cmp), lambda b, g, i: (0, 0)),
                  pl.BlockSpec((LANES, s), lambda b, g, i: (0, 0))],
        out_specs=pl.BlockSpec((1, tq, gw), lambda b, g, i: (b, i, g)),
        out_shape=jax.ShapeDtypeStruct((bsz, s, NSA_Q_W), _BF16),
        compiler_params=_cparams(("parallel", "parallel", "arbitrary")),
        name="nsa_attention",
    )(nq, cmp_kv, cmp_kv, sw, sw, sw, sw, gates, jnp.asarray(ov2, _BF16), jnp.asarray(ex, _BF16))


def _outproj_kernel(a_ref, b_ref, wa_ref, wb_ref, x_ref, g_ref, lg_ref, lb_ref, o_ref):
    y = _dot(a_ref[0], wa_ref[...]) + _dot(b_ref[0], wb_ref[...])
    r = DN_ALPHA * x_ref[0] + (1.0 + g_ref[0]) * y
    o_ref[0] = _layer_norm(r, lg_ref[...], lb_ref[...])


def _outproj_ln(o_sb, o_nsa, w_out, x, gate, ln_g, ln_b):
    bsz, s, d = x.shape
    tm = 512
    row = lambda b, i: (b, i, 0)
    vec = pl.BlockSpec((1, d), lambda b, i: (0, 0))
    return pl.pallas_call(
        _outproj_kernel,
        grid=(bsz, s // tm),
        in_specs=[pl.BlockSpec((1, tm, SB_W), row), pl.BlockSpec((1, tm, NSA_Q_W), row),
                  pl.BlockSpec((SB_W, d), lambda b, i: (0, 0)),
                  pl.BlockSpec((NSA_Q_W, d), lambda b, i: (1, 0)),
                  pl.BlockSpec((1, tm, d), row),
                  pl.BlockSpec((1, 1, d), lambda b, i: (b, 0, 0)), vec, vec],
        out_specs=pl.BlockSpec((1, tm, d), row),
        out_shape=jax.ShapeDtypeStruct((bsz, s, d), _F32),
        compiler_params=_cparams(("parallel", "parallel")),
        name="l0_outproj_ln",
    )(o_sb, o_nsa, w_out, w_out, x, gate, ln_g.reshape(1, d), ln_b.reshape(1, d))


def _ffn_kernel(x_ref, sc_ref, sh_ref, g_ref, w1_ref, w3_ref, w2_ref, lg_ref, lb_ref, o_ref, h_ref, acc_ref):
    c = pl.program_id(2)

    @pl.when(c == 0)
    def _():
        h_ref[...] = (x_ref[0] * (1.0 + sc_ref[0]) + sh_ref[0]).astype(_BF16)
        acc_ref[...] = jnp.zeros_like(acc_ref)

    h = h_ref[...]
    a = _silu(_dot(h, w1_ref[...])) * _dot(h, w3_ref[...])
    acc_ref[...] += _dot(a.astype(_BF16), w2_ref[...])

    @pl.when(c == pl.num_programs(2) - 1)
    def _():
        r = DN_ALPHA * x_ref[0] + (1.0 + g_ref[0]) * acc_ref[...]
        o_ref[0] = _layer_norm(r, lg_ref[...], lb_ref[...])


def _ffn_ln(x, sc, sh, gate, w1, w3, w2, ln_g, ln_b):
    bsz, s, d = x.shape
    ff = w1.shape[1]
    tm, tf = 1024, 256
    row = lambda b, i, c: (b, i, 0)
    mod = pl.BlockSpec((1, 1, d), lambda b, i, c: (b, 0, 0))
    vec = pl.BlockSpec((1, d), lambda b, i, c: (0, 0))
    return pl.pallas_call(
        _ffn_kernel,
        grid=(bsz, s // tm, ff // tf),
        in_specs=[pl.BlockSpec((1, tm, d), row), mod, mod, mod,
                  pl.BlockSpec((d, tf), lambda b, i, c: (0, c)),
                  pl.BlockSpec((d, tf), lambda b, i, c: (0, c)),
                  pl.BlockSpec((tf, d), lambda b, i, c: (c, 0)), vec, vec],
        out_specs=pl.BlockSpec((1, tm, d), row),
        out_shape=jax.ShapeDtypeStruct((bsz, s, d), _F32),
        scratch_shapes=[pltpu.VMEM((tm, d), _BF16), pltpu.VMEM((tm, d), _F32)],
        compiler_params=_cparams(("parallel", "parallel", "arbitrary")),
        name="l0_ffn_ln",
    )(x, sc, sh, gate, w1, w3, w2, ln_g.reshape(1, d), ln_b.reshape(1, d))


def _conv_kernel(x_ref, sc_ref, sh_ref, g_ref, win_ref, taps_ref, wout_ref, lg_ref, lb_ref, o_ref, tail_ref):
    ts, d = x_ref.shape[1], x_ref.shape[2]

    @pl.when(pl.program_id(1) == 0)
    def _():
        tail_ref[...] = jnp.zeros_like(tail_ref)

    x = x_ref[0]
    h = (x * (1.0 + sc_ref[0]) + sh_ref[0]).astype(_BF16)
    z = _dot(h, win_ref[:, d:2 * d]) * _dot(h, win_ref[:, 2 * d:3 * d])
    row = lax.broadcasted_iota(jnp.int32, (ts, 1), 0)
    prev1 = tail_ref[7:8, :]
    prev2 = tail_ref[6:7, :]
    z1 = jnp.where(row == 0, prev1, pltpu.roll(z, 1, 0))
    z2 = jnp.where(row == 0, prev2, jnp.where(row == 1, prev1, pltpu.roll(z, 2, 0)))
    zc = taps_ref[0:1, :] * z2 + taps_ref[1:2, :] * z1 + taps_ref[2:3, :] * z
    tail_ref[...] = z[ts - 8:ts, :]
    y = _dot((_dot(h, win_ref[:, 0:d]) * zc).astype(_BF16), wout_ref[...])
    r = DN_ALPHA * x + (1.0 + g_ref[0]) * y
    o_ref[0] = _layer_norm(r, lg_ref[...], lb_ref[...])


def _conv_ln(x, sc, sh, gate, w_in, taps, w_out, ln_g, ln_b):
    bsz, s, d = x.shape
    ts = 512
    row = lambda b, i: (b, i, 0)
    mod = pl.BlockSpec((1, 1, d), lambda b, i: (b, 0, 0))
    vec = pl.BlockSpec((1, d), lambda b, i: (0, 0))
    return pl.pallas_call(
        _conv_kernel,
        grid=(bsz, s // ts),
        in_specs=[pl.BlockSpec((1, ts, d), row), mod, mod, mod,
                  pl.BlockSpec((d, 3 * d), lambda b, i: (0, 0)),
                  pl.BlockSpec((taps.shape[0], d), lambda b, i: (0, 0)),
                  pl.BlockSpec((d, d), lambda b, i: (0, 0)), vec, vec],
        out_specs=pl.BlockSpec((1, ts, d), row),
        out_shape=jax.ShapeDtypeStruct((bsz, s, d), _F32),
        scratch_shapes=[pltpu.VMEM((8, d), _F32)],
        compiler_params=_cparams(("arbitrary", "arbitrary")),
        name="l1_conv_ln",
    )(x, sc, sh, gate, w_in, taps, w_out, ln_g.reshape(1, d), ln_b.reshape(1, d))


ROUTE_TM = 512


def _router_kernel(x_ref, sc_ref, sh_ref, rw_ref, rb_ref, tri_ref, h_ref, route_ref, cnt_ref, run_ref):
    tm = ROUTE_TM

    @pl.when((pl.program_id(0) == 0) & (pl.program_id(1) == 0))
    def _():
        run_ref[...] = jnp.zeros_like(run_ref)

    h = x_ref[0] * (1.0 + sc_ref[0]) + sh_ref[0]
    h_ref[0] = h
    lane = lax.broadcasted_iota(jnp.int32, (1, LANES), 1)
    logits = jnp.dot(h, rw_ref[...], preferred_element_type=_F32,
                     precision=lax.Precision.HIGHEST) + rb_ref[...]
    logits = jnp.where(lane < N_EXPERTS, logits, -3e38)
    m1 = jnp.max(logits, axis=1, keepdims=True)
    e1 = jnp.min(jnp.where(logits == m1, lane, LANES), axis=1, keepdims=True)
    rest = jnp.where(lane == e1, -3e38, logits)
    m2 = jnp.max(rest, axis=1, keepdims=True)
    e2 = jnp.min(jnp.where(rest == m2, lane, LANES), axis=1, keepdims=True)
    ex = jnp.exp(m2 - m1)
    g1 = 1.0 / (1.0 + ex)
    g2 = ex / (1.0 + ex)
    oh1 = jnp.where(lane == e1, 1.0, 0.0)
    oh2 = jnp.where(lane == e2, 1.0, 0.0)
    oh = oh1 + oh2
    before = run_ref[...] + _dot(tri_ref[...], oh.astype(_BF16))
    p1 = jnp.sum(oh1 * before, axis=1, keepdims=True)
    p2 = jnp.sum(oh2 * before, axis=1, keepdims=True)
    run_ref[...] += jnp.sum(oh, axis=0, keepdims=True)
    cnt_ref[...] = run_ref[...]
    vals = (e1.astype(_F32), e2.astype(_F32), p1, p2, g1, g2)
    out = jnp.zeros((tm, LANES), _F32)
    for k, v in enumerate(vals):
        out = jnp.where(lane == k, v, out)
    route_ref[0] = out


def _router(x, sc, sh, rw, rb):
    bsz, s, d = x.shape
    tm = ROUTE_TM
    row = lambda b, i: (b, i, 0)
    mod = pl.BlockSpec((1, 1, d), lambda b, i: (b, 0, 0))
    tri = (np.arange(tm)[None, :] < np.arange(tm)[:, None]).astype(np.float32)
    return pl.pallas_call(
        _router_kernel,
        grid=(bsz, s // tm),
        in_specs=[pl.BlockSpec((1, tm, d), row), mod, mod,
                  pl.BlockSpec((d, LANES), lambda b, i: (0, 0)),
                  pl.BlockSpec((1, LANES), lambda b, i: (0, 0)),
                  pl.BlockSpec((tm, tm), lambda b, i: (0, 0))],
        out_specs=[pl.BlockSpec((1, tm, d), row),
                   pl.BlockSpec((1, tm, LANES), row),
                   pl.BlockSpec((1, LANES), lambda b, i: (0, 0))],
        out_shape=[jax.ShapeDtypeStruct((bsz, s, d), _F32),
                   jax.ShapeDtypeStruct((bsz, s, LANES), _F32),
                   jax.ShapeDtypeStruct((1, LANES), _F32)],
        scratch_shapes=[pltpu.VMEM((1, LANES), _F32)],
        compiler_params=_cparams(("arbitrary", "arbitrary")),
        name="l1_router",
    )(x, sc, sh, rw, rb, jnp.asarray(tri, _BF16))


EXP_TF = 256
GATHER_UNROLL = 8


def _row_gather(src_hbm, idx_ref, base, dst_ref, sem, n):
    def issue(r, carry):
        tok = idx_ref[base + r]
        pltpu.make_async_copy(src_hbm.at[pl.ds(tok, 1)], dst_ref.at[pl.ds(r, 1)], sem).start()
        return carry
    lax.fori_loop(0, n, issue, 0, unroll=GATHER_UNROLL)


def _row_gather_wait(src_hbm, dst_ref, sem, n):
    pltpu.make_async_copy(src_hbm.at[pl.ds(0, n)], dst_ref, sem).wait()


def _expert_kernel(blk_e_ref, n_used_ref, row_tok_ref, h_hbm, w1_ref, w3_ref, w2_ref, y_ref, xbuf, sems):
    i = pl.program_id(0)
    n_used = n_used_ref[0]
    rb = MOE_ROW_BLOCK
    ff = w1_ref.shape[2]
    slot = lax.rem(i, 2)

    @pl.when(i == 0)
    def _():
        _row_gather(h_hbm, row_tok_ref, 0, xbuf.at[0], sems.at[0], rb)

    @pl.when(i + 1 < n_used)
    def _():
        _row_gather(h_hbm, row_tok_ref, (i + 1) * rb, xbuf.at[1 - slot], sems.at[1 - slot], rb)

    @pl.when(i < n_used)
    def _():
        _row_gather_wait(h_hbm, xbuf.at[slot], sems.at[slot], rb)
        xb = xbuf[slot].astype(_BF16)
        acc = jnp.zeros(y_ref.shape, _F32)
        for c in range(ff // EXP_TF):
            cs = slice(c * EXP_TF, (c + 1) * EXP_TF)
            a = _silu(_dot(xb, w1_ref[0, :, cs])) * _dot(xb, w3_ref[0, :, cs])
            acc = acc + _dot(a.astype(_BF16), w2_ref[0, cs, :])
        y_ref[...] = acc

    @pl.when(i >= n_used)
    def _():
        y_ref[...] = jnp.zeros_like(y_ref)


def _experts(blk_e, n_used, row_tok, h_flat, w1, w3, w2):
    t, d = h_flat.shape
    n_blocks = blk_e.shape[0]
    rb = MOE_ROW_BLOCK
    ff = w1.shape[2]
    grid_spec = pltpu.PrefetchScalarGridSpec(
        num_scalar_prefetch=3,
        grid=(n_blocks,),
        in_specs=[pl.BlockSpec(memory_space=pl.ANY),
                  pl.BlockSpec((1, d, ff), lambda i, be, nu, rt: (be[i], 0, 0)),
                  pl.BlockSpec((1, d, ff), lambda i, be, nu, rt: (be[i], 0, 0)),
                  pl.BlockSpec((1, ff, d), lambda i, be, nu, rt: (be[i], 0, 0))],
        out_specs=pl.BlockSpec((rb, d), lambda i, be, nu, rt: (i, 0)),
        scratch_shapes=[pltpu.VMEM((2, rb, d), _F32), pltpu.SemaphoreType.DMA((2,))],
    )
    return pl.pallas_call(
        _expert_kernel,
        grid_spec=grid_spec,
        out_shape=jax.ShapeDtypeStruct((n_blocks * rb, d), _F32),
        compiler_params=_cparams(("arbitrary",)),
        name="l1_experts",
    )(blk_e, n_used, row_tok, h_flat, w1, w3, w2)


COMB_TM = 256


def _combine_kernel(d1_ref, d2_ref, y_hbm, x_ref, route_ref, g_ref, lg_ref, lb_ref, o_ref, y1buf, y2buf, sems):
    tm = COMB_TM
    i = pl.program_id(0) * pl.num_programs(1) + pl.program_id(1)
    n = pl.num_programs(0) * pl.num_programs(1)
    slot = lax.rem(i, 2)

    def gather(tile, s):
        _row_gather(y_hbm, d1_ref, tile * tm, y1buf.at[s], sems.at[0, s], tm)
        _row_gather(y_hbm, d2_ref, tile * tm, y2buf.at[s], sems.at[1, s], tm)

    @pl.when(i == 0)
    def _():
        gather(0, 0)

    @pl.when(i + 1 < n)
    def _():
        gather(i + 1, 1 - slot)

    _row_gather_wait(y_hbm, y1buf.at[slot], sems.at[0, slot], tm)
    _row_gather_wait(y_hbm, y2buf.at[slot], sems.at[1, slot], tm)
    route = route_ref[0]
    y = route[:, 4:5] * y1buf[slot] + route[:, 5:6] * y2buf[slot]
    r = DN_ALPHA * x_ref[0] + (1.0 + g_ref[0]) * y
    o_ref[0] = _layer_norm(r, lg_ref[...], lb_ref[...])


def _combine_ln(dest1, dest2, y, x, route, gate, ln_g, ln_b):
    bsz, s, d = x.shape
    tm = COMB_TM
    row = lambda b, i, d1, d2: (b, i, 0)
    vec = pl.BlockSpec((1, d), lambda b, i, d1, d2: (0, 0))
    grid_spec = pltpu.PrefetchScalarGridSpec(
        num_scalar_prefetch=2,
        grid=(bsz, s // tm),
        in_specs=[pl.BlockSpec(memory_space=pl.ANY),
                  pl.BlockSpec((1, tm, d), row),
                  pl.BlockSpec((1, tm, LANES), row),
                  pl.BlockSpec((1, 1, d), lambda b, i, d1, d2: (b, 0, 0)), vec, vec],
        out_specs=pl.BlockSpec((1, tm, d), row),
        scratch_shapes=[pltpu.VMEM((2, tm, d), _F32), pltpu.VMEM((2, tm, d), _F32),
                        pltpu.SemaphoreType.DMA((2, 2))],
    )
    return pl.pallas_call(
        _combine_kernel,
        grid_spec=grid_spec,
        out_shape=jax.ShapeDtypeStruct((bsz, s, d), _F32),
        compiler_params=_cparams(("arbitrary", "arbitrary")),
        name="l1_combine_ln",
    )(dest1, dest2, y, x, route, gate, ln_g.reshape(1, d), ln_b.reshape(1, d))


def _arrange_inproj_weight(w):
    d = w.shape[0]
    kv0 = 3 * SB_W + NSA_Q_W
    g0 = kv0 + 6 * NSA_KV_GROUPS * HEAD_DIM
    per_group = NSA_REP * 3
    pad = jnp.zeros((d, LANES - per_group), w.dtype)
    cols = [w[:, :SB_W] * QK_SCALE, w[:, SB_W:3 * SB_W], w[:, 3 * SB_W:kv0] * QK_SCALE, w[:, kv0:g0]]
    for g in range(NSA_KV_GROUPS):
        cols += [w[:, g0 + g * per_group:g0 + (g + 1) * per_group], pad]
    return jnp.concatenate(cols, axis=1).astype(_BF16)


def _mods(mod, i, bsz, d):
    m = mod[i].reshape(bsz, 6, 1, d)
    return [m[:, k] for k in range(6)]


def kernel(x, c, ada_w, ada_b, ln_g, ln_b, mix_w_in, cmp_pos, cmp_w1, cmp_w2, mix_w_out, ffn_w1, ffn_w3, ffn_w2,
           conv_w_in, conv_taps, conv_w_out, router_w, router_b, exp_w1, exp_w3, exp_w2):
    bsz, s, d = x.shape
    t = bsz * s
    mod = _ada_mod(c, ada_w, ada_b)

    sh1, sc1, g1, sh2, sc2, g2 = _mods(mod, 0, bsz, d)
    sb, nq, kvc, sw, gates = _inproj(x, sc1, sh1, _arrange_inproj_weight(mix_w_in[0]))
    o_sb = _sb_attention(sb)
    pos = cmp_pos[0].reshape(2, 2, NSA_CMP_STRIDE * HEAD_DIM)
    w2rep = jnp.tile(cmp_w2[0], (1, 1, NSA_REP)).astype(_BF16)
    cmp_kv = _compress(kvc, pos, cmp_w1[0].astype(_BF16), w2rep)
    o_nsa = _nsa_attention(nq, cmp_kv, sw, gates)
    x = _outproj_ln(o_sb, o_nsa, mix_w_out[0].astype(_BF16), x, g1, ln_g[0, 0], ln_b[0, 0])
    x = _ffn_ln(x, sc2, sh2, g2, ffn_w1[0].astype(_BF16), ffn_w3[0].astype(_BF16), ffn_w2[0].astype(_BF16),
                ln_g[0, 1], ln_b[0, 1])

    sh1, sc1, g1, sh2, sc2, g2 = _mods(mod, 1, bsz, d)
    x = _conv_ln(x, sc1, sh1, g1, conv_w_in[0].astype(_BF16), conv_taps[0], conv_w_out[0].astype(_BF16),
                 ln_g[1, 0], ln_b[1, 0])
    rw = jnp.pad(router_w[0], ((0, 0), (0, LANES - N_EXPERTS)))
    rb = jnp.pad(router_b[0], (0, LANES - N_EXPERTS)).reshape(1, LANES)
    h2, route, counts = _router(x, sc2, sh2, rw, rb)

    route_flat = route.reshape(t, LANES)
    e1 = route_flat[:, 0].astype(jnp.int32)
    e2 = route_flat[:, 1].astype(jnp.int32)
    p1 = route_flat[:, 2].astype(jnp.int32)
    p2 = route_flat[:, 3].astype(jnp.int32)
    cnt = counts[0, :N_EXPERTS].astype(jnp.int32)
    rb_rows = MOE_ROW_BLOCK
    padded = (cnt + rb_rows - 1) // rb_rows * rb_rows
    pends = jnp.cumsum(padded)
    pstarts = pends - padded
    dest1 = pstarts[e1] + p1
    dest2 = pstarts[e2] + p2
    n_blocks = -(-(t * 2) // rb_rows) + N_EXPERTS
    tok = jnp.arange(t, dtype=jnp.int32)
    row_tok = jnp.zeros((n_blocks * rb_rows,), jnp.int32).at[dest1].set(tok).at[dest2].set(tok)
    blk_e = jnp.minimum(jnp.searchsorted(pends, jnp.arange(n_blocks, dtype=jnp.int32) * rb_rows, side='right'),
                        N_EXPERTS - 1).astype(jnp.int32)
    n_used = (pends[-1:] // rb_rows).astype(jnp.int32)

    y = _experts(blk_e, n_used, row_tok, h2.reshape(t, d), exp_w1[0].astype(_BF16), exp_w3[0].astype(_BF16),
                 exp_w2[0].astype(_BF16))
    return _combine_ln(dest1, dest2, y, x, route, g2, ln_g[1, 1], ln_b[1, 1])
```

```python
import functools

import numpy as np
import jax
import jax.numpy as jnp
from jax import lax
from jax.experimental import pallas as pl
from jax.experimental.pallas import tpu as pltpu

HEAD_DIM = 64
SB_HEADS = 8
NSA_HEADS = 8
NSA_KV_GROUPS = 2
NSA_REP = NSA_HEADS // NSA_KV_GROUPS
NSA_CMP_LEN = 32
NSA_CMP_STRIDE = 16
NSA_SLC_BLOCK = 64
NSA_SLC_TOPK = 16
NSA_WINDOW = 512
SB_W = SB_HEADS * HEAD_DIM
NSA_Q_W = NSA_HEADS * HEAD_DIM
N_EXPERTS = 8
MOE_ROW_BLOCK = 512
DEPTH = 2
LN_EPS = 1e-5
DN_ALPHA = (2 * DEPTH) ** 0.25
NEG = -1e30
SEL_FORCE = 1e30
QK_SCALE = HEAD_DIM ** -0.5

LANES = 128
VMEM_LIMIT = 56 * 1024 * 1024

C_SB = 0
C_NQ = 3 * SB_W
C_CMP = C_NQ + NSA_Q_W
C_SW = C_CMP + 4 * HEAD_DIM
C_GATE = C_SW + 4 * 2 * HEAD_DIM
C_END = C_GATE + 2 * LANES

_F32 = jnp.float32
_BF16 = jnp.bfloat16


def _cparams(sem, vmem=VMEM_LIMIT):
    return pltpu.CompilerParams(dimension_semantics=sem, vmem_limit_bytes=vmem)


def _dot(a, b):
    return jnp.dot(a, b, preferred_element_type=_F32)


def _dot_t(a, b):
    return lax.dot_general(a, b, (((1,), (1,)), ((), ())), preferred_element_type=_F32)


def _split_bf16(x):
    hi = x.astype(_BF16)
    lo = (x - hi.astype(_F32)).astype(_BF16)
    return hi, lo


def _silu(x):
    return x / (1.0 + jnp.exp(-x))


def _layer_norm(r, g, b):
    mu = jnp.mean(r, axis=-1, keepdims=True)
    d = r - mu
    var = jnp.mean(d * d, axis=-1, keepdims=True)
    return d * lax.rsqrt(var + LN_EPS) * g + b


def _ada_kernel(c_ref, w_ref, b_ref, o_ref):
    cond = _silu(c_ref[...])
    o_ref[0] = jnp.dot(cond, w_ref[0], preferred_element_type=_F32,
                       precision=lax.Precision.HIGHEST) + b_ref[0]


def _ada_mod(c, ada_w, ada_b):
    depth, d, n = ada_w.shape
    bsz = c.shape[0]
    tn = n // 6
    return pl.pallas_call(
        _ada_kernel,
        grid=(depth, n // tn),
        in_specs=[pl.BlockSpec((bsz, d), lambda i, j: (0, 0)),
                  pl.BlockSpec((1, d, tn), lambda i, j: (i, 0, j)),
                  pl.BlockSpec((1, 1, tn), lambda i, j: (i, 0, j))],
        out_specs=pl.BlockSpec((1, bsz, tn), lambda i, j: (i, 0, j)),
        out_shape=jax.ShapeDtypeStruct((depth, bsz, n), _F32),
        compiler_params=_cparams(("parallel", "parallel")),
        name="ada_mod",
    )(c, ada_w, ada_b.reshape(depth, 1, n))


def _inproj_kernel(x_ref, sc_ref, sh_ref, w_ref, sb_ref, nq_ref, cmp_ref, sw_ref, gate_ref):
    h = (x_ref[0] * (1.0 + sc_ref[0]) + sh_ref[0]).astype(_BF16)
    sb_ref[0] = _dot(h, w_ref[:, C_SB:C_NQ]).astype(_BF16)
    nq_ref[0] = _dot(h, w_ref[:, C_NQ:C_CMP]).astype(_BF16)
    kvc = _dot(h, w_ref[:, C_CMP:C_SW])
    for j in range(4):
        cmp_ref[0, j] = kvc[:, j * HEAD_DIM:(j + 1) * HEAD_DIM]
    lane = lax.broadcasted_iota(jnp.int32, (1, LANES), 1)
    first = lane < HEAD_DIM
    for kind in range(4):
        a = _dot(h, w_ref[:, C_SW + kind * LANES:C_SW + (kind + 1) * LANES])
        r = pltpu.roll(a, HEAD_DIM, 1)
        g0 = jnp.where(first, a, r).astype(_BF16)
        g1 = jnp.where(first, r, a).astype(_BF16)
        sw_ref[0, :, (2 * kind) * 256:(2 * kind + 1) * 256] = jnp.concatenate([g0, g0], axis=1)
        sw_ref[0, :, (2 * kind + 1) * 256:(2 * kind + 2) * 256] = jnp.concatenate([g1, g1], axis=1)
    gate_ref[0] = _dot(h, w_ref[:, C_GATE:C_END])


def _inproj(x, sc, sh, w):
    bsz, s, d = x.shape
    tm = 512
    row = lambda b, i: (b, i, 0)
    return pl.pallas_call(
        _inproj_kernel,
        grid=(bsz, s // tm),
        in_specs=[pl.BlockSpec((1, tm, d), row),
                  pl.BlockSpec((1, 1, d), lambda b, i: (b, 0, 0)),
                  pl.BlockSpec((1, 1, d), lambda b, i: (b, 0, 0)),
                  pl.BlockSpec((d, C_END), lambda b, i: (0, 0))],
        out_specs=[pl.BlockSpec((1, tm, 3 * SB_W), row),
                   pl.BlockSpec((1, tm, NSA_Q_W), row),
                   pl.BlockSpec((1, 4, tm, HEAD_DIM), lambda b, i: (b, 0, i, 0)),
                   pl.BlockSpec((1, tm, 2048), row),
                   pl.BlockSpec((1, tm, 2 * LANES), row)],
        out_shape=[jax.ShapeDtypeStruct((bsz, s, 3 * SB_W), _BF16),
                   jax.ShapeDtypeStruct((bsz, s, NSA_Q_W), _BF16),
                   jax.ShapeDtypeStruct((bsz, 4, s, HEAD_DIM), _F32),
                   jax.ShapeDtypeStruct((bsz, s, 2048), _BF16),
                   jax.ShapeDtypeStruct((bsz, s, 2 * LANES), _F32)],
        compiler_params=_cparams(("parallel", "parallel")),
        name="l0_inproj",
    )(x, sc, sh, w)


SB_TQ = 512
SB_TK = 128
SB_UNROLL = 4


def _sb_kernel(q_ref, k_ref, v_ref, tri_ref, o_ref, acc_ref, run_ref):
    tq, tk = SB_TQ, SB_TK
    qi = pl.program_id(2)
    q = q_ref[0]
    lane = lax.broadcasted_iota(jnp.int32, (1, LANES), 1)
    first = lane < HEAD_DIM
    zero = jnp.zeros_like(q)
    q2 = jnp.concatenate([jnp.where(first, q, zero), jnp.where(first, zero, q)], axis=0)
    tri = tri_ref[...]
    acc_ref[...] = jnp.zeros_like(acc_ref)
    run_ref[...] = jnp.zeros_like(run_ref)

    def scores(qrows, key_off):
        return _dot_t(qrows, k_ref[0, pl.ds(key_off, tk), :])

    def suffix(z, causal):
        nlf = jnp.maximum(z, 0.0) + jnp.log(1.0 + jnp.exp(-jnp.abs(z)))
        if causal is not None:
            nlf = jnp.where(causal, nlf, 0.0)
        hi_f = lax.bitcast_convert_type(
            lax.bitcast_convert_type(nlf, jnp.uint32) & jnp.uint32(0xFFFF0000), _F32)
        hi = hi_f.astype(_BF16)
        lo = (nlf - hi_f).astype(_BF16)
        return _dot(jnp.concatenate([hi, lo], axis=1), tri)

    def weigh(z, ext, run, causal, key_off):
        w = jnp.exp(z + ext[:, :tk] + run)
        if causal is not None:
            w = jnp.where(causal, w, 0.0)
        return _dot(w.astype(_BF16), v_ref[0, pl.ds(key_off, tk), :]), run + ext[:, tk:]

    n_diag = tq // tk
    diag = []
    for kd in reversed(range(n_diag)):
        r0 = kd * tk
        key_off = pl.multiple_of(qi * tq + r0, tk)
        row = r0 + lax.broadcasted_iota(jnp.int32, (tq - r0, tk), 0)
        col = r0 + lax.broadcasted_iota(jnp.int32, (tq - r0, tk), 1)
        causal = jnp.concatenate([col < row] * 2, axis=0)
        z = scores(jnp.concatenate([q2[r0:tq], q2[tq + r0:2 * tq]], axis=0), key_off)
        diag.append((r0, key_off, causal, z))
    diag = [(r0, key_off, causal, z, suffix(z, causal)) for r0, key_off, causal, z in diag]
    for r0, key_off, causal, z, ext in diag:
        n = tq - r0
        run = jnp.concatenate([run_ref[r0:tq, :], run_ref[tq + r0:2 * tq, :]], axis=0)
        pv, run = weigh(z, ext, run, causal, key_off)
        for h in range(2):
            rows = slice(h * tq + r0, (h + 1) * tq)
            acc_ref[rows, :] += pv[h * n:(h + 1) * n]
            run_ref[rows, :] = run[h * n:(h + 1) * n]

    def below(jj, carry):
        offs = [pl.multiple_of((qi * n_diag - 1 - SB_UNROLL * jj - u) * tk, tk) for u in range(SB_UNROLL)]
        zs = [scores(q2, off) for off in offs]
        exts = [suffix(z, None) for z in zs]
        run = run_ref[...]
        acc = acc_ref[...]
        for z, ext, off in zip(zs, exts, offs):
            pv, run = weigh(z, ext, run, None, off)
            acc = acc + pv
        acc_ref[...] = acc
        run_ref[...] = run
        return carry
    lax.fori_loop(0, qi * (n_diag // SB_UNROLL), below, 0)
    o_ref[0] = jnp.where(first, acc_ref[0:tq, :], acc_ref[tq:2 * tq, :]).astype(o_ref.dtype)


def _sb_attention(sb):
    bsz, s, _ = sb.shape
    tq, tk = SB_TQ, SB_TK
    npair = SB_W // LANES
    suffix = (np.arange(tk)[:, None] >= np.arange(tk)[None, :]).astype(np.float32)
    tri = -np.tile(np.concatenate([suffix, np.ones((tk, tk), np.float32)], axis=1), (2, 1))
    return pl.pallas_call(
        _sb_kernel,
        grid=(bsz, npair, s // tq),
        in_specs=[pl.BlockSpec((1, tq, LANES), lambda b, p, i: (b, i, p)),
                  pl.BlockSpec((1, s, LANES), lambda b, p, i: (b, 0, npair + p)),
                  pl.BlockSpec((1, s, LANES), lambda b, p, i: (b, 0, 2 * npair + p)),
                  pl.BlockSpec((2 * tk, 2 * tk), lambda b, p, i: (0, 0))],
        out_specs=pl.BlockSpec((1, tq, LANES), lambda b, p, i: (b, i, p)),
        out_shape=jax.ShapeDtypeStruct((bsz, s, SB_W), _BF16),
        scratch_shapes=[pltpu.VMEM((2 * tq, LANES), _F32), pltpu.VMEM((2 * tq, LANES), _F32)],
        compiler_params=_cparams(("parallel", "parallel", "arbitrary")),
        name="sb_attention",
    )(sb, sb, sb, jnp.asarray(tri, _BF16))


def _compress_kernel(x_ref, pos_ref, w1_ref, w2_ref, o_ref):
    half = w1_ref.shape[1] // 2
    ch = x_ref[0, 0]
    top = _dot((ch + pos_ref[0, 0:1, :]).astype(_BF16), w1_ref[0, :half, :])
    bot = _dot((ch + pos_ref[0, 1:2, :]).astype(_BF16), w1_ref[0, half:, :])
    n = ch.shape[0]
    pre = top + pltpu.roll(bot, n - 1, 0)
    o_ref[0, 0] = _dot(_silu(pre).astype(_BF16), w2_ref[0]).astype(o_ref.dtype)


def _compress(kvc, pos, w1, w2rep):
    bsz, _, s, dh = kvc.shape
    n_chunk = s // NSA_CMP_STRIDE
    cw = NSA_CMP_STRIDE * dh
    x = kvc.reshape(bsz, 4, n_chunk, cw)
    hid = w1.shape[-1]
    return pl.pallas_call(
        _compress_kernel,
        grid=(bsz, 4),
        in_specs=[pl.BlockSpec((1, 1, n_chunk, cw), lambda b, j: (b, j, 0, 0)),
                  pl.BlockSpec((1, 2, cw), lambda b, j: (j // 2, 0, 0)),
                  pl.BlockSpec((1, 2 * cw, hid), lambda b, j: (j // 2, 0, 0)),
                  pl.BlockSpec((1, hid, 4 * dh), lambda b, j: (j // 2, 0, 0))],
        out_specs=pl.BlockSpec((1, 1, n_chunk, 4 * dh), lambda b, j: (b, j, 0, 0)),
        out_shape=jax.ShapeDtypeStruct((bsz, 4, n_chunk, 4 * dh), _BF16),
        compiler_params=_cparams(("parallel", "parallel")),
        name="nsa_compress",
    )(x, pos, w1, w2rep)


NSA_TQ = 128
NSA_TK = 512
WIN_SPAN = NSA_WINDOW + NSA_TQ


def _nsa_kernel(qa_ref, qb_ref, kc_ref, vc_ref, ks_ref, vs_ref, kw_ref, vw_ref, ga_ref, gb_ref, ov_ref, ex_ref,
                o_ref):
    tq, tk = NSA_TQ, NSA_TK
    s_len = ks_ref.shape[1]
    n_tiles = s_len // tq
    n_cmp = kc_ref.shape[2]
    n_slc = s_len // NSA_SLC_BLOCK
    gw = NSA_REP * HEAD_DIM
    i = pl.program_id(2)
    head_of_lane = lax.broadcasted_iota(jnp.int32, (1, gw), 1) // HEAD_DIM

    def unstack(o):
        out = jnp.zeros((tq, gw), _F32)
        for r in range(NSA_REP):
            out = jnp.where(head_of_lane == r, o[r * tq:(r + 1) * tq], out)
        return out

    def tile4(x):
        return jnp.concatenate([x] * NSA_REP, axis=0)

    class Tile:
        pass
    tiles = []
    for q_ref, g_ref, t0, n_blk in ((qa_ref, ga_ref, i * tq, s_len // (2 * tk)),
                                    (qb_ref, gb_ref, (n_tiles - 1 - i) * tq, s_len // tk)):
        t = Tile()
        q = q_ref[0]
        zero = jnp.zeros_like(q)
        t.qs = jnp.concatenate([jnp.where(head_of_lane == r, q, zero) for r in range(NSA_REP)], axis=0)
        t.t0, t.n_blk, t.g_ref = t0, n_blk, g_ref
        t.t_row = t0 + lax.broadcasted_iota(jnp.int32, (tq, 1), 0)
        t.start = pl.multiple_of(jnp.maximum(t0 - NSA_WINDOW, 0), tq)
        tiles.append(t)

    for t in tiles:
        t.s_c = _dot_t(t.qs, kc_ref[0, 0])
        t.s_w = _dot_t(t.qs, kw_ref[0, pl.ds(t.start, WIN_SPAN), :])
        t.s_s = [_dot_t(t.qs, ks_ref[0, j * tk:(j + 1) * tk, :]) for j in range(t.n_blk)]

    cmp_last = lax.broadcasted_iota(jnp.int32, (1, n_cmp), 1) * NSA_CMP_STRIDE + (NSA_CMP_LEN - 1)
    for t in tiles:
        mask_c = tile4(cmp_last <= t.t_row)
        sm = jnp.where(mask_c, t.s_c, NEG)
        e_c = jnp.where(mask_c, jnp.exp(sm - jnp.max(sm, axis=1, keepdims=True)), 0.0)
        l_c = jnp.sum(e_c, axis=1, keepdims=True)
        t.p_c = e_c / jnp.where(l_c == 0.0, 1.0, l_c)
    for t in tiles:
        t.o_cmp = unstack(_dot(t.p_c.astype(_BF16), vc_ref[0, 0]))
        p_sum = t.p_c[0:tq] + t.p_c[tq:2 * tq] + t.p_c[2 * tq:3 * tq] + t.p_c[3 * tq:4 * tq]
        hi, lo = _split_bf16(p_sum)
        t.imp = _dot_t(ov_ref[...], jnp.concatenate([hi, lo], axis=1))[:n_slc]

    for t in tiles:
        diff = t.t_row - (t.start + lax.broadcasted_iota(jnp.int32, (1, WIN_SPAN), 1))
        bias_w = jnp.where((diff >= 0) & (diff < NSA_WINDOW), 0.0, NEG)
        sm = t.s_w + tile4(bias_w)
        e_w = jnp.exp(sm - jnp.max(sm, axis=1, keepdims=True))
        t.l_w = jnp.sum(e_w, axis=1, keepdims=True)
        t.e_w = e_w.astype(_BF16)
    for t in tiles:
        t.o_win = unstack(_dot(t.e_w, vw_ref[0, pl.ds(t.start, WIN_SPAN), :]) / t.l_w)

    blk = lax.broadcasted_iota(jnp.int32, (n_slc, 1), 0)
    for t in tiles:
        cur = (t.t0 + lax.broadcasted_iota(jnp.int32, (1, tq), 1)) // NSA_SLC_BLOCK
        valid = blk <= cur
        forced = (blk == 0) | (blk == cur) | (blk == cur - 1)
        val = jnp.where(valid, jnp.where(forced, SEL_FORCE, t.imp), NEG)
        rank = jnp.zeros((n_slc, tq), _F32)
        for b in range(n_slc):
            vb = val[b:b + 1, :]
            ahead = (vb > val) | ((vb == val) & (blk > b))
            rank = rank + jnp.where(ahead, 1.0, 0.0)
        sel_t = jnp.where(rank < float(min(NSA_SLC_TOPK, n_slc)), 1.0, 0.0)
        sel_t = jnp.concatenate([sel_t, jnp.zeros((LANES - n_slc, tq), _F32)], axis=0)
        t.sel = sel_t.T.astype(_BF16)

    for t in tiles:
        masked = []
        for j in range(t.n_blk):
            chosen = _dot(t.sel, ex_ref[:, j * tk:(j + 1) * tk])
            kpos = j * tk + lax.broadcasted_iota(jnp.int32, (1, tk), 1)
            bias = jnp.where((chosen > 0.5) & (kpos <= t.t_row), 0.0, NEG)
            masked.append(t.s_s[j] + tile4(bias))
        m = masked[0].max(axis=1, keepdims=True)
        for sb in masked[1:]:
            m = jnp.maximum(m, sb.max(axis=1, keepdims=True))
        e_s = [jnp.exp(sb - m) for sb in masked]
        t.l_s = sum(e.sum(axis=1, keepdims=True) for e in e_s)
        t.e_s = [e.astype(_BF16) for e in e_s]
    for t in tiles:
        acc = _dot(t.e_s[0], vs_ref[0, 0:tk, :])
        for j in range(1, t.n_blk):
            acc = acc + _dot(t.e_s[j], vs_ref[0, j * tk:(j + 1) * tk, :])
        t.o_slc = unstack(acc / t.l_s)

    for half, t in enumerate(tiles):
        g = 1.0 / (1.0 + jnp.exp(-t.g_ref[0]))
        out = jnp.zeros((tq, gw), _F32)
        for branch, o_b in enumerate((t.o_cmp, t.o_slc, t.o_win)):
            gm = jnp.zeros((tq, gw), _F32)
            for r in range(NSA_REP):
                c = 3 * r + branch
                gm = jnp.where(head_of_lane == r, g[:, c:c + 1], gm)
            out = out + gm * o_b
        o_ref[0, half, 0] = out.astype(o_ref.dtype)


def _nsa_attention(nq, cmp_kv, sw, gates):
    bsz, s, _ = nq.shape
    tq = NSA_TQ
    n_cmp = cmp_kv.shape[2]
    n_slc = s // NSA_SLC_BLOCK
    gw = 4 * HEAD_DIM
    cmp_start = np.arange(n_cmp) * NSA_CMP_STRIDE
    slc_start = np.arange(LANES) * NSA_SLC_BLOCK
    ov = ((cmp_start[:, None] < slc_start[None, :] + NSA_SLC_BLOCK)
          & (cmp_start[:, None] + NSA_CMP_LEN > slc_start[None, :])
          & (np.arange(LANES)[None, :] < n_slc)).astype(np.float32)
    ov[n_cmp - 1:] = 0.0
    ov2 = np.concatenate([ov.T, ov.T], axis=1)
    ex = (np.arange(LANES)[:, None] == (np.arange(s)[None, :] // NSA_SLC_BLOCK)).astype(np.float32)
    kv_spec = lambda kind: pl.BlockSpec((1, s, gw), lambda b, g, i, kind=kind: (b, 0, 2 * kind + g))
    n_tiles = s // tq
    n_pair = n_tiles // 2
    out = pl.pallas_call(
        _nsa_kernel,
        grid=(bsz, NSA_KV_GROUPS, n_pair),
        in_specs=[pl.BlockSpec((1, tq, gw), lambda b, g, i: (b, i, g)),
                  pl.BlockSpec((1, tq, gw), lambda b, g, i: (b, n_tiles - 1 - i, g)),
                  pl.BlockSpec((1, 1, n_cmp, gw), lambda b, g, i: (b, g, 0, 0)),
                  pl.BlockSpec((1, 1, n_cmp, gw), lambda b, g, i: (b, 2 + g, 0, 0)),
                  kv_spec(0), kv_spec(1), kv_spec(2), kv_spec(3),
                  pl.BlockSpec((1, tq, LANES), lambda b, g, i: (b, i, g)),
                  pl.BlockSpec((1, tq, LANES), lambda b, g, i: (b, n_tiles - 1 - i, g)),
                  pl.BlockSpec((LANES, 2 * n_cmp), lambda b, g, i: (0, 0)),
                  pl.BlockSpec((LANES, s), lambda b, g, i: (0, 0))],
        out_specs=pl.BlockSpec((1, 2, 1, tq, gw), lambda b, g, i: (b, 0, i, 0, g)),
        out_shape=jax.ShapeDtypeStruct((bsz, 2, n_pair, tq, NSA_Q_W), _BF16),
        compiler_params=_cparams(("parallel", "parallel", "arbitrary")),
        name="nsa_attention",
    )(nq, nq, cmp_kv, cmp_kv, sw, sw, sw, sw, gates, gates, jnp.asarray(ov2, _BF16), jnp.asarray(ex, _BF16))
    return jnp.concatenate([out[:, 0], out[:, 1, ::-1]], axis=1).reshape(bsz, s, NSA_Q_W)


def _outproj_kernel(a_ref, b_ref, wa_ref, wb_ref, x_ref, g_ref, lg_ref, lb_ref, o_ref):
    y = _dot(a_ref[0], wa_ref[...]) + _dot(b_ref[0], wb_ref[...])
    r = DN_ALPHA * x_ref[0] + (1.0 + g_ref[0]) * y
    o_ref[0] = _layer_norm(r, lg_ref[...], lb_ref[...])


def _outproj_ln(o_sb, o_nsa, w_out, x, gate, ln_g, ln_b):
    bsz, s, d = x.shape
    tm = 512
    row = lambda b, i: (b, i, 0)
    vec = pl.BlockSpec((1, d), lambda b, i: (0, 0))
    return pl.pallas_call(
        _outproj_kernel,
        grid=(bsz, s // tm),
        in_specs=[pl.BlockSpec((1, tm, SB_W), row), pl.BlockSpec((1, tm, NSA_Q_W), row),
                  pl.BlockSpec((SB_W, d), lambda b, i: (0, 0)),
                  pl.BlockSpec((NSA_Q_W, d), lambda b, i: (1, 0)),
                  pl.BlockSpec((1, tm, d), row),
                  pl.BlockSpec((1, 1, d), lambda b, i: (b, 0, 0)), vec, vec],
        out_specs=pl.BlockSpec((1, tm, d), row),
        out_shape=jax.ShapeDtypeStruct((bsz, s, d), _F32),
        compiler_params=_cparams(("parallel", "parallel")),
        name="l0_outproj_ln",
    )(o_sb, o_nsa, w_out, w_out, x, gate, ln_g.reshape(1, d), ln_b.reshape(1, d))


FFN_TF = 256


def _swiglu(xb, w1, w3, w2):
    ff = w1.shape[1]
    acc = jnp.zeros((xb.shape[0], w2.shape[1]), _F32)
    for c in range(ff // FFN_TF):
        cs = slice(c * FFN_TF, (c + 1) * FFN_TF)
        a = _silu(_dot(xb, w1[:, cs])) * _dot(xb, w3[:, cs])
        acc = acc + _dot(a.astype(_BF16), w2[cs, :])
    return acc


def _ffn_kernel(x_ref, sc_ref, sh_ref, g_ref, w1_ref, w3_ref, w2_ref, lg_ref, lb_ref, o_ref):
    x = x_ref[0]
    h = (x * (1.0 + sc_ref[0]) + sh_ref[0]).astype(_BF16)
    r = DN_ALPHA * x + (1.0 + g_ref[0]) * _swiglu(h, w1_ref, w3_ref, w2_ref)
    o_ref[0] = _layer_norm(r, lg_ref[...], lb_ref[...])


def _ffn_ln(x, sc, sh, gate, w1, w3, w2, ln_g, ln_b):
    bsz, s, d = x.shape
    ff = w1.shape[1]
    tm = 512
    row = lambda b, i: (b, i, 0)
    mod = pl.BlockSpec((1, 1, d), lambda b, i: (b, 0, 0))
    vec = pl.BlockSpec((1, d), lambda b, i: (0, 0))
    return pl.pallas_call(
        _ffn_kernel,
        grid=(bsz, s // tm),
        in_specs=[pl.BlockSpec((1, tm, d), row), mod, mod, mod,
                  pl.BlockSpec((d, ff), lambda b, i: (0, 0)),
                  pl.BlockSpec((d, ff), lambda b, i: (0, 0)),
                  pl.BlockSpec((ff, d), lambda b, i: (0, 0)), vec, vec],
        out_specs=pl.BlockSpec((1, tm, d), row),
        out_shape=jax.ShapeDtypeStruct((bsz, s, d), _F32),
        compiler_params=_cparams(("parallel", "parallel")),
        name="l0_ffn_ln",
    )(x, sc, sh, gate, w1, w3, w2, ln_g.reshape(1, d), ln_b.reshape(1, d))


def _conv_kernel(x_ref, sc_ref, sh_ref, g_ref, win_ref, taps_ref, wout_ref, lg_ref, lb_ref, o_ref, tail_ref):
    ts, d = x_ref.shape[1], x_ref.shape[2]

    @pl.when(pl.program_id(1) == 0)
    def _():
        tail_ref[...] = jnp.zeros_like(tail_ref)

    x = x_ref[0]
    h = (x * (1.0 + sc_ref[0]) + sh_ref[0]).astype(_BF16)
    z = _dot(h, win_ref[:, d:2 * d]) * _dot(h, win_ref[:, 2 * d:3 * d])
    row = lax.broadcasted_iota(jnp.int32, (ts, 1), 0)
    prev1 = tail_ref[7:8, :]
    prev2 = tail_ref[6:7, :]
    z1 = jnp.where(row == 0, prev1, pltpu.roll(z, 1, 0))
    z2 = jnp.where(row == 0, prev2, jnp.where(row == 1, prev1, pltpu.roll(z, 2, 0)))
    zc = taps_ref[0:1, :] * z2 + taps_ref[1:2, :] * z1 + taps_ref[2:3, :] * z
    tail_ref[...] = z[ts - 8:ts, :]
    y = _dot((_dot(h, win_ref[:, 0:d]) * zc).astype(_BF16), wout_ref[...])
    r = DN_ALPHA * x + (1.0 + g_ref[0]) * y
    o_ref[0] = _layer_norm(r, lg_ref[...], lb_ref[...])


def _conv_ln(x, sc, sh, gate, w_in, taps, w_out, ln_g, ln_b):
    bsz, s, d = x.shape
    ts = 512
    row = lambda b, i: (b, i, 0)
    mod = pl.BlockSpec((1, 1, d), lambda b, i: (b, 0, 0))
    vec = pl.BlockSpec((1, d), lambda b, i: (0, 0))
    return pl.pallas_call(
        _conv_kernel,
        grid=(bsz, s // ts),
        in_specs=[pl.BlockSpec((1, ts, d), row), mod, mod, mod,
                  pl.BlockSpec((d, 3 * d), lambda b, i: (0, 0)),
                  pl.BlockSpec((taps.shape[0], d), lambda b, i: (0, 0)),
                  pl.BlockSpec((d, d), lambda b, i: (0, 0)), vec, vec],
        out_specs=pl.BlockSpec((1, ts, d), row),
        out_shape=jax.ShapeDtypeStruct((bsz, s, d), _F32),
        scratch_shapes=[pltpu.VMEM((8, d), _F32)],
        compiler_params=_cparams(("arbitrary", "arbitrary")),
        name="l1_conv_ln",
    )(x, sc, sh, gate, w_in, taps, w_out, ln_g.reshape(1, d), ln_b.reshape(1, d))


ROUTE_TM = 512


def _router_kernel(x_ref, sc_ref, sh_ref, rw_ref, rb_ref, tri_ref, h_ref, route_ref, cnt_ref, run_ref):
    tm = ROUTE_TM

    @pl.when((pl.program_id(0) == 0) & (pl.program_id(1) == 0))
    def _():
        run_ref[...] = jnp.zeros_like(run_ref)

    h = x_ref[0] * (1.0 + sc_ref[0]) + sh_ref[0]
    h_ref[0] = h
    lane = lax.broadcasted_iota(jnp.int32, (1, LANES), 1)
    logits = jnp.dot(h, rw_ref[...], preferred_element_type=_F32,
                     precision=lax.Precision.HIGHEST) + rb_ref[...]
    logits = jnp.where(lane < N_EXPERTS, logits, -3e38)
    m1 = jnp.max(logits, axis=1, keepdims=True)
    e1 = jnp.min(jnp.where(logits == m1, lane, LANES), axis=1, keepdims=True)
    rest = jnp.where(lane == e1, -3e38, logits)
    m2 = jnp.max(rest, axis=1, keepdims=True)
    e2 = jnp.min(jnp.where(rest == m2, lane, LANES), axis=1, keepdims=True)
    ex = jnp.exp(m2 - m1)
    g1 = 1.0 / (1.0 + ex)
    g2 = ex / (1.0 + ex)
    oh1 = jnp.where(lane == e1, 1.0, 0.0)
    oh2 = jnp.where(lane == e2, 1.0, 0.0)
    oh = oh1 + oh2
    before = run_ref[...] + _dot(tri_ref[...], oh.astype(_BF16))
    p1 = jnp.sum(oh1 * before, axis=1, keepdims=True)
    p2 = jnp.sum(oh2 * before, axis=1, keepdims=True)
    run_ref[...] += jnp.sum(oh, axis=0, keepdims=True)
    cnt_ref[...] = run_ref[...]
    vals = (e1.astype(_F32), e2.astype(_F32), p1, p2, g1, g2)
    out = jnp.zeros((tm, LANES), _F32)
    for k, v in enumerate(vals):
        out = jnp.where(lane == k, v, out)
    route_ref[0] = out


def _router(x, sc, sh, rw, rb):
    bsz, s, d = x.shape
    tm = ROUTE_TM
    row = lambda b, i: (b, i, 0)
    mod = pl.BlockSpec((1, 1, d), lambda b, i: (b, 0, 0))
    tri = (np.arange(tm)[None, :] < np.arange(tm)[:, None]).astype(np.float32)
    return pl.pallas_call(
        _router_kernel,
        grid=(bsz, s // tm),
        in_specs=[pl.BlockSpec((1, tm, d), row), mod, mod,
                  pl.BlockSpec((d, LANES), lambda b, i: (0, 0)),
                  pl.BlockSpec((1, LANES), lambda b, i: (0, 0)),
                  pl.BlockSpec((tm, tm), lambda b, i: (0, 0))],
        out_specs=[pl.BlockSpec((1, tm, d), row),
                   pl.BlockSpec((1, tm, LANES), row),
                   pl.BlockSpec((1, LANES), lambda b, i: (0, 0))],
        out_shape=[jax.ShapeDtypeStruct((bsz, s, d), _F32),
                   jax.ShapeDtypeStruct((bsz, s, LANES), _F32),
                   jax.ShapeDtypeStruct((1, LANES), _F32)],
        scratch_shapes=[pltpu.VMEM((1, LANES), _F32)],
        compiler_params=_cparams(("arbitrary", "arbitrary")),
        name="l1_router",
    )(x, sc, sh, rw, rb, jnp.asarray(tri, _BF16))


GATHER_UNROLL = 8


def _row_gather(src_hbm, idx_ref, base, dst_ref, sem, n):
    def issue(r, carry):
        tok = idx_ref[base + r]
        pltpu.make_async_copy(src_hbm.at[pl.ds(tok, 1)], dst_ref.at[pl.ds(r, 1)], sem).start()
        return carry
    lax.fori_loop(0, n, issue, 0, unroll=GATHER_UNROLL)


def _row_gather_wait(src_hbm, dst_ref, sem, n):
    pltpu.make_async_copy(src_hbm.at[pl.ds(0, n)], dst_ref, sem).wait()


def _expert_kernel(blk_e_ref, n_used_ref, row_tok_ref, h_hbm, w1_ref, w3_ref, w2_ref, y_ref, xbuf, sems):
    i = pl.program_id(0)
    n_used = n_used_ref[0]
    rb = MOE_ROW_BLOCK
    slot = lax.rem(i, 2)

    @pl.when((i == 0) & (n_used > 0))
    def _():
        _row_gather(h_hbm, row_tok_ref, 0, xbuf.at[0], sems.at[0], rb)

    @pl.when(i + 1 < n_used)
    def _():
        _row_gather(h_hbm, row_tok_ref, (i + 1) * rb, xbuf.at[1 - slot], sems.at[1 - slot], rb)

    @pl.when(i < n_used)
    def _():
        _row_gather_wait(h_hbm, xbuf.at[slot], sems.at[slot], rb)
        y_ref[...] = _swiglu(xbuf[slot].astype(_BF16), w1_ref.at[0], w3_ref.at[0], w2_ref.at[0])

    @pl.when(i >= n_used)
    def _():
        y_ref[...] = jnp.zeros_like(y_ref)


def _experts(blk_e, n_used, row_tok, h_flat, w1, w3, w2):
    t, d = h_flat.shape
    n_blocks = blk_e.shape[0]
    rb = MOE_ROW_BLOCK
    ff = w1.shape[2]
    grid_spec = pltpu.PrefetchScalarGridSpec(
        num_scalar_prefetch=3,
        grid=(n_blocks,),
        in_specs=[pl.BlockSpec(memory_space=pl.ANY),
                  pl.BlockSpec((1, d, ff), lambda i, be, nu, rt: (be[i], 0, 0)),
                  pl.BlockSpec((1, d, ff), lambda i, be, nu, rt: (be[i], 0, 0)),
                  pl.BlockSpec((1, ff, d), lambda i, be, nu, rt: (be[i], 0, 0))],
        out_specs=pl.BlockSpec((rb, d), lambda i, be, nu, rt: (i, 0)),
        scratch_shapes=[pltpu.VMEM((2, rb, d), _F32), pltpu.SemaphoreType.DMA((2,))],
    )
    return pl.pallas_call(
        _expert_kernel,
        grid_spec=grid_spec,
        out_shape=jax.ShapeDtypeStruct((n_blocks * rb, d), _F32),
        compiler_params=_cparams(("arbitrary",)),
        name="l1_experts",
    )(blk_e, n_used, row_tok, h_flat, w1, w3, w2)


COMB_TM = 256


def _combine_kernel(d1_ref, d2_ref, y_hbm, x_ref, route_ref, g_ref, lg_ref, lb_ref, o_ref, y1buf, y2buf, sems):
    tm = COMB_TM
    i = pl.program_id(0) * pl.num_programs(1) + pl.program_id(1)
    n = pl.num_programs(0) * pl.num_programs(1)
    slot = lax.rem(i, 2)

    def gather(tile, s):
        _row_gather(y_hbm, d1_ref, tile * tm, y1buf.at[s], sems.at[0, s], tm)
        _row_gather(y_hbm, d2_ref, tile * tm, y2buf.at[s], sems.at[1, s], tm)

    @pl.when(i == 0)
    def _():
        gather(0, 0)

    @pl.when(i + 1 < n)
    def _():
        gather(i + 1, 1 - slot)

    _row_gather_wait(y_hbm, y1buf.at[slot], sems.at[0, slot], tm)
    _row_gather_wait(y_hbm, y2buf.at[slot], sems.at[1, slot], tm)
    route = route_ref[0]
    y = route[:, 4:5] * y1buf[slot] + route[:, 5:6] * y2buf[slot]
    r = DN_ALPHA * x_ref[0] + (1.0 + g_ref[0]) * y
    o_ref[0] = _layer_norm(r, lg_ref[...], lb_ref[...])


def _combine_ln(dest1, dest2, y, x, route, gate, ln_g, ln_b):
    bsz, s, d = x.shape
    tm = COMB_TM
    row = lambda b, i, d1, d2: (b, i, 0)
    vec = pl.BlockSpec((1, d), lambda b, i, d1, d2: (0, 0))
    grid_spec = pltpu.PrefetchScalarGridSpec(
        num_scalar_prefetch=2,
        grid=(bsz, s // tm),
        in_specs=[pl.BlockSpec(memory_space=pl.ANY),
                  pl.BlockSpec((1, tm, d), row),
                  pl.BlockSpec((1, tm, LANES), row),
                  pl.BlockSpec((1, 1, d), lambda b, i, d1, d2: (b, 0, 0)), vec, vec],
        out_specs=pl.BlockSpec((1, tm, d), row),
        scratch_shapes=[pltpu.VMEM((2, tm, d), _F32), pltpu.VMEM((2, tm, d), _F32),
                        pltpu.SemaphoreType.DMA((2, 2))],
    )
    return pl.pallas_call(
        _combine_kernel,
        grid_spec=grid_spec,
        out_shape=jax.ShapeDtypeStruct((bsz, s, d), _F32),
        compiler_params=_cparams(("arbitrary", "arbitrary")),
        name="l1_combine_ln",
    )(dest1, dest2, y, x, route, gate, ln_g.reshape(1, d), ln_b.reshape(1, d))


def _arrange_inproj_weight(w):
    d = w.shape[0]
    kv0 = 3 * SB_W + NSA_Q_W
    g0 = kv0 + 6 * NSA_KV_GROUPS * HEAD_DIM
    per_group = NSA_REP * 3
    pad = jnp.zeros((d, LANES - per_group), w.dtype)
    cols = [w[:, :SB_W] * QK_SCALE, w[:, SB_W:3 * SB_W], w[:, 3 * SB_W:kv0] * QK_SCALE, w[:, kv0:g0]]
    for g in range(NSA_KV_GROUPS):
        cols += [w[:, g0 + g * per_group:g0 + (g + 1) * per_group], pad]
    return jnp.concatenate(cols, axis=1).astype(_BF16)


def _mods(mod, i, bsz, d):
    m = mod[i].reshape(bsz, 6, 1, d)
    return [m[:, k] for k in range(6)]


def kernel(x, c, ada_w, ada_b, ln_g, ln_b, mix_w_in, cmp_pos, cmp_w1, cmp_w2, mix_w_out, ffn_w1, ffn_w3, ffn_w2,
           conv_w_in, conv_taps, conv_w_out, router_w, router_b, exp_w1, exp_w3, exp_w2):
    bsz, s, d = x.shape
    t = bsz * s
    mod = _ada_mod(c, ada_w, ada_b)

    sh1, sc1, g1, sh2, sc2, g2 = _mods(mod, 0, bsz, d)
    sb, nq, kvc, sw, gates = _inproj(x, sc1, sh1, _arrange_inproj_weight(mix_w_in[0]))
    o_sb = _sb_attention(sb)
    pos = cmp_pos[0].reshape(2, 2, NSA_CMP_STRIDE * HEAD_DIM)
    w2rep = jnp.tile(cmp_w2[0], (1, 1, NSA_REP)).astype(_BF16)
    cmp_kv = _compress(kvc, pos, cmp_w1[0].astype(_BF16), w2rep)
    o_nsa = _nsa_attention(nq, cmp_kv, sw, gates)
    x = _outproj_ln(o_sb, o_nsa, mix_w_out[0].astype(_BF16), x, g1, ln_g[0, 0], ln_b[0, 0])
    x = _ffn_ln(x, sc2, sh2, g2, ffn_w1[0].astype(_BF16), ffn_w3[0].astype(_BF16), ffn_w2[0].astype(_BF16),
                ln_g[0, 1], ln_b[0, 1])

    sh1, sc1, g1, sh2, sc2, g2 = _mods(mod, 1, bsz, d)
    x = _conv_ln(x, sc1, sh1, g1, conv_w_in[0].astype(_BF16), conv_taps[0], conv_w_out[0].astype(_BF16),
                 ln_g[1, 0], ln_b[1, 0])
    rw = jnp.pad(router_w[0], ((0, 0), (0, LANES - N_EXPERTS)))
    rb = jnp.pad(router_b[0], (0, LANES - N_EXPERTS)).reshape(1, LANES)
    h2, route, counts = _router(x, sc2, sh2, rw, rb)

    route_flat = route.reshape(t, LANES)
    e1 = route_flat[:, 0].astype(jnp.int32)
    e2 = route_flat[:, 1].astype(jnp.int32)
    p1 = route_flat[:, 2].astype(jnp.int32)
    p2 = route_flat[:, 3].astype(jnp.int32)
    cnt = counts[0, :N_EXPERTS].astype(jnp.int32)
    rb_rows = MOE_ROW_BLOCK
    padded = (cnt + rb_rows - 1) // rb_rows * rb_rows
    pends = jnp.cumsum(padded)
    pstarts = pends - padded
    dest1 = pstarts[e1] + p1
    dest2 = pstarts[e2] + p2
    n_blocks = -(-(t * 2) // rb_rows) + N_EXPERTS
    tok = jnp.arange(t, dtype=jnp.int32)
    row_tok = jnp.zeros((n_blocks * rb_rows,), jnp.int32).at[dest1].set(tok).at[dest2].set(tok)
    blk_e = jnp.minimum(jnp.searchsorted(pends, jnp.arange(n_blocks, dtype=jnp.int32) * rb_rows, side='right'),
                        N_EXPERTS - 1).astype(jnp.int32)
    n_used = (pends[-1:] // rb_rows).astype(jnp.int32)

    y = _experts(blk_e, n_used, row_tok, h2.reshape(t, d), exp_w1[0].astype(_BF16), exp_w3[0].astype(_BF16),
                 exp_w2[0].astype(_BF16))
    return _combine_ln(dest1, dest2, y, x, route, g2, ln_g[1, 1], ln_b[1, 1])
```

```python
import functools

import numpy as np
import jax
import jax.numpy as jnp
from jax import lax
from jax.experimental import pallas as pl
from jax.experimental.pallas import tpu as pltpu

HEAD_DIM = 64
SB_HEADS = 8
NSA_HEADS = 8
NSA_KV_GROUPS = 2
NSA_REP = NSA_HEADS // NSA_KV_GROUPS
NSA_CMP_LEN = 32
NSA_CMP_STRIDE = 16
NSA_SLC_BLOCK = 64
NSA_SLC_TOPK = 16
NSA_WINDOW = 512
SB_W = SB_HEADS * HEAD_DIM
NSA_Q_W = NSA_HEADS * HEAD_DIM
N_EXPERTS = 8
MOE_ROW_BLOCK = 512
DEPTH = 2
LN_EPS = 1e-5
DN_ALPHA = (2 * DEPTH) ** 0.25
NEG = -1e30
SEL_FORCE = 1e30
QK_SCALE = HEAD_DIM ** -0.5
LOG2E = 1.4426950408889634

LANES = 128
VMEM_LIMIT = 56 * 1024 * 1024

C_SB = 0
C_NQ = 3 * SB_W
C_CMP = C_NQ + NSA_Q_W
C_SW = C_CMP + 4 * HEAD_DIM
C_GATE = C_SW + 4 * 2 * HEAD_DIM
C_END = C_GATE + 2 * LANES

_F32 = jnp.float32
_BF16 = jnp.bfloat16


def _cparams(sem, vmem=VMEM_LIMIT):
    return pltpu.CompilerParams(dimension_semantics=sem, vmem_limit_bytes=vmem)


def _dot(a, b):
    return jnp.dot(a, b, preferred_element_type=_F32)


def _dot_t(a, b):
    return lax.dot_general(a, b, (((1,), (1,)), ((), ())), preferred_element_type=_F32)


def _split_bf16(x):
    hi = x.astype(_BF16)
    lo = (x - hi.astype(_F32)).astype(_BF16)
    return hi, lo


def _silu(x):
    return x / (1.0 + jnp.exp(-x))


def _layer_norm(r, g, b):
    mu = jnp.mean(r, axis=-1, keepdims=True)
    d = r - mu
    var = jnp.mean(d * d, axis=-1, keepdims=True)
    return d * lax.rsqrt(var + LN_EPS) * g + b


def _ada_kernel(c_ref, w_ref, b_ref, o_ref):
    cond = _silu(c_ref[...])
    o_ref[0] = jnp.dot(cond, w_ref[0], preferred_element_type=_F32,
                       precision=lax.Precision.HIGHEST) + b_ref[0]


def _ada_mod(c, ada_w, ada_b):
    depth, d, n = ada_w.shape
    bsz = c.shape[0]
    tn = n // 6
    return pl.pallas_call(
        _ada_kernel,
        grid=(depth, n // tn),
        in_specs=[pl.BlockSpec((bsz, d), lambda i, j: (0, 0)),
                  pl.BlockSpec((1, d, tn), lambda i, j: (i, 0, j)),
                  pl.BlockSpec((1, 1, tn), lambda i, j: (i, 0, j))],
        out_specs=pl.BlockSpec((1, bsz, tn), lambda i, j: (i, 0, j)),
        out_shape=jax.ShapeDtypeStruct((depth, bsz, n), _F32),
        compiler_params=_cparams(("parallel", "parallel")),
        name="ada_mod",
    )(c, ada_w, ada_b.reshape(depth, 1, n))


def _inproj_kernel(x_ref, sc_ref, sh_ref, w_ref, sb_ref, nq_ref, cmp_ref, sw_ref, gate_ref):
    h = (x_ref[0] * (1.0 + sc_ref[0]) + sh_ref[0]).astype(_BF16)
    sb_ref[0] = _dot(h, w_ref[:, C_SB:C_NQ]).astype(_BF16)
    nq_ref[0] = _dot(h, w_ref[:, C_NQ:C_CMP]).astype(_BF16)
    kvc = _dot(h, w_ref[:, C_CMP:C_SW])
    for j in range(4):
        cmp_ref[0, j] = kvc[:, j * HEAD_DIM:(j + 1) * HEAD_DIM]
    lane = lax.broadcasted_iota(jnp.int32, (1, LANES), 1)
    first = lane < HEAD_DIM
    for kind in range(4):
        a = _dot(h, w_ref[:, C_SW + kind * LANES:C_SW + (kind + 1) * LANES])
        r = pltpu.roll(a, HEAD_DIM, 1)
        g0 = jnp.where(first, a, r).astype(_BF16)
        g1 = jnp.where(first, r, a).astype(_BF16)
        sw_ref[0, :, (2 * kind) * 256:(2 * kind + 1) * 256] = jnp.concatenate([g0, g0], axis=1)
        sw_ref[0, :, (2 * kind + 1) * 256:(2 * kind + 2) * 256] = jnp.concatenate([g1, g1], axis=1)
    gate_ref[0] = _dot(h, w_ref[:, C_GATE:C_END])


def _inproj(x, sc, sh, w):
    bsz, s, d = x.shape
    tm = 512
    row = lambda b, i: (b, i, 0)
    return pl.pallas_call(
        _inproj_kernel,
        grid=(bsz, s // tm),
        in_specs=[pl.BlockSpec((1, tm, d), row),
                  pl.BlockSpec((1, 1, d), lambda b, i: (b, 0, 0)),
                  pl.BlockSpec((1, 1, d), lambda b, i: (b, 0, 0)),
                  pl.BlockSpec((d, C_END), lambda b, i: (0, 0))],
        out_specs=[pl.BlockSpec((1, tm, 3 * SB_W), row),
                   pl.BlockSpec((1, tm, NSA_Q_W), row),
                   pl.BlockSpec((1, 4, tm, HEAD_DIM), lambda b, i: (b, 0, i, 0)),
                   pl.BlockSpec((1, tm, 2048), row),
                   pl.BlockSpec((1, tm, 2 * LANES), row)],
        out_shape=[jax.ShapeDtypeStruct((bsz, s, 3 * SB_W), _BF16),
                   jax.ShapeDtypeStruct((bsz, s, NSA_Q_W), _BF16),
                   jax.ShapeDtypeStruct((bsz, 4, s, HEAD_DIM), _F32),
                   jax.ShapeDtypeStruct((bsz, s, 2048), _BF16),
                   jax.ShapeDtypeStruct((bsz, s, 2 * LANES), _F32)],
        compiler_params=_cparams(("parallel", "parallel")),
        name="l0_inproj",
    )(x, sc, sh, w)


SB_TQ = 512
SB_TK = 128
SB_UNROLL = 4


def _sb_kernel(q_ref, k_ref, v_ref, tri_ref, o_ref, acc_ref, run_ref):
    tq, tk = SB_TQ, SB_TK
    qi = pl.program_id(2)
    q = q_ref[0]
    lane = lax.broadcasted_iota(jnp.int32, (1, LANES), 1)
    first = lane < HEAD_DIM
    zero = jnp.zeros_like(q)
    q2 = jnp.concatenate([jnp.where(first, q, zero), jnp.where(first, zero, q)], axis=0)
    tri = tri_ref[...]
    acc_ref[...] = jnp.zeros_like(acc_ref)
    run_ref[...] = jnp.zeros_like(run_ref)

    def scores(qrows, key_off):
        return _dot_t(qrows, k_ref[0, pl.ds(key_off, tk), :])

    def suffix(z, causal):
        nlf = jnp.maximum(z, 0.0) + jnp.log(1.0 + jnp.exp2(jnp.abs(z) * -LOG2E))
        if causal is not None:
            nlf = jnp.where(causal, nlf, 0.0)
        hi_f = lax.bitcast_convert_type(
            lax.bitcast_convert_type(nlf, jnp.uint32) & jnp.uint32(0xFFFF0000), _F32)
        hi = hi_f.astype(_BF16)
        lo = (nlf - hi_f).astype(_BF16)
        return _dot(jnp.concatenate([hi, lo], axis=1), tri)

    def weigh(z, ext, run, causal, key_off):
        w = jnp.exp(z + ext[:, :tk] + run)
        if causal is not None:
            w = jnp.where(causal, w, 0.0)
        return _dot(w.astype(_BF16), v_ref[0, pl.ds(key_off, tk), :]), run + ext[:, tk:]

    n_diag = tq // tk
    diag = []
    for kd in reversed(range(n_diag)):
        r0 = kd * tk
        key_off = pl.multiple_of(qi * tq + r0, tk)
        row = r0 + lax.broadcasted_iota(jnp.int32, (tq - r0, tk), 0)
        col = r0 + lax.broadcasted_iota(jnp.int32, (tq - r0, tk), 1)
        causal = jnp.concatenate([col < row] * 2, axis=0)
        z = scores(jnp.concatenate([q2[r0:tq], q2[tq + r0:2 * tq]], axis=0), key_off)
        diag.append((r0, key_off, causal, z))
    diag = [(r0, key_off, causal, z, suffix(z, causal)) for r0, key_off, causal, z in diag]
    for r0, key_off, causal, z, ext in diag:
        n = tq - r0
        run = jnp.concatenate([run_ref[r0:tq, :], run_ref[tq + r0:2 * tq, :]], axis=0)
        pv, run = weigh(z, ext, run, causal, key_off)
        for h in range(2):
            rows = slice(h * tq + r0, (h + 1) * tq)
            acc_ref[rows, :] += pv[h * n:(h + 1) * n]
            run_ref[rows, :] = run[h * n:(h + 1) * n]

    def below(jj, carry):
        offs = [pl.multiple_of((qi * n_diag - 1 - SB_UNROLL * jj - u) * tk, tk) for u in range(SB_UNROLL)]
        zs = [scores(q2, off) for off in offs]
        exts = [suffix(z, None) for z in zs]
        run = run_ref[...]
        acc = acc_ref[...]
        for z, ext, off in zip(zs, exts, offs):
            pv, run = weigh(z, ext, run, None, off)
            acc = acc + pv
        acc_ref[...] = acc
        run_ref[...] = run
        return carry
    lax.fori_loop(0, qi * (n_diag // SB_UNROLL), below, 0)
    o_ref[0] = jnp.where(first, acc_ref[0:tq, :], acc_ref[tq:2 * tq, :]).astype(o_ref.dtype)


def _sb_attention(sb):
    bsz, s, _ = sb.shape
    tq, tk = SB_TQ, SB_TK
    npair = SB_W // LANES
    suffix = (np.arange(tk)[:, None] >= np.arange(tk)[None, :]).astype(np.float32)
    tri = -np.tile(np.concatenate([suffix, np.ones((tk, tk), np.float32)], axis=1), (2, 1))
    return pl.pallas_call(
        _sb_kernel,
        grid=(bsz, npair, s // tq),
        in_specs=[pl.BlockSpec((1, tq, LANES), lambda b, p, i: (b, i, p)),
                  pl.BlockSpec((1, s, LANES), lambda b, p, i: (b, 0, npair + p)),
                  pl.BlockSpec((1, s, LANES), lambda b, p, i: (b, 0, 2 * npair + p)),
                  pl.BlockSpec((2 * tk, 2 * tk), lambda b, p, i: (0, 0))],
        out_specs=pl.BlockSpec((1, tq, LANES), lambda b, p, i: (b, i, p)),
        out_shape=jax.ShapeDtypeStruct((bsz, s, SB_W), _BF16),
        scratch_shapes=[pltpu.VMEM((2 * tq, LANES), _F32), pltpu.VMEM((2 * tq, LANES), _F32)],
        compiler_params=_cparams(("parallel", "parallel", "arbitrary")),
        name="sb_attention",
    )(sb, sb, sb, jnp.asarray(tri, _BF16))


def _compress_kernel(x_ref, pos_ref, w1_ref, w2_ref, o_ref):
    half = w1_ref.shape[1] // 2
    ch = x_ref[0, 0]
    top = _dot((ch + pos_ref[0, 0:1, :]).astype(_BF16), w1_ref[0, :half, :])
    bot = _dot((ch + pos_ref[0, 1:2, :]).astype(_BF16), w1_ref[0, half:, :])
    n = ch.shape[0]
    pre = top + pltpu.roll(bot, n - 1, 0)
    o_ref[0, 0] = _dot(_silu(pre).astype(_BF16), w2_ref[0]).astype(o_ref.dtype)


def _compress(kvc, pos, w1, w2rep):
    bsz, _, s, dh = kvc.shape
    n_chunk = s // NSA_CMP_STRIDE
    cw = NSA_CMP_STRIDE * dh
    x = kvc.reshape(bsz, 4, n_chunk, cw)
    hid = w1.shape[-1]
    return pl.pallas_call(
        _compress_kernel,
        grid=(bsz, 4),
        in_specs=[pl.BlockSpec((1, 1, n_chunk, cw), lambda b, j: (b, j, 0, 0)),
                  pl.BlockSpec((1, 2, cw), lambda b, j: (j // 2, 0, 0)),
                  pl.BlockSpec((1, 2 * cw, hid), lambda b, j: (j // 2, 0, 0)),
                  pl.BlockSpec((1, hid, 4 * dh), lambda b, j: (j // 2, 0, 0))],
        out_specs=pl.BlockSpec((1, 1, n_chunk, 4 * dh), lambda b, j: (b, j, 0, 0)),
        out_shape=jax.ShapeDtypeStruct((bsz, 4, n_chunk, 4 * dh), _BF16),
        compiler_params=_cparams(("parallel", "parallel")),
        name="nsa_compress",
    )(x, pos, w1, w2rep)


NSA_TQ = 128
NSA_TK = 512
WIN_SPAN = NSA_WINDOW + NSA_TQ


def _nsa_kernel(qa_ref, qb_ref, kc_ref, vc_ref, ks_ref, vs_ref, kw_ref, vw_ref, ga_ref, gb_ref, ov_ref, ex_ref,
                o_ref):
    tq, tk = NSA_TQ, NSA_TK
    s_len = ks_ref.shape[1]
    n_tiles = s_len // tq
    n_cmp = kc_ref.shape[2]
    n_slc = s_len // NSA_SLC_BLOCK
    gw = NSA_REP * HEAD_DIM
    i = pl.program_id(2)
    head_of_lane = lax.broadcasted_iota(jnp.int32, (1, gw), 1) // HEAD_DIM

    def unstack(o):
        out = jnp.zeros((tq, gw), _F32)
        for r in range(NSA_REP):
            out = jnp.where(head_of_lane == r, o[r * tq:(r + 1) * tq], out)
        return out

    def tile4(x):
        return jnp.concatenate([x] * NSA_REP, axis=0)

    class Tile:
        pass
    tiles = []
    for q_ref, g_ref, t0, n_blk in ((qa_ref, ga_ref, i * tq, s_len // (2 * tk)),
                                    (qb_ref, gb_ref, (n_tiles - 1 - i) * tq, s_len // tk)):
        t = Tile()
        q = q_ref[0]
        zero = jnp.zeros_like(q)
        t.qs = jnp.concatenate([jnp.where(head_of_lane == r, q, zero) for r in range(NSA_REP)], axis=0)
        t.t0, t.n_blk, t.g_ref = t0, n_blk, g_ref
        t.t_row = t0 + lax.broadcasted_iota(jnp.int32, (tq, 1), 0)
        t.start = pl.multiple_of(jnp.maximum(t0 - NSA_WINDOW, 0), tq)
        tiles.append(t)

    for t in tiles:
        t.s_c = _dot_t(t.qs, kc_ref[0, 0])
        t.s_w = _dot_t(t.qs, kw_ref[0, pl.ds(t.start, WIN_SPAN), :])
        t.s_s = [_dot_t(t.qs, ks_ref[0, j * tk:(j + 1) * tk, :]) for j in range(t.n_blk)]

    cmp_last = lax.broadcasted_iota(jnp.int32, (1, n_cmp), 1) * NSA_CMP_STRIDE + (NSA_CMP_LEN - 1)
    for t in tiles:
        mask_c = tile4(cmp_last <= t.t_row)
        sm = jnp.where(mask_c, t.s_c, NEG)
        e_c = jnp.where(mask_c, jnp.exp2(sm - jnp.max(sm, axis=1, keepdims=True)), 0.0)
        l_c = jnp.sum(e_c, axis=1, keepdims=True)
        t.p_c = e_c / jnp.where(l_c == 0.0, 1.0, l_c)
    for t in tiles:
        t.o_cmp = unstack(_dot(t.p_c.astype(_BF16), vc_ref[0, 0]))
        p_sum = t.p_c[0:tq] + t.p_c[tq:2 * tq] + t.p_c[2 * tq:3 * tq] + t.p_c[3 * tq:4 * tq]
        hi, lo = _split_bf16(p_sum)
        t.imp = _dot_t(ov_ref[...], jnp.concatenate([hi, lo], axis=1))[:n_slc]

    for t in tiles:
        diff = t.t_row - (t.start + lax.broadcasted_iota(jnp.int32, (1, WIN_SPAN), 1))
        bias_w = jnp.where((diff >= 0) & (diff < NSA_WINDOW), 0.0, NEG)
        sm = t.s_w + tile4(bias_w)
        e_w = jnp.exp2(sm - jnp.max(sm, axis=1, keepdims=True))
        t.l_w = jnp.sum(e_w, axis=1, keepdims=True)
        t.e_w = e_w.astype(_BF16)
    for t in tiles:
        t.o_win = unstack(_dot(t.e_w, vw_ref[0, pl.ds(t.start, WIN_SPAN), :]) / t.l_w)

    blk = lax.broadcasted_iota(jnp.int32, (n_slc, 1), 0)
    for t in tiles:
        cur = (t.t0 + lax.broadcasted_iota(jnp.int32, (1, tq), 1)) // NSA_SLC_BLOCK
        valid = blk <= cur
        forced = (blk == 0) | (blk == cur) | (blk == cur - 1)
        val = jnp.where(valid, jnp.where(forced, SEL_FORCE, t.imp), NEG)
        rank = jnp.zeros((n_slc, tq), _F32)
        for b in range(n_slc):
            vb = val[b:b + 1, :]
            ahead = (vb > val) | ((vb == val) & (blk > b))
            rank = rank + jnp.where(ahead, 1.0, 0.0)
        sel_t = jnp.where(rank < float(min(NSA_SLC_TOPK, n_slc)), 1.0, 0.0)
        sel_t = jnp.concatenate([sel_t, jnp.zeros((LANES - n_slc, tq), _F32)], axis=0)
        t.sel = sel_t.T.astype(_BF16)

    for t in tiles:
        masked = []
        for j in range(t.n_blk):
            chosen = _dot(t.sel, ex_ref[:, j * tk:(j + 1) * tk])
            kpos = j * tk + lax.broadcasted_iota(jnp.int32, (1, tk), 1)
            bias = jnp.where((chosen > 0.5) & (kpos <= t.t_row), 0.0, NEG)
            masked.append(t.s_s[j] + tile4(bias))
        m = masked[0].max(axis=1, keepdims=True)
        for sb in masked[1:]:
            m = jnp.maximum(m, sb.max(axis=1, keepdims=True))
        e_s = [jnp.exp2(sb - m) for sb in masked]
        t.l_s = sum(e.sum(axis=1, keepdims=True) for e in e_s)
        t.e_s = [e.astype(_BF16) for e in e_s]
    for t in tiles:
        acc = _dot(t.e_s[0], vs_ref[0, 0:tk, :])
        for j in range(1, t.n_blk):
            acc = acc + _dot(t.e_s[j], vs_ref[0, j * tk:(j + 1) * tk, :])
        t.o_slc = unstack(acc / t.l_s)

    for half, t in enumerate(tiles):
        g = 1.0 / (1.0 + jnp.exp(-t.g_ref[0]))
        out = jnp.zeros((tq, gw), _F32)
        for branch, o_b in enumerate((t.o_cmp, t.o_slc, t.o_win)):
            gm = jnp.zeros((tq, gw), _F32)
            for r in range(NSA_REP):
                c = 3 * r + branch
                gm = jnp.where(head_of_lane == r, g[:, c:c + 1], gm)
            out = out + gm * o_b
        o_ref[0, half, 0] = out.astype(o_ref.dtype)


def _nsa_attention(nq, cmp_kv, sw, gates):
    bsz, s, _ = nq.shape
    tq = NSA_TQ
    n_cmp = cmp_kv.shape[2]
    n_slc = s // NSA_SLC_BLOCK
    gw = 4 * HEAD_DIM
    cmp_start = np.arange(n_cmp) * NSA_CMP_STRIDE
    slc_start = np.arange(LANES) * NSA_SLC_BLOCK
    ov = ((cmp_start[:, None] < slc_start[None, :] + NSA_SLC_BLOCK)
          & (cmp_start[:, None] + NSA_CMP_LEN > slc_start[None, :])
          & (np.arange(LANES)[None, :] < n_slc)).astype(np.float32)
    ov[n_cmp - 1:] = 0.0
    ov2 = np.concatenate([ov.T, ov.T], axis=1)
    ex = (np.arange(LANES)[:, None] == (np.arange(s)[None, :] // NSA_SLC_BLOCK)).astype(np.float32)
    kv_spec = lambda kind: pl.BlockSpec((1, s, gw), lambda b, g, i, kind=kind: (b, 0, 2 * kind + g))
    n_tiles = s // tq
    n_pair = n_tiles // 2
    out = pl.pallas_call(
        _nsa_kernel,
        grid=(bsz, NSA_KV_GROUPS, n_pair),
        in_specs=[pl.BlockSpec((1, tq, gw), lambda b, g, i: (b, i, g)),
                  pl.BlockSpec((1, tq, gw), lambda b, g, i: (b, n_tiles - 1 - i, g)),
                  pl.BlockSpec((1, 1, n_cmp, gw), lambda b, g, i: (b, g, 0, 0)),
                  pl.BlockSpec((1, 1, n_cmp, gw), lambda b, g, i: (b, 2 + g, 0, 0)),
                  kv_spec(0), kv_spec(1), kv_spec(2), kv_spec(3),
                  pl.BlockSpec((1, tq, LANES), lambda b, g, i: (b, i, g)),
                  pl.BlockSpec((1, tq, LANES), lambda b, g, i: (b, n_tiles - 1 - i, g)),
                  pl.BlockSpec((LANES, 2 * n_cmp), lambda b, g, i: (0, 0)),
                  pl.BlockSpec((LANES, s), lambda b, g, i: (0, 0))],
        out_specs=pl.BlockSpec((1, 2, 1, tq, gw), lambda b, g, i: (b, 0, i, 0, g)),
        out_shape=jax.ShapeDtypeStruct((bsz, 2, n_pair, tq, NSA_Q_W), _BF16),
        compiler_params=_cparams(("parallel", "parallel", "arbitrary")),
        name="nsa_attention",
    )(nq, nq, cmp_kv, cmp_kv, sw, sw, sw, sw, gates, gates, jnp.asarray(ov2, _BF16), jnp.asarray(ex, _BF16))
    return jnp.concatenate([out[:, 0], out[:, 1, ::-1]], axis=1).reshape(bsz, s, NSA_Q_W)


def _outproj_kernel(a_ref, b_ref, wa_ref, wb_ref, x_ref, g_ref, lg_ref, lb_ref, o_ref):
    y = _dot(a_ref[0], wa_ref[...]) + _dot(b_ref[0], wb_ref[...])
    r = DN_ALPHA * x_ref[0] + (1.0 + g_ref[0]) * y
    o_ref[0] = _layer_norm(r, lg_ref[...], lb_ref[...])


def _outproj_ln(o_sb, o_nsa, w_out, x, gate, ln_g, ln_b):
    bsz, s, d = x.shape
    tm = 512
    row = lambda b, i: (b, i, 0)
    vec = pl.BlockSpec((1, d), lambda b, i: (0, 0))
    return pl.pallas_call(
        _outproj_kernel,
        grid=(bsz, s // tm),
        in_specs=[pl.BlockSpec((1, tm, SB_W), row), pl.BlockSpec((1, tm, NSA_Q_W), row),
                  pl.BlockSpec((SB_W, d), lambda b, i: (0, 0)),
                  pl.BlockSpec((NSA_Q_W, d), lambda b, i: (1, 0)),
                  pl.BlockSpec((1, tm, d), row),
                  pl.BlockSpec((1, 1, d), lambda b, i: (b, 0, 0)), vec, vec],
        out_specs=pl.BlockSpec((1, tm, d), row),
        out_shape=jax.ShapeDtypeStruct((bsz, s, d), _F32),
        compiler_params=_cparams(("parallel", "parallel")),
        name="l0_outproj_ln",
    )(o_sb, o_nsa, w_out, w_out, x, gate, ln_g.reshape(1, d), ln_b.reshape(1, d))


FFN_TF = 256


def _swiglu(xb, w1, w3, w2, between=None):
    ff = w1.shape[1]
    acc = jnp.zeros((xb.shape[0], w2.shape[1]), _F32)
    for c in range(ff // FFN_TF):
        cs = slice(c * FFN_TF, (c + 1) * FFN_TF)
        a = _silu(_dot(xb, w1[:, cs])) * _dot(xb, w3[:, cs])
        acc = acc + _dot(a.astype(_BF16), w2[cs, :])
        if between is not None:
            between(c)
    return acc


def _ffn_kernel(x_ref, sc_ref, sh_ref, g_ref, w1_ref, w3_ref, w2_ref, lg_ref, lb_ref, o_ref):
    x = x_ref[0]
    h = (x * (1.0 + sc_ref[0]) + sh_ref[0]).astype(_BF16)
    r = DN_ALPHA * x + (1.0 + g_ref[0]) * _swiglu(h, w1_ref, w3_ref, w2_ref)
    o_ref[0] = _layer_norm(r, lg_ref[...], lb_ref[...])


def _ffn_ln(x, sc, sh, gate, w1, w3, w2, ln_g, ln_b):
    bsz, s, d = x.shape
    ff = w1.shape[1]
    tm = 512
    row = lambda b, i: (b, i, 0)
    mod = pl.BlockSpec((1, 1, d), lambda b, i: (b, 0, 0))
    vec = pl.BlockSpec((1, d), lambda b, i: (0, 0))
    return pl.pallas_call(
        _ffn_kernel,
        grid=(bsz, s // tm),
        in_specs=[pl.BlockSpec((1, tm, d), row), mod, mod, mod,
                  pl.BlockSpec((d, ff), lambda b, i: (0, 0)),
                  pl.BlockSpec((d, ff), lambda b, i: (0, 0)),
                  pl.BlockSpec((ff, d), lambda b, i: (0, 0)), vec, vec],
        out_specs=pl.BlockSpec((1, tm, d), row),
        out_shape=jax.ShapeDtypeStruct((bsz, s, d), _F32),
        compiler_params=_cparams(("parallel", "parallel")),
        name="l0_ffn_ln",
    )(x, sc, sh, gate, w1, w3, w2, ln_g.reshape(1, d), ln_b.reshape(1, d))


def _conv_kernel(x_ref, sc_ref, sh_ref, g_ref, win_ref, taps_ref, wout_ref, lg_ref, lb_ref, o_ref, tail_ref):
    ts, d = x_ref.shape[1], x_ref.shape[2]

    @pl.when(pl.program_id(1) == 0)
    def _():
        tail_ref[...] = jnp.zeros_like(tail_ref)

    x = x_ref[0]
    h = (x * (1.0 + sc_ref[0]) + sh_ref[0]).astype(_BF16)
    z = _dot(h, win_ref[:, d:2 * d]) * _dot(h, win_ref[:, 2 * d:3 * d])
    row = lax.broadcasted_iota(jnp.int32, (ts, 1), 0)
    prev1 = tail_ref[7:8, :]
    prev2 = tail_ref[6:7, :]
    z1 = jnp.where(row == 0, prev1, pltpu.roll(z, 1, 0))
    z2 = jnp.where(row == 0, prev2, jnp.where(row == 1, prev1, pltpu.roll(z, 2, 0)))
    zc = taps_ref[0:1, :] * z2 + taps_ref[1:2, :] * z1 + taps_ref[2:3, :] * z
    tail_ref[...] = z[ts - 8:ts, :]
    y = _dot((_dot(h, win_ref[:, 0:d]) * zc).astype(_BF16), wout_ref[...])
    r = DN_ALPHA * x + (1.0 + g_ref[0]) * y
    o_ref[0] = _layer_norm(r, lg_ref[...], lb_ref[...])


def _conv_ln(x, sc, sh, gate, w_in, taps, w_out, ln_g, ln_b):
    bsz, s, d = x.shape
    ts = 512
    row = lambda b, i: (b, i, 0)
    mod = pl.BlockSpec((1, 1, d), lambda b, i: (b, 0, 0))
    vec = pl.BlockSpec((1, d), lambda b, i: (0, 0))
    return pl.pallas_call(
        _conv_kernel,
        grid=(bsz, s // ts),
        in_specs=[pl.BlockSpec((1, ts, d), row), mod, mod, mod,
                  pl.BlockSpec((d, 3 * d), lambda b, i: (0, 0)),
                  pl.BlockSpec((taps.shape[0], d), lambda b, i: (0, 0)),
                  pl.BlockSpec((d, d), lambda b, i: (0, 0)), vec, vec],
        out_specs=pl.BlockSpec((1, ts, d), row),
        out_shape=jax.ShapeDtypeStruct((bsz, s, d), _F32),
        scratch_shapes=[pltpu.VMEM((8, d), _F32)],
        compiler_params=_cparams(("arbitrary", "arbitrary")),
        name="l1_conv_ln",
    )(x, sc, sh, gate, w_in, taps, w_out, ln_g.reshape(1, d), ln_b.reshape(1, d))


ROUTE_TM = 512


def _router_kernel(x_ref, sc_ref, sh_ref, rw_ref, rb_ref, tri_ref, h_ref, route_ref, route_t_ref, cnt_ref, run_ref):
    tm = ROUTE_TM

    @pl.when((pl.program_id(0) == 0) & (pl.program_id(1) == 0))
    def _():
        run_ref[...] = jnp.zeros_like(run_ref)

    h = x_ref[0] * (1.0 + sc_ref[0]) + sh_ref[0]
    h_ref[0] = h
    lane = lax.broadcasted_iota(jnp.int32, (1, LANES), 1)
    h_hi, h_lo = _split_bf16(h)
    two = _dot(h_hi, rw_ref[...])
    logits = two[:, :LANES] + two[:, LANES:] + _dot(h_lo, rw_ref[:, :LANES]) + rb_ref[...]
    logits = jnp.where(lane < N_EXPERTS, logits, -3e38)
    m1 = jnp.max(logits, axis=1, keepdims=True)
    e1 = jnp.min(jnp.where(logits == m1, lane, LANES), axis=1, keepdims=True)
    rest = jnp.where(lane == e1, -3e38, logits)
    m2 = jnp.max(rest, axis=1, keepdims=True)
    e2 = jnp.min(jnp.where(rest == m2, lane, LANES), axis=1, keepdims=True)
    ex = jnp.exp(m2 - m1)
    g1 = 1.0 / (1.0 + ex)
    g2 = ex / (1.0 + ex)
    oh1 = jnp.where(lane == e1, 1.0, 0.0)
    oh2 = jnp.where(lane == e2, 1.0, 0.0)
    oh = oh1 + oh2
    before = run_ref[...] + _dot(tri_ref[...], oh.astype(_BF16))
    p1 = jnp.sum(oh1 * before, axis=1, keepdims=True)
    p2 = jnp.sum(oh2 * before, axis=1, keepdims=True)
    run_ref[...] += jnp.sum(oh, axis=0, keepdims=True)
    cnt_ref[...] = run_ref[...]
    vals = (e1.astype(_F32), e2.astype(_F32), p1, p2, g1, g2)
    out = jnp.zeros((tm, LANES), _F32)
    for k, v in enumerate(vals):
        out = jnp.where(lane == k, v, out)
    route_ref[0] = out
    route_t_ref[0, 0] = out.T[:8, :]


def _router(x, sc, sh, rw, rb):
    bsz, s, d = x.shape
    tm = ROUTE_TM
    row = lambda b, i: (b, i, 0)
    mod = pl.BlockSpec((1, 1, d), lambda b, i: (b, 0, 0))
    tri = (np.arange(tm)[None, :] < np.arange(tm)[:, None]).astype(np.float32)
    return pl.pallas_call(
        _router_kernel,
        grid=(bsz, s // tm),
        in_specs=[pl.BlockSpec((1, tm, d), row), mod, mod,
                  pl.BlockSpec((d, 2 * LANES), lambda b, i: (0, 0)),
                  pl.BlockSpec((1, LANES), lambda b, i: (0, 0)),
                  pl.BlockSpec((tm, tm), lambda b, i: (0, 0))],
        out_specs=[pl.BlockSpec((1, tm, d), row),
                   pl.BlockSpec((1, tm, LANES), row),
                   pl.BlockSpec((1, 1, 8, tm), lambda b, i: (b, i, 0, 0)),
                   pl.BlockSpec((1, LANES), lambda b, i: (0, 0))],
        out_shape=[jax.ShapeDtypeStruct((bsz, s, d), _F32),
                   jax.ShapeDtypeStruct((bsz, s, LANES), _F32),
                   jax.ShapeDtypeStruct((bsz, s // tm, 8, tm), _F32),
                   jax.ShapeDtypeStruct((1, LANES), _F32)],
        scratch_shapes=[pltpu.VMEM((1, LANES), _F32)],
        compiler_params=_cparams(("arbitrary", "arbitrary")),
        name="l1_router",
    )(x, sc, sh, rw, rb, jnp.asarray(tri, _BF16))


def _row_tokens_kernel(d1_ref, d2_ref, lo_ref, hi_ref, o_ref):
    def clear_range(k, carry):
        def clear(r, c):
            o_ref[r] = 0
            return c
        return lax.fori_loop(lo_ref[k], hi_ref[k], clear, carry)
    lax.fori_loop(0, lo_ref.shape[0], clear_range, 0)

    def place(a, carry):
        o_ref[d1_ref[a]] = a
        o_ref[d2_ref[a]] = a
        return carry
    lax.fori_loop(0, d1_ref.shape[0], place, 0, unroll=8)


def _row_tokens(dest1, dest2, pad_lo, pad_hi, rows):
    smem = pl.BlockSpec(memory_space=pltpu.SMEM)
    return pl.pallas_call(
        _row_tokens_kernel,
        in_specs=[smem, smem, smem, smem],
        out_specs=smem,
        out_shape=jax.ShapeDtypeStruct((rows,), jnp.int32),
        name="l1_row_tokens",
    )(dest1, dest2, pad_lo, pad_hi)


GATHER_UNROLL = 8


def _row_gather(src_hbm, idx_ref, base, dst_ref, sem, n):
    def issue(r, carry):
        tok = idx_ref[base + r]
        pltpu.make_async_copy(src_hbm.at[pl.ds(tok, 1)], dst_ref.at[pl.ds(r, 1)], sem).start()
        return carry
    lax.fori_loop(0, n, issue, 0, unroll=GATHER_UNROLL)


def _row_gather_wait(src_hbm, dst_ref, sem, n):
    pltpu.make_async_copy(src_hbm.at[pl.ds(0, n)], dst_ref, sem).wait()


def _expert_kernel(blk_e_ref, n_used_ref, row_tok_ref, h_hbm, w1_ref, w3_ref, w2_ref, y_ref, xbuf, sems):
    i = pl.program_id(0)
    n_used = n_used_ref[0]
    rb = MOE_ROW_BLOCK
    slot = lax.rem(i, 2)

    @pl.when((i == 0) & (n_used > 0))
    def _():
        _row_gather(h_hbm, row_tok_ref, 0, xbuf.at[0], sems.at[0], rb)

    @pl.when(i < n_used)
    def _():
        _row_gather_wait(h_hbm, xbuf.at[slot], sems.at[slot], rb)

    n_chunk = w1_ref.shape[2] // FFN_TF
    per_chunk = -(-rb // n_chunk)

    def issue_next(c):
        for r in range(c * per_chunk, min((c + 1) * per_chunk, rb)):
            tok = row_tok_ref[(i + 1) * rb + r]
            pltpu.make_async_copy(h_hbm.at[pl.ds(tok, 1)], xbuf.at[1 - slot, pl.ds(r, 1)], sems.at[1 - slot]).start()

    @pl.when(i + 1 < n_used)
    def _():
        y_ref[...] = _swiglu(xbuf[slot].astype(_BF16), w1_ref.at[0], w3_ref.at[0], w2_ref.at[0], issue_next)

    @pl.when(i + 1 == n_used)
    def _():
        y_ref[...] = _swiglu(xbuf[slot].astype(_BF16), w1_ref.at[0], w3_ref.at[0], w2_ref.at[0])

    @pl.when(i >= n_used)
    def _():
        y_ref[...] = jnp.zeros_like(y_ref)


def _experts(blk_e, n_used, row_tok, h_flat, w1, w3, w2):
    t, d = h_flat.shape
    n_blocks = blk_e.shape[0]
    rb = MOE_ROW_BLOCK
    ff = w1.shape[2]
    grid_spec = pltpu.PrefetchScalarGridSpec(
        num_scalar_prefetch=3,
        grid=(n_blocks,),
        in_specs=[pl.BlockSpec(memory_space=pl.ANY),
                  pl.BlockSpec((1, d, ff), lambda i, be, nu, rt: (be[i], 0, 0)),
                  pl.BlockSpec((1, d, ff), lambda i, be, nu, rt: (be[i], 0, 0)),
                  pl.BlockSpec((1, ff, d), lambda i, be, nu, rt: (be[i], 0, 0))],
        out_specs=pl.BlockSpec((rb, d), lambda i, be, nu, rt: (i, 0)),
        scratch_shapes=[pltpu.VMEM((2, rb, d), _F32), pltpu.SemaphoreType.DMA((2,))],
    )
    return pl.pallas_call(
        _expert_kernel,
        grid_spec=grid_spec,
        out_shape=jax.ShapeDtypeStruct((n_blocks * rb, d), _F32),
        compiler_params=_cparams(("arbitrary",)),
        name="l1_experts",
    )(blk_e, n_used, row_tok, h_flat, w1, w3, w2)


COMB_TM = 256


def _combine_kernel(d1_ref, d2_ref, y_hbm, x_ref, route_ref, g_ref, lg_ref, lb_ref, o_ref, y1buf, y2buf, sems):
    tm = COMB_TM
    i = pl.program_id(0) * pl.num_programs(1) + pl.program_id(1)
    n = pl.num_programs(0) * pl.num_programs(1)
    slot = lax.rem(i, 2)

    def gather(tile, s):
        for r in range(tm):
            pltpu.make_async_copy(y_hbm.at[pl.ds(d1_ref[tile * tm + r], 1)], y1buf.at[s, pl.ds(r, 1)],
                                  sems.at[0, s]).start()
            pltpu.make_async_copy(y_hbm.at[pl.ds(d2_ref[tile * tm + r], 1)], y2buf.at[s, pl.ds(r, 1)],
                                  sems.at[1, s]).start()

    @pl.when(i == 0)
    def _():
        gather(0, 0)

    @pl.when(i + 1 < n)
    def _():
        gather(i + 1, 1 - slot)

    _row_gather_wait(y_hbm, y1buf.at[slot], sems.at[0, slot], tm)
    _row_gather_wait(y_hbm, y2buf.at[slot], sems.at[1, slot], tm)
    route = route_ref[0]
    y = route[:, 4:5] * y1buf[slot] + route[:, 5:6] * y2buf[slot]
    r = DN_ALPHA * x_ref[0] + (1.0 + g_ref[0]) * y
    o_ref[0] = _layer_norm(r, lg_ref[...], lb_ref[...])


def _combine_ln(dest1, dest2, y, x, route, gate, ln_g, ln_b):
    bsz, s, d = x.shape
    tm = COMB_TM
    row = lambda b, i, d1, d2: (b, i, 0)
    vec = pl.BlockSpec((1, d), lambda b, i, d1, d2: (0, 0))
    grid_spec = pltpu.PrefetchScalarGridSpec(
        num_scalar_prefetch=2,
        grid=(bsz, s // tm),
        in_specs=[pl.BlockSpec(memory_space=pl.ANY),
                  pl.BlockSpec((1, tm, d), row),
                  pl.BlockSpec((1, tm, LANES), row),
                  pl.BlockSpec((1, 1, d), lambda b, i, d1, d2: (b, 0, 0)), vec, vec],
        out_specs=pl.BlockSpec((1, tm, d), row),
        scratch_shapes=[pltpu.VMEM((2, tm, d), _F32), pltpu.VMEM((2, tm, d), _F32),
                        pltpu.SemaphoreType.DMA((2, 2))],
    )
    return pl.pallas_call(
        _combine_kernel,
        grid_spec=grid_spec,
        out_shape=jax.ShapeDtypeStruct((bsz, s, d), _F32),
        compiler_params=_cparams(("arbitrary", "arbitrary")),
        name="l1_combine_ln",
    )(dest1, dest2, y, x, route, gate, ln_g.reshape(1, d), ln_b.reshape(1, d))


def _arrange_inproj_weight(w):
    d = w.shape[0]
    kv0 = 3 * SB_W + NSA_Q_W
    g0 = kv0 + 6 * NSA_KV_GROUPS * HEAD_DIM
    per_group = NSA_REP * 3
    pad = jnp.zeros((d, LANES - per_group), w.dtype)
    cols = [w[:, :SB_W] * QK_SCALE, w[:, SB_W:3 * SB_W], w[:, 3 * SB_W:kv0] * (QK_SCALE * LOG2E), w[:, kv0:g0]]
    for g in range(NSA_KV_GROUPS):
        cols += [w[:, g0 + g * per_group:g0 + (g + 1) * per_group], pad]
    return jnp.concatenate(cols, axis=1).astype(_BF16)


def _mods(mod, i, bsz, d):
    m = mod[i].reshape(bsz, 6, 1, d)
    return [m[:, k] for k in range(6)]


def kernel(x, c, ada_w, ada_b, ln_g, ln_b, mix_w_in, cmp_pos, cmp_w1, cmp_w2, mix_w_out, ffn_w1, ffn_w3, ffn_w2,
           conv_w_in, conv_taps, conv_w_out, router_w, router_b, exp_w1, exp_w3, exp_w2):
    bsz, s, d = x.shape
    t = bsz * s
    mod = _ada_mod(c, ada_w, ada_b)

    sh1, sc1, g1, sh2, sc2, g2 = _mods(mod, 0, bsz, d)
    sb, nq, kvc, sw, gates = _inproj(x, sc1, sh1, _arrange_inproj_weight(mix_w_in[0]))
    o_sb = _sb_attention(sb)
    pos = cmp_pos[0].reshape(2, 2, NSA_CMP_STRIDE * HEAD_DIM)
    w2rep = jnp.tile(cmp_w2[0], (1, 1, NSA_REP)).astype(_BF16)
    cmp_kv = _compress(kvc, pos, cmp_w1[0].astype(_BF16), w2rep)
    o_nsa = _nsa_attention(nq, cmp_kv, sw, gates)
    x = _outproj_ln(o_sb, o_nsa, mix_w_out[0].astype(_BF16), x, g1, ln_g[0, 0], ln_b[0, 0])
    x = _ffn_ln(x, sc2, sh2, g2, ffn_w1[0].astype(_BF16), ffn_w3[0].astype(_BF16), ffn_w2[0].astype(_BF16),
                ln_g[0, 1], ln_b[0, 1])

    sh1, sc1, g1, sh2, sc2, g2 = _mods(mod, 1, bsz, d)
    x = _conv_ln(x, sc1, sh1, g1, conv_w_in[0].astype(_BF16), conv_taps[0], conv_w_out[0].astype(_BF16),
                 ln_g[1, 0], ln_b[1, 0])
    rw = jnp.concatenate(_split_bf16(jnp.pad(router_w[0], ((0, 0), (0, LANES - N_EXPERTS)))), axis=1)
    rb = jnp.pad(router_b[0], (0, LANES - N_EXPERTS)).reshape(1, LANES)
    h2, route, route_t, counts = _router(x, sc2, sh2, rw, rb)

    field = lambda k: route_t[:, :, k, :].reshape(t).astype(jnp.int32)
    e1, e2, p1, p2 = field(0), field(1), field(2), field(3)
    cnt = counts[0, :N_EXPERTS].astype(jnp.int32)
    rb_rows = MOE_ROW_BLOCK
    padded = (cnt + rb_rows - 1) // rb_rows * rb_rows
    pends = jnp.cumsum(padded)
    pstarts = pends - padded
    dest1 = pstarts[e1] + p1
    dest2 = pstarts[e2] + p2
    n_blocks = -(-(t * 2) // rb_rows) + N_EXPERTS
    rows = n_blocks * rb_rows
    pad_lo = jnp.concatenate([pstarts + cnt, pends[-1:]]).astype(jnp.int32)
    pad_hi = jnp.concatenate([pends, jnp.full((1,), rows, jnp.int32)]).astype(jnp.int32)
    row_tok = _row_tokens(dest1, dest2, pad_lo, pad_hi, rows)
    blk_e = jnp.minimum(jnp.searchsorted(pends, jnp.arange(n_blocks, dtype=jnp.int32) * rb_rows, side='right'),
                        N_EXPERTS - 1).astype(jnp.int32)
    n_used = (pends[-1:] // rb_rows).astype(jnp.int32)

    y = _experts(blk_e, n_used, row_tok, h2.reshape(t, d), exp_w1[0].astype(_BF16), exp_w3[0].astype(_BF16),
                 exp_w2[0].astype(_BF16))
    return _combine_ln(dest1, dest2, y, x, route, g2, ln_g[1, 1], ln_b[1, 1])
```

```python
import functools

import numpy as np
import jax
import jax.numpy as jnp
from jax import lax
from jax.experimental import pallas as pl
from jax.experimental.pallas import tpu as pltpu

HEAD_DIM = 64
SB_HEADS = 8
NSA_HEADS = 8
NSA_KV_GROUPS = 2
NSA_REP = NSA_HEADS // NSA_KV_GROUPS
NSA_CMP_LEN = 32
NSA_CMP_STRIDE = 16
NSA_SLC_BLOCK = 64
NSA_SLC_TOPK = 16
NSA_WINDOW = 512
SB_W = SB_HEADS * HEAD_DIM
NSA_Q_W = NSA_HEADS * HEAD_DIM
N_EXPERTS = 8
MOE_ROW_BLOCK = 512
DEPTH = 2
LN_EPS = 1e-5
DN_ALPHA = (2 * DEPTH) ** 0.25
NEG = -1e30
SEL_FORCE = 1e30
QK_SCALE = HEAD_DIM ** -0.5
LOG2E = 1.4426950408889634

LANES = 128
VMEM_LIMIT = 56 * 1024 * 1024

C_SB = 0
C_NQ = 3 * SB_W
C_CMP = C_NQ + NSA_Q_W
C_SW = C_CMP + 4 * HEAD_DIM
C_GATE = C_SW + 4 * 2 * HEAD_DIM
C_END = C_GATE + 2 * LANES

_F32 = jnp.float32
_BF16 = jnp.bfloat16


def _cparams(sem, vmem=VMEM_LIMIT):
    return pltpu.CompilerParams(dimension_semantics=sem, vmem_limit_bytes=vmem)


def _dot(a, b):
    return jnp.dot(a, b, preferred_element_type=_F32)


def _dot_t(a, b):
    return lax.dot_general(a, b, (((1,), (1,)), ((), ())), preferred_element_type=_F32)


def _split_bf16(x):
    hi = x.astype(_BF16)
    lo = (x - hi.astype(_F32)).astype(_BF16)
    return hi, lo


def _silu(x):
    return x / (1.0 + jnp.exp(-x))


def _layer_norm(r, g, b):
    mu = jnp.mean(r, axis=-1, keepdims=True)
    d = r - mu
    var = jnp.mean(d * d, axis=-1, keepdims=True)
    return d * lax.rsqrt(var + LN_EPS) * g + b


def _ada_kernel(c_ref, w_ref, b_ref, o_ref):
    cond = _silu(c_ref[...])
    o_ref[0] = jnp.dot(cond, w_ref[0], preferred_element_type=_F32,
                       precision=lax.Precision.HIGHEST) + b_ref[0]


def _ada_mod(c, ada_w, ada_b):
    depth, d, n = ada_w.shape
    bsz = c.shape[0]
    tn = n // 6
    return pl.pallas_call(
        _ada_kernel,
        grid=(depth, n // tn),
        in_specs=[pl.BlockSpec((bsz, d), lambda i, j: (0, 0)),
                  pl.BlockSpec((1, d, tn), lambda i, j: (i, 0, j)),
                  pl.BlockSpec((1, 1, tn), lambda i, j: (i, 0, j))],
        out_specs=pl.BlockSpec((1, bsz, tn), lambda i, j: (i, 0, j)),
        out_shape=jax.ShapeDtypeStruct((depth, bsz, n), _F32),
        compiler_params=_cparams(("parallel", "parallel")),
        name="ada_mod",
    )(c, ada_w, ada_b.reshape(depth, 1, n))


def _inproj_kernel(x_ref, sc_ref, sh_ref, w_ref, sb_ref, nq_ref, cmp_ref, sw_ref, gate_ref):
    h = (x_ref[0] * (1.0 + sc_ref[0]) + sh_ref[0]).astype(_BF16)
    sb_ref[0] = _dot(h, w_ref[:, C_SB:C_NQ]).astype(_BF16)
    nq_ref[0] = _dot(h, w_ref[:, C_NQ:C_CMP]).astype(_BF16)
    kvc = _dot(h, w_ref[:, C_CMP:C_SW])
    for j in range(4):
        cmp_ref[0, j] = kvc[:, j * HEAD_DIM:(j + 1) * HEAD_DIM]
    lane = lax.broadcasted_iota(jnp.int32, (1, LANES), 1)
    first = lane < HEAD_DIM
    for kind in range(4):
        a = _dot(h, w_ref[:, C_SW + kind * LANES:C_SW + (kind + 1) * LANES])
        r = pltpu.roll(a, HEAD_DIM, 1)
        g0 = jnp.where(first, a, r).astype(_BF16)
        g1 = jnp.where(first, r, a).astype(_BF16)
        sw_ref[0, :, (2 * kind) * 256:(2 * kind + 1) * 256] = jnp.concatenate([g0, g0], axis=1)
        sw_ref[0, :, (2 * kind + 1) * 256:(2 * kind + 2) * 256] = jnp.concatenate([g1, g1], axis=1)
    gate_ref[0] = _dot(h, w_ref[:, C_GATE:C_END])


def _inproj(x, sc, sh, w):
    bsz, s, d = x.shape
    tm = 512
    row = lambda b, i: (b, i, 0)
    return pl.pallas_call(
        _inproj_kernel,
        grid=(bsz, s // tm),
        in_specs=[pl.BlockSpec((1, tm, d), row),
                  pl.BlockSpec((1, 1, d), lambda b, i: (b, 0, 0)),
                  pl.BlockSpec((1, 1, d), lambda b, i: (b, 0, 0)),
                  pl.BlockSpec((d, C_END), lambda b, i: (0, 0))],
        out_specs=[pl.BlockSpec((1, tm, 3 * SB_W), row),
                   pl.BlockSpec((1, tm, NSA_Q_W), row),
                   pl.BlockSpec((1, 4, tm, HEAD_DIM), lambda b, i: (b, 0, i, 0)),
                   pl.BlockSpec((1, tm, 2048), row),
                   pl.BlockSpec((1, tm, 2 * LANES), row)],
        out_shape=[jax.ShapeDtypeStruct((bsz, s, 3 * SB_W), _BF16),
                   jax.ShapeDtypeStruct((bsz, s, NSA_Q_W), _BF16),
                   jax.ShapeDtypeStruct((bsz, 4, s, HEAD_DIM), _F32),
                   jax.ShapeDtypeStruct((bsz, s, 2048), _BF16),
                   jax.ShapeDtypeStruct((bsz, s, 2 * LANES), _F32)],
        compiler_params=_cparams(("parallel", "parallel")),
        name="l0_inproj",
    )(x, sc, sh, w)


SB_TQ = 512
SB_TK = 256
SB_UNROLL = 2


def _sb_kernel(q_ref, k_ref, v_ref, tri_ref, o_ref, acc_ref, run_ref):
    tq, tk = SB_TQ, SB_TK
    qi = pl.program_id(2)
    q = q_ref[0]
    lane = lax.broadcasted_iota(jnp.int32, (1, LANES), 1)
    first = lane < HEAD_DIM
    zero = jnp.zeros_like(q)
    q2 = jnp.concatenate([jnp.where(first, q, zero), jnp.where(first, zero, q)], axis=0)
    tri = tri_ref[...]
    acc_ref[...] = jnp.zeros_like(acc_ref)
    run_ref[...] = jnp.zeros_like(run_ref)

    def scores(qrows, key_off):
        return _dot_t(qrows, k_ref[0, pl.ds(key_off, tk), :])

    def suffix(z, causal):
        nlf = jnp.maximum(z, 0.0) + jnp.log(1.0 + jnp.exp2(jnp.abs(z) * -LOG2E))
        if causal is not None:
            nlf = jnp.where(causal, nlf, 0.0)
        return _dot(nlf.astype(_BF16), tri)

    def weigh(z, ext, run, causal, key_off):
        w = jnp.exp(z + ext + jnp.concatenate([run] * (tk // LANES), axis=1))
        if causal is not None:
            w = jnp.where(causal, w, 0.0)
        return (_dot(w.astype(_BF16), v_ref[0, pl.ds(key_off, tk), :]),
                run + jnp.broadcast_to(ext[:, 0:1], run.shape))

    n_diag = tq // tk
    diag = []
    for kd in reversed(range(n_diag)):
        r0 = kd * tk
        key_off = pl.multiple_of(qi * tq + r0, tk)
        row = r0 + lax.broadcasted_iota(jnp.int32, (tq - r0, tk), 0)
        col = r0 + lax.broadcasted_iota(jnp.int32, (tq - r0, tk), 1)
        causal = jnp.concatenate([col < row] * 2, axis=0)
        z = scores(jnp.concatenate([q2[r0:tq], q2[tq + r0:2 * tq]], axis=0), key_off)
        diag.append((r0, key_off, causal, z))
    diag = [(r0, key_off, causal, z, suffix(z, causal)) for r0, key_off, causal, z in diag]
    for r0, key_off, causal, z, ext in diag:
        n = tq - r0
        run = jnp.concatenate([run_ref[r0:tq, :], run_ref[tq + r0:2 * tq, :]], axis=0)
        pv, run = weigh(z, ext, run, causal, key_off)
        for h in range(2):
            rows = slice(h * tq + r0, (h + 1) * tq)
            acc_ref[rows, :] += pv[h * n:(h + 1) * n]
            run_ref[rows, :] = run[h * n:(h + 1) * n]

    def below(jj, carry):
        offs = [pl.multiple_of((qi * n_diag - 1 - SB_UNROLL * jj - u) * tk, tk) for u in range(SB_UNROLL)]
        zs = [scores(q2, off) for off in offs]
        exts = [suffix(z, None) for z in zs]
        run = run_ref[...]
        acc = acc_ref[...]
        for z, ext, off in zip(zs, exts, offs):
            pv, run = weigh(z, ext, run, None, off)
            acc = acc + pv
        acc_ref[...] = acc
        run_ref[...] = run
        return carry
    lax.fori_loop(0, qi * (n_diag // SB_UNROLL), below, 0)
    o_ref[0] = jnp.where(first, acc_ref[0:tq, :], acc_ref[tq:2 * tq, :]).astype(o_ref.dtype)


def _sb_attention(sb):
    bsz, s, _ = sb.shape
    tq, tk = SB_TQ, SB_TK
    npair = SB_W // LANES
    suffix = (np.arange(tk)[:, None] >= np.arange(tk)[None, :]).astype(np.float32)
    tri = -suffix
    return pl.pallas_call(
        _sb_kernel,
        grid=(bsz, npair, s // tq),
        in_specs=[pl.BlockSpec((1, tq, LANES), lambda b, p, i: (b, i, p)),
                  pl.BlockSpec((1, s, LANES), lambda b, p, i: (b, 0, npair + p)),
                  pl.BlockSpec((1, s, LANES), lambda b, p, i: (b, 0, 2 * npair + p)),
                  pl.BlockSpec((tk, tk), lambda b, p, i: (0, 0))],
        out_specs=pl.BlockSpec((1, tq, LANES), lambda b, p, i: (b, i, p)),
        out_shape=jax.ShapeDtypeStruct((bsz, s, SB_W), _BF16),
        scratch_shapes=[pltpu.VMEM((2 * tq, LANES), _F32), pltpu.VMEM((2 * tq, LANES), _F32)],
        compiler_params=_cparams(("parallel", "parallel", "arbitrary")),
        name="sb_attention",
    )(sb, sb, sb, jnp.asarray(tri, _BF16))


def _compress_kernel(x_ref, pos_ref, w1_ref, w2_ref, o_ref):
    half = w1_ref.shape[1] // 2
    ch = x_ref[0, 0]
    top = _dot((ch + pos_ref[0, 0:1, :]).astype(_BF16), w1_ref[0, :half, :])
    bot = _dot((ch + pos_ref[0, 1:2, :]).astype(_BF16), w1_ref[0, half:, :])
    n = ch.shape[0]
    pre = top + pltpu.roll(bot, n - 1, 0)
    o_ref[0, 0] = _dot(_silu(pre).astype(_BF16), w2_ref[0]).astype(o_ref.dtype)


def _compress(kvc, pos, w1, w2rep):
    bsz, _, s, dh = kvc.shape
    n_chunk = s // NSA_CMP_STRIDE
    cw = NSA_CMP_STRIDE * dh
    x = kvc.reshape(bsz, 4, n_chunk, cw)
    hid = w1.shape[-1]
    return pl.pallas_call(
        _compress_kernel,
        grid=(bsz, 4),
        in_specs=[pl.BlockSpec((1, 1, n_chunk, cw), lambda b, j: (b, j, 0, 0)),
                  pl.BlockSpec((1, 2, cw), lambda b, j: (j // 2, 0, 0)),
                  pl.BlockSpec((1, 2 * cw, hid), lambda b, j: (j // 2, 0, 0)),
                  pl.BlockSpec((1, hid, 4 * dh), lambda b, j: (j // 2, 0, 0))],
        out_specs=pl.BlockSpec((1, 1, n_chunk, 4 * dh), lambda b, j: (b, j, 0, 0)),
        out_shape=jax.ShapeDtypeStruct((bsz, 4, n_chunk, 4 * dh), _BF16),
        compiler_params=_cparams(("parallel", "parallel")),
        name="nsa_compress",
    )(x, pos, w1, w2rep)


NSA_TQ = 128
NSA_TK = 512
WIN_SPAN = NSA_WINDOW + NSA_TQ


def _nsa_kernel(*refs):
    s_len = refs[4].shape[1]
    tiles_per_blk = NSA_TK // NSA_TQ
    n_blk = s_len // NSA_TK
    i = pl.program_id(2)
    for v in range(n_blk // 2):
        pl.when(i // tiles_per_blk == v)(functools.partial(_nsa_body, v + 1, n_blk - v, *refs))


def _nsa_body(n_blk_a, n_blk_b, qa_ref, qb_ref, kc_ref, vc_ref, ks_ref, vs_ref, kw_ref, vw_ref, ga_ref, gb_ref,
              ov_ref, ex_ref, o_ref):
    tq, tk = NSA_TQ, NSA_TK
    s_len = ks_ref.shape[1]
    n_tiles = s_len // tq
    n_cmp = kc_ref.shape[2]
    n_slc = s_len // NSA_SLC_BLOCK
    gw = NSA_REP * HEAD_DIM
    i = pl.program_id(2)
    head_of_lane = lax.broadcasted_iota(jnp.int32, (1, gw), 1) // HEAD_DIM

    def unstack(o):
        out = jnp.zeros((tq, gw), _F32)
        for r in range(NSA_REP):
            out = jnp.where(head_of_lane == r, o[r * tq:(r + 1) * tq], out)
        return out

    def tile4(x):
        return jnp.concatenate([x] * NSA_REP, axis=0)

    class Tile:
        pass
    tiles = []
    for q_ref, g_ref, t0, n_blk in ((qa_ref, ga_ref, i * tq, n_blk_a),
                                    (qb_ref, gb_ref, (n_tiles - 1 - i) * tq, n_blk_b)):
        t = Tile()
        q = q_ref[0]
        zero = jnp.zeros_like(q)
        t.qs = jnp.concatenate([jnp.where(head_of_lane == r, q, zero) for r in range(NSA_REP)], axis=0)
        t.t0, t.n_blk, t.g_ref = t0, n_blk, g_ref
        t.t_row = t0 + lax.broadcasted_iota(jnp.int32, (tq, 1), 0)
        t.start = pl.multiple_of(jnp.maximum(t0 - NSA_WINDOW, 0), tq)
        tiles.append(t)

    for t in tiles:
        t.s_c = _dot_t(t.qs, kc_ref[0, 0])
        t.s_w = _dot_t(t.qs, kw_ref[0, pl.ds(t.start, WIN_SPAN), :])
        t.s_s = [_dot_t(t.qs, ks_ref[0, j * tk:(j + 1) * tk, :]) for j in range(t.n_blk)]

    cmp_last = lax.broadcasted_iota(jnp.int32, (1, n_cmp), 1) * NSA_CMP_STRIDE + (NSA_CMP_LEN - 1)
    for t in tiles:
        mask_c = tile4(cmp_last <= t.t_row)
        sm = jnp.where(mask_c, t.s_c, NEG)
        e_c = jnp.where(mask_c, jnp.exp2(sm - jnp.max(sm, axis=1, keepdims=True)), 0.0)
        l_c = jnp.sum(e_c, axis=1, keepdims=True)
        t.p_c = e_c / jnp.where(l_c == 0.0, 1.0, l_c)
    for t in tiles:
        t.o_cmp = unstack(_dot(t.p_c.astype(_BF16), vc_ref[0, 0]))
        p_sum = t.p_c[0:tq] + t.p_c[tq:2 * tq] + t.p_c[2 * tq:3 * tq] + t.p_c[3 * tq:4 * tq]
        hi, lo = _split_bf16(p_sum)
        t.imp = _dot_t(ov_ref[...], jnp.concatenate([hi, lo], axis=1))[:n_slc]

    for t in tiles:
        diff = t.t_row - (t.start + lax.broadcasted_iota(jnp.int32, (1, WIN_SPAN), 1))
        bias_w = jnp.where((diff >= 0) & (diff < NSA_WINDOW), 0.0, NEG)
        sm = t.s_w + tile4(bias_w)
        e_w = jnp.exp2(sm - jnp.max(sm, axis=1, keepdims=True))
        t.l_w = jnp.sum(e_w, axis=1, keepdims=True)
        t.e_w = e_w.astype(_BF16)
    for t in tiles:
        t.o_win = unstack(_dot(t.e_w, vw_ref[0, pl.ds(t.start, WIN_SPAN), :]) / t.l_w)

    blk = lax.broadcasted_iota(jnp.int32, (n_slc, 1), 0)
    for t in tiles:
        cur = (t.t0 + lax.broadcasted_iota(jnp.int32, (1, tq), 1)) // NSA_SLC_BLOCK
        valid = blk <= cur
        forced = (blk == 0) | (blk == cur) | (blk == cur - 1)
        val = jnp.where(valid, jnp.where(forced, SEL_FORCE, t.imp), NEG)
        rank = jnp.zeros((n_slc, tq), _F32)
        for b in range(n_slc):
            vb = val[b:b + 1, :]
            ahead = (vb > val) | ((vb == val) & (blk > b))
            rank = rank + jnp.where(ahead, 1.0, 0.0)
        sel_t = jnp.where(rank < float(min(NSA_SLC_TOPK, n_slc)), 1.0, 0.0)
        sel_t = jnp.concatenate([sel_t, jnp.zeros((LANES - n_slc, tq), _F32)], axis=0)
        t.sel = sel_t.T.astype(_BF16)

    for t in tiles:
        masked = []
        for j in range(t.n_blk):
            chosen = _dot(t.sel, ex_ref[:, j * tk:(j + 1) * tk])
            kpos = j * tk + lax.broadcasted_iota(jnp.int32, (1, tk), 1)
            bias = jnp.where((chosen > 0.5) & (kpos <= t.t_row), 0.0, NEG)
            masked.append(t.s_s[j] + tile4(bias))
        m = masked[0].max(axis=1, keepdims=True)
        for sb in masked[1:]:
            m = jnp.maximum(m, sb.max(axis=1, keepdims=True))
        e_s = [jnp.exp2(sb - m) for sb in masked]
        t.l_s = sum(e.sum(axis=1, keepdims=True) for e in e_s)
        t.e_s = [e.astype(_BF16) for e in e_s]
    for t in tiles:
        acc = _dot(t.e_s[0], vs_ref[0, 0:tk, :])
        for j in range(1, t.n_blk):
            acc = acc + _dot(t.e_s[j], vs_ref[0, j * tk:(j + 1) * tk, :])
        t.o_slc = unstack(acc / t.l_s)

    for half, t in enumerate(tiles):
        g = 1.0 / (1.0 + jnp.exp(-t.g_ref[0]))
        out = jnp.zeros((tq, gw), _F32)
        for branch, o_b in enumerate((t.o_cmp, t.o_slc, t.o_win)):
            gm = jnp.zeros((tq, gw), _F32)
            for r in range(NSA_REP):
                c = 3 * r + branch
                gm = jnp.where(head_of_lane == r, g[:, c:c + 1], gm)
            out = out + gm * o_b
        o_ref[0, half, 0] = out.astype(o_ref.dtype)


def _nsa_attention(nq, cmp_kv, sw, gates):
    bsz, s, _ = nq.shape
    tq = NSA_TQ
    n_cmp = cmp_kv.shape[2]
    n_slc = s // NSA_SLC_BLOCK
    gw = 4 * HEAD_DIM
    cmp_start = np.arange(n_cmp) * NSA_CMP_STRIDE
    slc_start = np.arange(LANES) * NSA_SLC_BLOCK
    ov = ((cmp_start[:, None] < slc_start[None, :] + NSA_SLC_BLOCK)
          & (cmp_start[:, None] + NSA_CMP_LEN > slc_start[None, :])
          & (np.arange(LANES)[None, :] < n_slc)).astype(np.float32)
    ov[n_cmp - 1:] = 0.0
    ov2 = np.concatenate([ov.T, ov.T], axis=1)
    ex = (np.arange(LANES)[:, None] == (np.arange(s)[None, :] // NSA_SLC_BLOCK)).astype(np.float32)
    kv_spec = lambda kind: pl.BlockSpec((1, s, gw), lambda b, g, i, kind=kind: (b, 0, 2 * kind + g))
    n_tiles = s // tq
    n_pair = n_tiles // 2
    out = pl.pallas_call(
        _nsa_kernel,
        grid=(bsz, NSA_KV_GROUPS, n_pair),
        in_specs=[pl.BlockSpec((1, tq, gw), lambda b, g, i: (b, i, g)),
                  pl.BlockSpec((1, tq, gw), lambda b, g, i: (b, n_tiles - 1 - i, g)),
                  pl.BlockSpec((1, 1, n_cmp, gw), lambda b, g, i: (b, g, 0, 0)),
                  pl.BlockSpec((1, 1, n_cmp, gw), lambda b, g, i: (b, 2 + g, 0, 0)),
                  kv_spec(0), kv_spec(1), kv_spec(2), kv_spec(3),
                  pl.BlockSpec((1, tq, LANES), lambda b, g, i: (b, i, g)),
                  pl.BlockSpec((1, tq, LANES), lambda b, g, i: (b, n_tiles - 1 - i, g)),
                  pl.BlockSpec((LANES, 2 * n_cmp), lambda b, g, i: (0, 0)),
                  pl.BlockSpec((LANES, s), lambda b, g, i: (0, 0))],
        out_specs=pl.BlockSpec((1, 2, 1, tq, gw), lambda b, g, i: (b, 0, i, 0, g)),
        out_shape=jax.ShapeDtypeStruct((bsz, 2, n_pair, tq, NSA_Q_W), _BF16),
        compiler_params=_cparams(("parallel", "parallel", "arbitrary")),
        name="nsa_attention",
    )(nq, nq, cmp_kv, cmp_kv, sw, sw, sw, sw, gates, gates, jnp.asarray(ov2, _BF16), jnp.asarray(ex, _BF16))
    return jnp.concatenate([out[:, 0], out[:, 1, ::-1]], axis=1).reshape(bsz, s, NSA_Q_W)


def _outproj_kernel(a_ref, b_ref, wa_ref, wb_ref, x_ref, g_ref, lg_ref, lb_ref, o_ref):
    y = _dot(a_ref[0], wa_ref[...]) + _dot(b_ref[0], wb_ref[...])
    r = DN_ALPHA * x_ref[0] + (1.0 + g_ref[0]) * y
    o_ref[0] = _layer_norm(r, lg_ref[...], lb_ref[...])


def _outproj_ln(o_sb, o_nsa, w_out, x, gate, ln_g, ln_b):
    bsz, s, d = x.shape
    tm = 512
    row = lambda b, i: (b, i, 0)
    vec = pl.BlockSpec((1, d), lambda b, i: (0, 0))
    return pl.pallas_call(
        _outproj_kernel,
        grid=(bsz, s // tm),
        in_specs=[pl.BlockSpec((1, tm, SB_W), row), pl.BlockSpec((1, tm, NSA_Q_W), row),
                  pl.BlockSpec((SB_W, d), lambda b, i: (0, 0)),
                  pl.BlockSpec((NSA_Q_W, d), lambda b, i: (1, 0)),
                  pl.BlockSpec((1, tm, d), row),
                  pl.BlockSpec((1, 1, d), lambda b, i: (b, 0, 0)), vec, vec],
        out_specs=pl.BlockSpec((1, tm, d), row),
        out_shape=jax.ShapeDtypeStruct((bsz, s, d), _F32),
        compiler_params=_cparams(("parallel", "parallel")),
        name="l0_outproj_ln",
    )(o_sb, o_nsa, w_out, w_out, x, gate, ln_g.reshape(1, d), ln_b.reshape(1, d))


FFN_TF = 256


def _swiglu(xb, w1, w3, w2):
    ff = w1.shape[1]
    acc = jnp.zeros((xb.shape[0], w2.shape[1]), _F32)
    for c in range(ff // FFN_TF):
        cs = slice(c * FFN_TF, (c + 1) * FFN_TF)
        a = _silu(_dot(xb, w1[:, cs])) * _dot(xb, w3[:, cs])
        acc = acc + _dot(a.astype(_BF16), w2[cs, :])
    return acc


def _ffn_kernel(x_ref, sc_ref, sh_ref, g_ref, w1_ref, w3_ref, w2_ref, lg_ref, lb_ref, o_ref):
    x = x_ref[0]
    h = (x * (1.0 + sc_ref[0]) + sh_ref[0]).astype(_BF16)
    r = DN_ALPHA * x + (1.0 + g_ref[0]) * _swiglu(h, w1_ref, w3_ref, w2_ref)
    o_ref[0] = _layer_norm(r, lg_ref[...], lb_ref[...])


def _ffn_ln(x, sc, sh, gate, w1, w3, w2, ln_g, ln_b):
    bsz, s, d = x.shape
    ff = w1.shape[1]
    tm = 512
    row = lambda b, i: (b, i, 0)
    mod = pl.BlockSpec((1, 1, d), lambda b, i: (b, 0, 0))
    vec = pl.BlockSpec((1, d), lambda b, i: (0, 0))
    return pl.pallas_call(
        _ffn_kernel,
        grid=(bsz, s // tm),
        in_specs=[pl.BlockSpec((1, tm, d), row), mod, mod, mod,
                  pl.BlockSpec((d, ff), lambda b, i: (0, 0)),
                  pl.BlockSpec((d, ff), lambda b, i: (0, 0)),
                  pl.BlockSpec((ff, d), lambda b, i: (0, 0)), vec, vec],
        out_specs=pl.BlockSpec((1, tm, d), row),
        out_shape=jax.ShapeDtypeStruct((bsz, s, d), _F32),
        compiler_params=_cparams(("parallel", "parallel")),
        name="l0_ffn_ln",
    )(x, sc, sh, gate, w1, w3, w2, ln_g.reshape(1, d), ln_b.reshape(1, d))


def _conv_kernel(x_ref, sc_ref, sh_ref, g_ref, win_ref, taps_ref, wout_ref, lg_ref, lb_ref, o_ref, tail_ref):
    ts, d = x_ref.shape[1], x_ref.shape[2]

    @pl.when(pl.program_id(1) == 0)
    def _():
        tail_ref[...] = jnp.zeros_like(tail_ref)

    x = x_ref[0]
    h = (x * (1.0 + sc_ref[0]) + sh_ref[0]).astype(_BF16)
    z = _dot(h, win_ref[:, d:2 * d]) * _dot(h, win_ref[:, 2 * d:3 * d])
    row = lax.broadcasted_iota(jnp.int32, (ts, 1), 0)
    prev1 = tail_ref[7:8, :]
    prev2 = tail_ref[6:7, :]
    z1 = jnp.where(row == 0, prev1, pltpu.roll(z, 1, 0))
    z2 = jnp.where(row == 0, prev2, jnp.where(row == 1, prev1, pltpu.roll(z, 2, 0)))
    zc = taps_ref[0:1, :] * z2 + taps_ref[1:2, :] * z1 + taps_ref[2:3, :] * z
    tail_ref[...] = z[ts - 8:ts, :]
    y = _dot((_dot(h, win_ref[:, 0:d]) * zc).astype(_BF16), wout_ref[...])
    r = DN_ALPHA * x + (1.0 + g_ref[0]) * y
    o_ref[0] = _layer_norm(r, lg_ref[...], lb_ref[...])


def _conv_ln(x, sc, sh, gate, w_in, taps, w_out, ln_g, ln_b):
    bsz, s, d = x.shape
    ts = 512
    row = lambda b, i: (b, i, 0)
    mod = pl.BlockSpec((1, 1, d), lambda b, i: (b, 0, 0))
    vec = pl.BlockSpec((1, d), lambda b, i: (0, 0))
    return pl.pallas_call(
        _conv_kernel,
        grid=(bsz, s // ts),
        in_specs=[pl.BlockSpec((1, ts, d), row), mod, mod, mod,
                  pl.BlockSpec((d, 3 * d), lambda b, i: (0, 0)),
                  pl.BlockSpec((taps.shape[0], d), lambda b, i: (0, 0)),
                  pl.BlockSpec((d, d), lambda b, i: (0, 0)), vec, vec],
        out_specs=pl.BlockSpec((1, ts, d), row),
        out_shape=jax.ShapeDtypeStruct((bsz, s, d), _F32),
        scratch_shapes=[pltpu.VMEM((8, d), _F32)],
        compiler_params=_cparams(("arbitrary", "arbitrary")),
        name="l1_conv_ln",
    )(x, sc, sh, gate, w_in, taps, w_out, ln_g.reshape(1, d), ln_b.reshape(1, d))


ROUTE_TM = 512


def _router_kernel(x_ref, sc_ref, sh_ref, rw_ref, rb_ref, tri_ref, h_ref, route_ref, route_t_ref, cnt_ref, run_ref):
    tm = ROUTE_TM

    @pl.when((pl.program_id(0) == 0) & (pl.program_id(1) == 0))
    def _():
        run_ref[...] = jnp.zeros_like(run_ref)

    h = x_ref[0] * (1.0 + sc_ref[0]) + sh_ref[0]
    h_ref[0] = h
    lane = lax.broadcasted_iota(jnp.int32, (1, LANES), 1)
    h_hi, h_lo = _split_bf16(h)
    two = _dot(h_hi, rw_ref[...])
    logits = two[:, :LANES] + two[:, LANES:] + _dot(h_lo, rw_ref[:, :LANES]) + rb_ref[...]
    logits = jnp.where(lane < N_EXPERTS, logits, -3e38)
    m1 = jnp.max(logits, axis=1, keepdims=True)
    e1 = jnp.min(jnp.where(logits == m1, lane, LANES), axis=1, keepdims=True)
    rest = jnp.where(lane == e1, -3e38, logits)
    m2 = jnp.max(rest, axis=1, keepdims=True)
    e2 = jnp.min(jnp.where(rest == m2, lane, LANES), axis=1, keepdims=True)
    ex = jnp.exp(m2 - m1)
    g1 = 1.0 / (1.0 + ex)
    g2 = ex / (1.0 + ex)
    oh1 = jnp.where(lane == e1, 1.0, 0.0)
    oh2 = jnp.where(lane == e2, 1.0, 0.0)
    oh = oh1 + oh2
    before = run_ref[...] + _dot(tri_ref[...], oh.astype(_BF16))
    p1 = jnp.sum(oh1 * before, axis=1, keepdims=True)
    p2 = jnp.sum(oh2 * before, axis=1, keepdims=True)
    run_ref[...] += jnp.sum(oh, axis=0, keepdims=True)
    cnt_ref[...] = run_ref[...]
    vals = (e1.astype(_F32), e2.astype(_F32), p1, p2, g1, g2)
    out = jnp.zeros((tm, LANES), _F32)
    for k, v in enumerate(vals):
        out = jnp.where(lane == k, v, out)
    route_ref[0] = out
    route_t_ref[0, 0] = out.T[:8, :]


def _router(x, sc, sh, rw, rb):
    bsz, s, d = x.shape
    tm = ROUTE_TM
    row = lambda b, i: (b, i, 0)
    mod = pl.BlockSpec((1, 1, d), lambda b, i: (b, 0, 0))
    tri = (np.arange(tm)[None, :] < np.arange(tm)[:, None]).astype(np.float32)
    return pl.pallas_call(
        _router_kernel,
        grid=(bsz, s // tm),
        in_specs=[pl.BlockSpec((1, tm, d), row), mod, mod,
                  pl.BlockSpec((d, 2 * LANES), lambda b, i: (0, 0)),
                  pl.BlockSpec((1, LANES), lambda b, i: (0, 0)),
                  pl.BlockSpec((tm, tm), lambda b, i: (0, 0))],
        out_specs=[pl.BlockSpec((1, tm, d), row),
                   pl.BlockSpec((1, tm, LANES), row),
                   pl.BlockSpec((1, 1, 8, tm), lambda b, i: (b, i, 0, 0)),
                   pl.BlockSpec((1, LANES), lambda b, i: (0, 0))],
        out_shape=[jax.ShapeDtypeStruct((bsz, s, d), _F32),
                   jax.ShapeDtypeStruct((bsz, s, LANES), _F32),
                   jax.ShapeDtypeStruct((bsz, s // tm, 8, tm), _F32),
                   jax.ShapeDtypeStruct((1, LANES), _F32)],
        scratch_shapes=[pltpu.VMEM((1, LANES), _F32)],
        compiler_params=_cparams(("arbitrary", "arbitrary")),
        name="l1_router",
    )(x, sc, sh, rw, rb, jnp.asarray(tri, _BF16))


def _row_tokens_kernel(d1_ref, d2_ref, lo_ref, hi_ref, o_ref):
    def clear_range(k, carry):
        def clear(r, c):
            o_ref[r] = 0
            return c
        return lax.fori_loop(lo_ref[k], hi_ref[k], clear, carry)
    lax.fori_loop(0, lo_ref.shape[0], clear_range, 0)

    def place(a, carry):
        o_ref[d1_ref[a]] = a
        o_ref[d2_ref[a]] = a
        return carry
    lax.fori_loop(0, d1_ref.shape[0], place, 0, unroll=8)


def _row_tokens(dest1, dest2, pad_lo, pad_hi, rows):
    smem = pl.BlockSpec(memory_space=pltpu.SMEM)
    return pl.pallas_call(
        _row_tokens_kernel,
        in_specs=[smem, smem, smem, smem],
        out_specs=smem,
        out_shape=jax.ShapeDtypeStruct((rows,), jnp.int32),
        name="l1_row_tokens",
    )(dest1, dest2, pad_lo, pad_hi)


GATHER_UNROLL = 8


def _row_gather(src_hbm, idx_ref, base, dst_ref, sem, n):
    def issue(r, carry):
        tok = idx_ref[base + r]
        pltpu.make_async_copy(src_hbm.at[pl.ds(tok, 1)], dst_ref.at[pl.ds(r, 1)], sem).start()
        return carry
    lax.fori_loop(0, n, issue, 0, unroll=GATHER_UNROLL)


def _row_gather_wait(src_hbm, dst_ref, sem, n):
    pltpu.make_async_copy(src_hbm.at[pl.ds(0, n)], dst_ref, sem).wait()


def _expert_kernel(blk_e_ref, n_used_ref, row_tok_ref, h_hbm, w1_ref, w3_ref, w2_ref, y_ref, xbuf, sems):
    i = pl.program_id(0)
    n_used = n_used_ref[0]
    rb = MOE_ROW_BLOCK
    slot = lax.rem(i, 2)

    def gather(block, s):
        for r in range(rb):
            pltpu.make_async_copy(h_hbm.at[pl.ds(row_tok_ref[block * rb + r], 1)], xbuf.at[s, pl.ds(r, 1)],
                                  sems.at[s]).start()

    @pl.when((i == 0) & (n_used > 0))
    def _():
        _row_gather(h_hbm, row_tok_ref, 0, xbuf.at[0], sems.at[0], rb)

    @pl.when(i + 1 < n_used)
    def _():
        gather(i + 1, 1 - slot)

    @pl.when(i < n_used)
    def _():
        _row_gather_wait(h_hbm, xbuf.at[slot], sems.at[slot], rb)
        y_ref[...] = _swiglu(xbuf[slot].astype(_BF16), w1_ref.at[0], w3_ref.at[0], w2_ref.at[0])

    @pl.when(i >= n_used)
    def _():
        y_ref[...] = jnp.zeros_like(y_ref)


def _experts(blk_e, n_used, row_tok, h_flat, w1, w3, w2):
    t, d = h_flat.shape
    n_blocks = blk_e.shape[0]
    rb = MOE_ROW_BLOCK
    ff = w1.shape[2]
    grid_spec = pltpu.PrefetchScalarGridSpec(
        num_scalar_prefetch=3,
        grid=(n_blocks,),
        in_specs=[pl.BlockSpec(memory_space=pl.ANY),
                  pl.BlockSpec((1, d, ff), lambda i, be, nu, rt: (be[i], 0, 0)),
                  pl.BlockSpec((1, d, ff), lambda i, be, nu, rt: (be[i], 0, 0)),
                  pl.BlockSpec((1, ff, d), lambda i, be, nu, rt: (be[i], 0, 0))],
        out_specs=pl.BlockSpec((rb, d), lambda i, be, nu, rt: (i, 0)),
        scratch_shapes=[pltpu.VMEM((2, rb, d), _F32), pltpu.SemaphoreType.DMA((2,))],
    )
    return pl.pallas_call(
        _expert_kernel,
        grid_spec=grid_spec,
        out_shape=jax.ShapeDtypeStruct((n_blocks * rb, d), _F32),
        compiler_params=_cparams(("arbitrary",)),
        name="l1_experts",
    )(blk_e, n_used, row_tok, h_flat, w1, w3, w2)


COMB_TM = 256


def _combine_kernel(d1_ref, d2_ref, y_hbm, x_ref, route_ref, g_ref, lg_ref, lb_ref, o_ref, y1buf, y2buf, sems):
    tm = COMB_TM
    i = pl.program_id(0) * pl.num_programs(1) + pl.program_id(1)
    n = pl.num_programs(0) * pl.num_programs(1)
    slot = lax.rem(i, 2)

    def gather(tile, s):
        for r in range(tm):
            pltpu.make_async_copy(y_hbm.at[pl.ds(d1_ref[tile * tm + r], 1)], y1buf.at[s, pl.ds(r, 1)],
                                  sems.at[0, s]).start(priority=0)
            pltpu.make_async_copy(y_hbm.at[pl.ds(d2_ref[tile * tm + r], 1)], y2buf.at[s, pl.ds(r, 1)],
                                  sems.at[1, s]).start(priority=1)

    @pl.when(i == 0)
    def _():
        gather(0, 0)

    @pl.when(i + 1 < n)
    def _():
        gather(i + 1, 1 - slot)

    _row_gather_wait(y_hbm, y1buf.at[slot], sems.at[0, slot], tm)
    _row_gather_wait(y_hbm, y2buf.at[slot], sems.at[1, slot], tm)
    route = route_ref[0]
    y = route[:, 4:5] * y1buf[slot] + route[:, 5:6] * y2buf[slot]
    r = DN_ALPHA * x_ref[0] + (1.0 + g_ref[0]) * y
    o_ref[0] = _layer_norm(r, lg_ref[...], lb_ref[...])


def _combine_ln(dest1, dest2, y, x, route, gate, ln_g, ln_b):
    bsz, s, d = x.shape
    tm = COMB_TM
    row = lambda b, i, d1, d2: (b, i, 0)
    vec = pl.BlockSpec((1, d), lambda b, i, d1, d2: (0, 0))
    grid_spec = pltpu.PrefetchScalarGridSpec(
        num_scalar_prefetch=2,
        grid=(bsz, s // tm),
        in_specs=[pl.BlockSpec(memory_space=pl.ANY),
                  pl.BlockSpec((1, tm, d), row),
                  pl.BlockSpec((1, tm, LANES), row),
                  pl.BlockSpec((1, 1, d), lambda b, i, d1, d2: (b, 0, 0)), vec, vec],
        out_specs=pl.BlockSpec((1, tm, d), row),
        scratch_shapes=[pltpu.VMEM((2, tm, d), _F32), pltpu.VMEM((2, tm, d), _F32),
                        pltpu.SemaphoreType.DMA((2, 2))],
    )
    return pl.pallas_call(
        _combine_kernel,
        grid_spec=grid_spec,
        out_shape=jax.ShapeDtypeStruct((bsz, s, d), _F32),
        compiler_params=_cparams(("arbitrary", "arbitrary")),
        name="l1_combine_ln",
    )(dest1, dest2, y, x, route, gate, ln_g.reshape(1, d), ln_b.reshape(1, d))


def _arrange_inproj_weight(w):
    d = w.shape[0]
    kv0 = 3 * SB_W + NSA_Q_W
    g0 = kv0 + 6 * NSA_KV_GROUPS * HEAD_DIM
    per_group = NSA_REP * 3
    pad = jnp.zeros((d, LANES - per_group), w.dtype)
    cols = [w[:, :SB_W] * QK_SCALE, w[:, SB_W:3 * SB_W], w[:, 3 * SB_W:kv0] * (QK_SCALE * LOG2E), w[:, kv0:g0]]
    for g in range(NSA_KV_GROUPS):
        cols += [w[:, g0 + g * per_group:g0 + (g + 1) * per_group], pad]
    return jnp.concatenate(cols, axis=1).astype(_BF16)


def _mods(mod, i, bsz, d):
    m = mod[i].reshape(bsz, 6, 1, d)
    return [m[:, k] for k in range(6)]


def kernel(x, c, ada_w, ada_b, ln_g, ln_b, mix_w_in, cmp_pos, cmp_w1, cmp_w2, mix_w_out, ffn_w1, ffn_w3, ffn_w2,
           conv_w_in, conv_taps, conv_w_out, router_w, router_b, exp_w1, exp_w3, exp_w2):
    bsz, s, d = x.shape
    t = bsz * s
    mod = _ada_mod(c, ada_w, ada_b)

    sh1, sc1, g1, sh2, sc2, g2 = _mods(mod, 0, bsz, d)
    sb, nq, kvc, sw, gates = _inproj(x, sc1, sh1, _arrange_inproj_weight(mix_w_in[0]))
    o_sb = _sb_attention(sb)
    pos = cmp_pos[0].reshape(2, 2, NSA_CMP_STRIDE * HEAD_DIM)
    w2rep = jnp.tile(cmp_w2[0], (1, 1, NSA_REP)).astype(_BF16)
    cmp_kv = _compress(kvc, pos, cmp_w1[0].astype(_BF16), w2rep)
    o_nsa = _nsa_attention(nq, cmp_kv, sw, gates)
    x = _outproj_ln(o_sb, o_nsa, mix_w_out[0].astype(_BF16), x, g1, ln_g[0, 0], ln_b[0, 0])
    x = _ffn_ln(x, sc2, sh2, g2, ffn_w1[0].astype(_BF16), ffn_w3[0].astype(_BF16), ffn_w2[0].astype(_BF16),
                ln_g[0, 1], ln_b[0, 1])

    sh1, sc1, g1, sh2, sc2, g2 = _mods(mod, 1, bsz, d)
    x = _conv_ln(x, sc1, sh1, g1, conv_w_in[0].astype(_BF16), conv_taps[0], conv_w_out[0].astype(_BF16),
                 ln_g[1, 0], ln_b[1, 0])
    rw = jnp.concatenate(_split_bf16(jnp.pad(router_w[0], ((0, 0), (0, LANES - N_EXPERTS)))), axis=1)
    rb = jnp.pad(router_b[0], (0, LANES - N_EXPERTS)).reshape(1, LANES)
    h2, route, route_t, counts = _router(x, sc2, sh2, rw, rb)

    field = lambda k: route_t[:, :, k, :].reshape(t).astype(jnp.int32)
    e1, e2, p1, p2 = field(0), field(1), field(2), field(3)
    cnt = counts[0, :N_EXPERTS].astype(jnp.int32)
    rb_rows = MOE_ROW_BLOCK
    padded = (cnt + rb_rows - 1) // rb_rows * rb_rows
    pends = jnp.cumsum(padded)
    pstarts = pends - padded
    dest1 = pstarts[e1] + p1
    dest2 = pstarts[e2] + p2
    n_blocks = -(-(t * 2) // rb_rows) + N_EXPERTS
    rows = n_blocks * rb_rows
    pad_lo = jnp.concatenate([pstarts + cnt, pends[-1:]]).astype(jnp.int32)
    pad_hi = jnp.concatenate([pends, jnp.full((1,), rows, jnp.int32)]).astype(jnp.int32)
    row_tok = _row_tokens(dest1, dest2, pad_lo, pad_hi, rows)
    blk_start = jnp.arange(n_blocks, dtype=jnp.int32) * rb_rows
    blk_e = jnp.minimum(jnp.sum(pends[None, :] <= blk_start[:, None], axis=1), N_EXPERTS - 1).astype(jnp.int32)
    n_used = (pends[-1:] // rb_rows).astype(jnp.int32)

    y = _experts(blk_e, n_used, row_tok, h2.reshape(t, d), exp_w1[0].astype(_BF16), exp_w3[0].astype(_BF16),
                 exp_w2[0].astype(_BF16))
    return _combine_ln(dest1, dest2, y, x, route, g2, ln_g[1, 1], ln_b[1, 1])
```

```python
import functools

import numpy as np
import jax
import jax.numpy as jnp
from jax import lax
from jax.experimental import pallas as pl
from jax.experimental.pallas import tpu as pltpu

HEAD_DIM = 64
SB_HEADS = 8
NSA_HEADS = 8
NSA_KV_GROUPS = 2
NSA_REP = NSA_HEADS // NSA_KV_GROUPS
NSA_CMP_LEN = 32
NSA_CMP_STRIDE = 16
NSA_SLC_BLOCK = 64
NSA_SLC_TOPK = 16
NSA_WINDOW = 512
SB_W = SB_HEADS * HEAD_DIM
NSA_Q_W = NSA_HEADS * HEAD_DIM
N_EXPERTS = 8
MOE_ROW_BLOCK = 512
DEPTH = 2
LN_EPS = 1e-5
DN_ALPHA = (2 * DEPTH) ** 0.25
NEG = -1e30
SEL_FORCE = 1e30
QK_SCALE = HEAD_DIM ** -0.5
LOG2E = 1.4426950408889634

LANES = 128
VMEM_LIMIT = 56 * 1024 * 1024

C_SB = 0
C_NQ = 3 * SB_W
C_CMP = C_NQ + NSA_Q_W
C_SW = C_CMP + 4 * HEAD_DIM
C_GATE = C_SW + 4 * 2 * HEAD_DIM
C_END = C_GATE + 2 * LANES

_F32 = jnp.float32
_BF16 = jnp.bfloat16


def _cparams(sem, vmem=VMEM_LIMIT):
    return pltpu.CompilerParams(dimension_semantics=sem, vmem_limit_bytes=vmem)


def _dot(a, b):
    return jnp.dot(a, b, preferred_element_type=_F32)


def _dot_t(a, b):
    return lax.dot_general(a, b, (((1,), (1,)), ((), ())), preferred_element_type=_F32)


def _split_bf16(x):
    hi = x.astype(_BF16)
    lo = (x - hi.astype(_F32)).astype(_BF16)
    return hi, lo


def _silu(x):
    return x / (1.0 + jnp.exp(-x))


def _layer_norm(r, g, b):
    mu = jnp.mean(r, axis=-1, keepdims=True)
    d = r - mu
    var = jnp.mean(d * d, axis=-1, keepdims=True)
    return d * lax.rsqrt(var + LN_EPS) * g + b


def _ada_kernel(c_ref, w_ref, b_ref, o_ref):
    cond = _silu(c_ref[...])
    o_ref[0] = jnp.dot(cond, w_ref[0], preferred_element_type=_F32,
                       precision=lax.Precision.HIGHEST) + b_ref[0]


def _ada_mod(c, ada_w, ada_b):
    depth, d, n = ada_w.shape
    bsz = c.shape[0]
    tn = n // 6
    return pl.pallas_call(
        _ada_kernel,
        grid=(depth, n // tn),
        in_specs=[pl.BlockSpec((bsz, d), lambda i, j: (0, 0)),
                  pl.BlockSpec((1, d, tn), lambda i, j: (i, 0, j)),
                  pl.BlockSpec((1, 1, tn), lambda i, j: (i, 0, j))],
        out_specs=pl.BlockSpec((1, bsz, tn), lambda i, j: (i, 0, j)),
        out_shape=jax.ShapeDtypeStruct((depth, bsz, n), _F32),
        compiler_params=_cparams(("parallel", "parallel")),
        name="ada_mod",
    )(c, ada_w, ada_b.reshape(depth, 1, n))


def _inproj_kernel(x_ref, sc_ref, sh_ref, w_ref, sb_ref, nq_ref, cmp_ref, sw_ref, gate_ref):
    h = (x_ref[0] * (1.0 + sc_ref[0]) + sh_ref[0]).astype(_BF16)
    sb_ref[0] = _dot(h, w_ref[:, C_SB:C_NQ]).astype(_BF16)
    nq_ref[0] = _dot(h, w_ref[:, C_NQ:C_CMP]).astype(_BF16)
    kvc = _dot(h, w_ref[:, C_CMP:C_SW])
    for j in range(4):
        cmp_ref[0, j] = kvc[:, j * HEAD_DIM:(j + 1) * HEAD_DIM]
    lane = lax.broadcasted_iota(jnp.int32, (1, LANES), 1)
    first = lane < HEAD_DIM
    for kind in range(4):
        a = _dot(h, w_ref[:, C_SW + kind * LANES:C_SW + (kind + 1) * LANES])
        r = pltpu.roll(a, HEAD_DIM, 1)
        g0 = jnp.where(first, a, r).astype(_BF16)
        g1 = jnp.where(first, r, a).astype(_BF16)
        sw_ref[0, :, (2 * kind) * 256:(2 * kind + 1) * 256] = jnp.concatenate([g0, g0], axis=1)
        sw_ref[0, :, (2 * kind + 1) * 256:(2 * kind + 2) * 256] = jnp.concatenate([g1, g1], axis=1)
    gate_ref[0] = _dot(h, w_ref[:, C_GATE:C_END])


def _inproj(x, sc, sh, w):
    bsz, s, d = x.shape
    tm = 512
    row = lambda b, i: (b, i, 0)
    return pl.pallas_call(
        _inproj_kernel,
        grid=(bsz, s // tm),
        in_specs=[pl.BlockSpec((1, tm, d), row),
                  pl.BlockSpec((1, 1, d), lambda b, i: (b, 0, 0)),
                  pl.BlockSpec((1, 1, d), lambda b, i: (b, 0, 0)),
                  pl.BlockSpec((d, C_END), lambda b, i: (0, 0))],
        out_specs=[pl.BlockSpec((1, tm, 3 * SB_W), row),
                   pl.BlockSpec((1, tm, NSA_Q_W), row),
                   pl.BlockSpec((1, 4, tm, HEAD_DIM), lambda b, i: (b, 0, i, 0)),
                   pl.BlockSpec((1, tm, 2048), row),
                   pl.BlockSpec((1, tm, 2 * LANES), row)],
        out_shape=[jax.ShapeDtypeStruct((bsz, s, 3 * SB_W), _BF16),
                   jax.ShapeDtypeStruct((bsz, s, NSA_Q_W), _BF16),
                   jax.ShapeDtypeStruct((bsz, 4, s, HEAD_DIM), _F32),
                   jax.ShapeDtypeStruct((bsz, s, 2048), _BF16),
                   jax.ShapeDtypeStruct((bsz, s, 2 * LANES), _F32)],
        compiler_params=_cparams(("parallel", "parallel")),
        name="l0_inproj",
    )(x, sc, sh, w)


SB_TQ = 512
SB_TK = 256
SB_UNROLL = 2


def _sb_kernel(q_ref, k_ref, v_ref, tri_ref, o_ref, acc_ref, run_ref):
    tq, tk = SB_TQ, SB_TK
    qi = pl.program_id(2)
    q = q_ref[0]
    lane = lax.broadcasted_iota(jnp.int32, (1, LANES), 1)
    first = lane < HEAD_DIM
    zero = jnp.zeros_like(q)
    q2 = jnp.concatenate([jnp.where(first, q, zero), jnp.where(first, zero, q)], axis=0)
    tri = tri_ref[...]
    acc_ref[...] = jnp.zeros_like(acc_ref)
    run_ref[...] = jnp.zeros_like(run_ref)

    def scores(qrows, key_off):
        return _dot_t(qrows, k_ref[0, pl.ds(key_off, tk), :])

    def suffix(z, causal):
        nlf = jnp.maximum(z, 0.0) + jnp.log(1.0 + jnp.exp2(jnp.abs(z) * -LOG2E))
        if causal is not None:
            nlf = jnp.where(causal, nlf, 0.0)
        return _dot(nlf.astype(_BF16), tri)

    def weigh(z, ext, run, causal, key_off):
        w = jnp.exp(z + ext + jnp.concatenate([run] * (tk // LANES), axis=1))
        if causal is not None:
            w = jnp.where(causal, w, 0.0)
        return (_dot(w.astype(_BF16), v_ref[0, pl.ds(key_off, tk), :]),
                run + jnp.broadcast_to(ext[:, 0:1], run.shape))

    n_diag = tq // tk
    diag = []
    for kd in reversed(range(n_diag)):
        r0 = kd * tk
        key_off = pl.multiple_of(qi * tq + r0, tk)
        row = r0 + lax.broadcasted_iota(jnp.int32, (tq - r0, tk), 0)
        col = r0 + lax.broadcasted_iota(jnp.int32, (tq - r0, tk), 1)
        causal = jnp.concatenate([col < row] * 2, axis=0)
        z = scores(jnp.concatenate([q2[r0:tq], q2[tq + r0:2 * tq]], axis=0), key_off)
        diag.append((r0, key_off, causal, z))
    diag = [(r0, key_off, causal, z, suffix(z, causal)) for r0, key_off, causal, z in diag]
    for r0, key_off, causal, z, ext in diag:
        n = tq - r0
        run = jnp.concatenate([run_ref[r0:tq, :], run_ref[tq + r0:2 * tq, :]], axis=0)
        pv, run = weigh(z, ext, run, causal, key_off)
        for h in range(2):
            rows = slice(h * tq + r0, (h + 1) * tq)
            acc_ref[rows, :] += pv[h * n:(h + 1) * n]
            run_ref[rows, :] = run[h * n:(h + 1) * n]

    def below(jj, carry):
        offs = [pl.multiple_of((qi * n_diag - 1 - SB_UNROLL * jj - u) * tk, tk) for u in range(SB_UNROLL)]
        zs = [scores(q2, off) for off in offs]
        exts = [suffix(z, None) for z in zs]
        run = run_ref[...]
        acc = acc_ref[...]
        for z, ext, off in zip(zs, exts, offs):
            pv, run = weigh(z, ext, run, None, off)
            acc = acc + pv
        acc_ref[...] = acc
        run_ref[...] = run
        return carry
    lax.fori_loop(0, qi * (n_diag // SB_UNROLL), below, 0)
    o_ref[0] = jnp.where(first, acc_ref[0:tq, :], acc_ref[tq:2 * tq, :]).astype(o_ref.dtype)


def _sb_attention(sb):
    bsz, s, _ = sb.shape
    tq, tk = SB_TQ, SB_TK
    npair = SB_W // LANES
    suffix = (np.arange(tk)[:, None] >= np.arange(tk)[None, :]).astype(np.float32)
    tri = -suffix
    return pl.pallas_call(
        _sb_kernel,
        grid=(bsz, npair, s // tq),
        in_specs=[pl.BlockSpec((1, tq, LANES), lambda b, p, i: (b, i, p)),
                  pl.BlockSpec((1, s, LANES), lambda b, p, i: (b, 0, npair + p)),
                  pl.BlockSpec((1, s, LANES), lambda b, p, i: (b, 0, 2 * npair + p)),
                  pl.BlockSpec((tk, tk), lambda b, p, i: (0, 0))],
        out_specs=pl.BlockSpec((1, tq, LANES), lambda b, p, i: (b, i, p)),
        out_shape=jax.ShapeDtypeStruct((bsz, s, SB_W), _BF16),
        scratch_shapes=[pltpu.VMEM((2 * tq, LANES), _F32), pltpu.VMEM((2 * tq, LANES), _F32)],
        compiler_params=_cparams(("parallel", "parallel", "arbitrary")),
        name="sb_attention",
    )(sb, sb, sb, jnp.asarray(tri, _BF16))


def _compress_kernel(x_ref, pos_ref, w1_ref, w2_ref, o_ref):
    half = w1_ref.shape[1] // 2
    n_chunk = x_ref.shape[2] // NSA_CMP_STRIDE
    ch = jnp.concatenate([x_ref[0, 0, pl.ds(l, n_chunk, stride=NSA_CMP_STRIDE), :]
                          for l in range(NSA_CMP_STRIDE)], axis=1)
    top = _dot((ch + pos_ref[0, 0:1, :]).astype(_BF16), w1_ref[0, :half, :])
    bot = _dot((ch + pos_ref[0, 1:2, :]).astype(_BF16), w1_ref[0, half:, :])
    n = ch.shape[0]
    pre = top + pltpu.roll(bot, n - 1, 0)
    o_ref[0, 0] = _dot(_silu(pre).astype(_BF16), w2_ref[0]).astype(o_ref.dtype)


def _compress(kvc, pos, w1, w2rep):
    bsz, _, s, dh = kvc.shape
    n_chunk = s // NSA_CMP_STRIDE
    cw = NSA_CMP_STRIDE * dh
    hid = w1.shape[-1]
    return pl.pallas_call(
        _compress_kernel,
        grid=(bsz, 4),
        in_specs=[pl.BlockSpec((1, 1, s, dh), lambda b, j: (b, j, 0, 0)),
                  pl.BlockSpec((1, 2, cw), lambda b, j: (j // 2, 0, 0)),
                  pl.BlockSpec((1, 2 * cw, hid), lambda b, j: (j // 2, 0, 0)),
                  pl.BlockSpec((1, hid, 4 * dh), lambda b, j: (j // 2, 0, 0))],
        out_specs=pl.BlockSpec((1, 1, n_chunk, 4 * dh), lambda b, j: (b, j, 0, 0)),
        out_shape=jax.ShapeDtypeStruct((bsz, 4, n_chunk, 4 * dh), _BF16),
        compiler_params=_cparams(("parallel", "parallel")),
        name="nsa_compress",
    )(kvc, pos, w1, w2rep)


NSA_TQ = 128
NSA_TK = 512
WIN_SPAN = NSA_WINDOW + NSA_TQ


def _nsa_kernel(*refs):
    s_len = refs[3].shape[1]
    tiles_per_blk = NSA_TK // NSA_TQ
    n_blk = s_len // NSA_TK
    i = pl.program_id(1)
    for v in range(n_blk // 2):
        pl.when(i // tiles_per_blk == v)(functools.partial(_nsa_body, v + 1, n_blk - v, *refs))


def _nsa_body(n_blk_a, n_blk_b, qa_ref, qb_ref, cmp_ref, sw_ref, ga_ref, gb_ref, ov_ref, ex_ref, o_ref):
    tq, tk = NSA_TQ, NSA_TK
    s_len = sw_ref.shape[1]
    n_tiles = s_len // tq
    n_cmp = cmp_ref.shape[2]
    n_slc = s_len // NSA_SLC_BLOCK
    gw = NSA_REP * HEAD_DIM
    i = pl.program_id(1)
    head_of_lane = lax.broadcasted_iota(jnp.int32, (1, gw), 1) // HEAD_DIM

    def kv_lanes(kind, g):
        return slice((2 * kind + g) * gw, (2 * kind + g + 1) * gw)

    def unstack(o):
        out = jnp.zeros((tq, gw), _F32)
        for r in range(NSA_REP):
            out = jnp.where(head_of_lane == r, o[r * tq:(r + 1) * tq], out)
        return out

    def tile4(x):
        return jnp.concatenate([x] * NSA_REP, axis=0)

    class Tile:
        pass
    tiles = []
    for half, (q_ref, g_ref, t0, n_blk) in enumerate(((qa_ref, ga_ref, i * tq, n_blk_a),
                                                      (qb_ref, gb_ref, (n_tiles - 1 - i) * tq, n_blk_b))):
        for g in range(NSA_KV_GROUPS):
            t = Tile()
            q = q_ref[0, :, g * gw:(g + 1) * gw]
            zero = jnp.zeros_like(q)
            t.qs = jnp.concatenate([jnp.where(head_of_lane == r, q, zero) for r in range(NSA_REP)], axis=0)
            t.half, t.g, t.t0, t.n_blk, t.g_ref = half, g, t0, n_blk, g_ref
            t.t_row = t0 + lax.broadcasted_iota(jnp.int32, (tq, 1), 0)
            t.start = pl.multiple_of(jnp.maximum(t0 - NSA_WINDOW, 0), tq)
            tiles.append(t)

    for t in tiles:
        t.s_c = _dot_t(t.qs, cmp_ref[0, t.g])
        t.s_w = _dot_t(t.qs, sw_ref[0, pl.ds(t.start, WIN_SPAN), kv_lanes(2, t.g)])
        t.s_s = [_dot_t(t.qs, sw_ref[0, j * tk:(j + 1) * tk, kv_lanes(0, t.g)]) for j in range(t.n_blk)]

    cmp_last = lax.broadcasted_iota(jnp.int32, (1, n_cmp), 1) * NSA_CMP_STRIDE + (NSA_CMP_LEN - 1)
    for t in tiles:
        mask_c = tile4(cmp_last <= t.t_row)
        sm = jnp.where(mask_c, t.s_c, NEG)
        e_c = jnp.where(mask_c, jnp.exp2(sm - jnp.max(sm, axis=1, keepdims=True)), 0.0)
        l_c = jnp.sum(e_c, axis=1, keepdims=True)
        t.p_c = e_c / jnp.where(l_c == 0.0, 1.0, l_c)
    for t in tiles:
        t.o_cmp = unstack(_dot(t.p_c.astype(_BF16), cmp_ref[0, NSA_KV_GROUPS + t.g]))
        p_sum = t.p_c[0:tq] + t.p_c[tq:2 * tq] + t.p_c[2 * tq:3 * tq] + t.p_c[3 * tq:4 * tq]
        hi, lo = _split_bf16(p_sum)
        t.imp = _dot_t(ov_ref[...], jnp.concatenate([hi, lo], axis=1))[:n_slc]

    for t in tiles:
        diff = t.t_row - (t.start + lax.broadcasted_iota(jnp.int32, (1, WIN_SPAN), 1))
        bias_w = jnp.where((diff >= 0) & (diff < NSA_WINDOW), 0.0, NEG)
        sm = t.s_w + tile4(bias_w)
        e_w = jnp.exp2(sm - jnp.max(sm, axis=1, keepdims=True))
        t.l_w = jnp.sum(e_w, axis=1, keepdims=True)
        t.e_w = e_w.astype(_BF16)
    for t in tiles:
        t.o_win = unstack(_dot(t.e_w, sw_ref[0, pl.ds(t.start, WIN_SPAN), kv_lanes(3, t.g)]) / t.l_w)

    blk = lax.broadcasted_iota(jnp.int32, (n_slc, 1), 0)
    for t in tiles:
        cur = (t.t0 + lax.broadcasted_iota(jnp.int32, (1, tq), 1)) // NSA_SLC_BLOCK
        valid = blk <= cur
        forced = (blk == 0) | (blk == cur) | (blk == cur - 1)
        val = jnp.where(valid, jnp.where(forced, SEL_FORCE, t.imp), NEG)
        rank = jnp.zeros((n_slc, tq), _F32)
        for b in range(n_slc):
            vb = val[b:b + 1, :]
            ahead = (vb > val) | ((vb == val) & (blk > b))
            rank = rank + jnp.where(ahead, 1.0, 0.0)
        sel_t = jnp.where(rank < float(min(NSA_SLC_TOPK, n_slc)), 1.0, 0.0)
        sel_t = jnp.concatenate([sel_t, jnp.zeros((LANES - n_slc, tq), _F32)], axis=0)
        t.sel = sel_t.T.astype(_BF16)

    for t in tiles:
        masked = []
        for j in range(t.n_blk):
            chosen = _dot(t.sel, ex_ref[:, j * tk:(j + 1) * tk])
            kpos = j * tk + lax.broadcasted_iota(jnp.int32, (1, tk), 1)
            bias = jnp.where((chosen > 0.5) & (kpos <= t.t_row), 0.0, NEG)
            masked.append(t.s_s[j] + tile4(bias))
            m_j = masked[j].max(axis=1, keepdims=True)
            m = m_j if j == 0 else jnp.maximum(m, m_j)
        e_s = [jnp.exp2(sb - m) for sb in masked]
        t.l_s = sum(e.sum(axis=1, keepdims=True) for e in e_s)
        t.e_s = [e.astype(_BF16) for e in e_s]
    for t in tiles:
        acc = _dot(t.e_s[0], sw_ref[0, 0:tk, kv_lanes(1, t.g)])
        for j in range(1, t.n_blk):
            acc = acc + _dot(t.e_s[j], sw_ref[0, j * tk:(j + 1) * tk, kv_lanes(1, t.g)])
        t.o_slc = unstack(acc / t.l_s)

    for t in tiles:
        g = 1.0 / (1.0 + jnp.exp(-t.g_ref[0, :, t.g * LANES:(t.g + 1) * LANES]))
        out = jnp.zeros((tq, gw), _F32)
        for branch, o_b in enumerate((t.o_cmp, t.o_slc, t.o_win)):
            gm = jnp.zeros((tq, gw), _F32)
            for r in range(NSA_REP):
                c = 3 * r + branch
                gm = jnp.where(head_of_lane == r, g[:, c:c + 1], gm)
            out = out + gm * o_b
        o_ref[0, t.half, 0, :, t.g * gw:(t.g + 1) * gw] = out.astype(o_ref.dtype)


def _nsa_attention(nq, cmp_kv, sw, gates):
    bsz, s, _ = nq.shape
    tq = NSA_TQ
    n_cmp = cmp_kv.shape[2]
    n_slc = s // NSA_SLC_BLOCK
    gw = 4 * HEAD_DIM
    cmp_start = np.arange(n_cmp) * NSA_CMP_STRIDE
    slc_start = np.arange(LANES) * NSA_SLC_BLOCK
    ov = ((cmp_start[:, None] < slc_start[None, :] + NSA_SLC_BLOCK)
          & (cmp_start[:, None] + NSA_CMP_LEN > slc_start[None, :])
          & (np.arange(LANES)[None, :] < n_slc)).astype(np.float32)
    ov[n_cmp - 1:] = 0.0
    ov2 = np.concatenate([ov.T, ov.T], axis=1)
    ex = (np.arange(LANES)[:, None] == (np.arange(s)[None, :] // NSA_SLC_BLOCK)).astype(np.float32)
    n_tiles = s // tq
    n_pair = n_tiles // 2
    return pl.pallas_call(
        _nsa_kernel,
        grid=(bsz, n_pair),
        in_specs=[pl.BlockSpec((1, tq, NSA_Q_W), lambda b, i: (b, i, 0)),
                  pl.BlockSpec((1, tq, NSA_Q_W), lambda b, i: (b, n_tiles - 1 - i, 0)),
                  pl.BlockSpec((1, 2 * NSA_KV_GROUPS, n_cmp, gw), lambda b, i: (b, 0, 0, 0)),
                  pl.BlockSpec((1, s, sw.shape[2]), lambda b, i: (b, 0, 0)),
                  pl.BlockSpec((1, tq, NSA_KV_GROUPS * LANES), lambda b, i: (b, i, 0)),
                  pl.BlockSpec((1, tq, NSA_KV_GROUPS * LANES), lambda b, i: (b, n_tiles - 1 - i, 0)),
                  pl.BlockSpec((LANES, 2 * n_cmp), lambda b, i: (0, 0)),
                  pl.BlockSpec((LANES, s), lambda b, i: (0, 0))],
        out_specs=pl.BlockSpec((1, 2, 1, tq, NSA_Q_W), lambda b, i: (b, 0, i, 0, 0)),
        out_shape=jax.ShapeDtypeStruct((bsz, 2, n_pair, tq, NSA_Q_W), _BF16),
        compiler_params=_cparams(("parallel", "arbitrary")),
        name="nsa_attention",
    )(nq, nq, cmp_kv, sw, gates, gates, jnp.asarray(ov2, _BF16), jnp.asarray(ex, _BF16))


FFN_TF = 256
L0_TM = 512


def _swiglu(xb, w1, w3, w2):
    ff = w1.shape[1]
    acc = jnp.zeros((xb.shape[0], w2.shape[1]), _F32)
    for c in range(ff // FFN_TF):
        cs = slice(c * FFN_TF, (c + 1) * FFN_TF)
        a = _silu(_dot(xb, w1[:, cs])) * _dot(xb, w3[:, cs])
        acc = acc + _dot(a.astype(_BF16), w2[cs, :])
    return acc


def _l0_tail_kernel(sb_ref, nsa_ref, wa_ref, wb_ref, x_ref, g1_ref, sc_ref, sh_ref, g2_ref, w1_ref, w3_ref, w2_ref,
                    lg1_ref, lb1_ref, lg2_ref, lb2_ref, o_ref):
    n_sub = nsa_ref.shape[2]
    tiles = [nsa_ref[0, 0, k] for k in range(n_sub)]
    upper = pl.program_id(1) >= pl.num_programs(1) // 2
    o_nsa = jnp.where(upper, jnp.concatenate(tiles[::-1], axis=0), jnp.concatenate(tiles, axis=0))
    y = _dot(sb_ref[0], wa_ref[...]) + _dot(o_nsa, wb_ref[...])
    x1 = _layer_norm(DN_ALPHA * x_ref[0] + (1.0 + g1_ref[0]) * y, lg1_ref[...], lb1_ref[...])
    h = (x1 * (1.0 + sc_ref[0]) + sh_ref[0]).astype(_BF16)
    r = DN_ALPHA * x1 + (1.0 + g2_ref[0]) * _swiglu(h, w1_ref, w3_ref, w2_ref)
    o_ref[0] = _layer_norm(r, lg2_ref[...], lb2_ref[...])


def _l0_tail(o_sb, o_nsa5, w_out, x, g1, sc2, sh2, g2, w1, w3, w2, ln_g, ln_b):
    bsz, s, d = x.shape
    ff = w1.shape[1]
    tm = L0_TM
    n_sub = tm // NSA_TQ
    per_half = s // (2 * tm)
    row = lambda b, i: (b, i, 0)
    mod = pl.BlockSpec((1, 1, d), lambda b, i: (b, 0, 0))
    vec = pl.BlockSpec((1, d), lambda b, i: (0, 0))

    def nsa_index(b, i):
        half, j = i // per_half, i % per_half
        return b, half, jnp.where(half == 0, j, per_half - 1 - j), 0, 0
    return pl.pallas_call(
        _l0_tail_kernel,
        grid=(bsz, s // tm),
        in_specs=[pl.BlockSpec((1, tm, SB_W), row),
                  pl.BlockSpec((1, 1, n_sub, NSA_TQ, NSA_Q_W), nsa_index),
                  pl.BlockSpec((SB_W, d), lambda b, i: (0, 0)),
                  pl.BlockSpec((NSA_Q_W, d), lambda b, i: (1, 0)),
                  pl.BlockSpec((1, tm, d), row), mod, mod, mod, mod,
                  pl.BlockSpec((d, ff), lambda b, i: (0, 0)),
                  pl.BlockSpec((d, ff), lambda b, i: (0, 0)),
                  pl.BlockSpec((ff, d), lambda b, i: (0, 0)), vec, vec, vec, vec],
        out_specs=pl.BlockSpec((1, tm, d), row),
        out_shape=jax.ShapeDtypeStruct((bsz, s, d), _F32),
        compiler_params=_cparams(("parallel", "parallel")),
        name="l0_outproj_ffn_ln",
    )(o_sb, o_nsa5, w_out, w_out, x, g1, sc2, sh2, g2, w1, w3, w2,
      ln_g[0].reshape(1, d), ln_b[0].reshape(1, d), ln_g[1].reshape(1, d), ln_b[1].reshape(1, d))


def _conv_kernel(x_ref, sc_ref, sh_ref, g_ref, win_ref, taps_ref, wout_ref, lg_ref, lb_ref, o_ref, tail_ref):
    ts, d = x_ref.shape[1], x_ref.shape[2]

    @pl.when(pl.program_id(1) == 0)
    def _():
        tail_ref[...] = jnp.zeros_like(tail_ref)

    x = x_ref[0]
    h = (x * (1.0 + sc_ref[0]) + sh_ref[0]).astype(_BF16)
    z = _dot(h, win_ref[:, d:2 * d]) * _dot(h, win_ref[:, 2 * d:3 * d])
    row = lax.broadcasted_iota(jnp.int32, (ts, 1), 0)
    prev1 = tail_ref[7:8, :]
    prev2 = tail_ref[6:7, :]
    z1 = jnp.where(row == 0, prev1, pltpu.roll(z, 1, 0))
    z2 = jnp.where(row == 0, prev2, jnp.where(row == 1, prev1, pltpu.roll(z, 2, 0)))
    zc = taps_ref[0:1, :] * z2 + taps_ref[1:2, :] * z1 + taps_ref[2:3, :] * z
    tail_ref[...] = z[ts - 8:ts, :]
    y = _dot((_dot(h, win_ref[:, 0:d]) * zc).astype(_BF16), wout_ref[...])
    r = DN_ALPHA * x + (1.0 + g_ref[0]) * y
    o_ref[0] = _layer_norm(r, lg_ref[...], lb_ref[...])


def _conv_ln(x, sc, sh, gate, w_in, taps, w_out, ln_g, ln_b):
    bsz, s, d = x.shape
    ts = 512
    row = lambda b, i: (b, i, 0)
    mod = pl.BlockSpec((1, 1, d), lambda b, i: (b, 0, 0))
    vec = pl.BlockSpec((1, d), lambda b, i: (0, 0))
    return pl.pallas_call(
        _conv_kernel,
        grid=(bsz, s // ts),
        in_specs=[pl.BlockSpec((1, ts, d), row), mod, mod, mod,
                  pl.BlockSpec((d, 3 * d), lambda b, i: (0, 0)),
                  pl.BlockSpec((taps.shape[0], d), lambda b, i: (0, 0)),
                  pl.BlockSpec((d, d), lambda b, i: (0, 0)), vec, vec],
        out_specs=pl.BlockSpec((1, ts, d), row),
        out_shape=jax.ShapeDtypeStruct((bsz, s, d), _F32),
        scratch_shapes=[pltpu.VMEM((8, d), _F32)],
        compiler_params=_cparams(("arbitrary", "arbitrary")),
        name="l1_conv_ln",
    )(x, sc, sh, gate, w_in, taps, w_out, ln_g.reshape(1, d), ln_b.reshape(1, d))


ROUTE_TM = 512


def _router_kernel(x_ref, sc_ref, sh_ref, rw_ref, rb_ref, tri_ref, h_ref, route_ref, route_t_ref, cnt_ref, run_ref):
    tm = ROUTE_TM

    @pl.when((pl.program_id(0) == 0) & (pl.program_id(1) == 0))
    def _():
        run_ref[...] = jnp.zeros_like(run_ref)

    h = x_ref[0] * (1.0 + sc_ref[0]) + sh_ref[0]
    h_ref[0] = h
    lane = lax.broadcasted_iota(jnp.int32, (1, LANES), 1)
    h_hi, h_lo = _split_bf16(h)
    two = _dot(h_hi, rw_ref[...])
    logits = two[:, :LANES] + two[:, LANES:] + _dot(h_lo, rw_ref[:, :LANES]) + rb_ref[...]
    logits = jnp.where(lane < N_EXPERTS, logits, -3e38)
    m1 = jnp.max(logits, axis=1, keepdims=True)
    e1 = jnp.min(jnp.where(logits == m1, lane, LANES), axis=1, keepdims=True)
    rest = jnp.where(lane == e1, -3e38, logits)
    m2 = jnp.max(rest, axis=1, keepdims=True)
    e2 = jnp.min(jnp.where(rest == m2, lane, LANES), axis=1, keepdims=True)
    ex = jnp.exp(m2 - m1)
    g1 = 1.0 / (1.0 + ex)
    g2 = ex / (1.0 + ex)
    oh1 = jnp.where(lane == e1, 1.0, 0.0)
    oh2 = jnp.where(lane == e2, 1.0, 0.0)
    oh = oh1 + oh2
    before = run_ref[...] + _dot(tri_ref[...], oh.astype(_BF16))
    p1 = jnp.sum(oh1 * before, axis=1, keepdims=True)
    p2 = jnp.sum(oh2 * before, axis=1, keepdims=True)
    run_ref[...] += jnp.sum(oh, axis=0, keepdims=True)
    cnt_ref[...] = run_ref[...]
    vals = (e1.astype(_F32), e2.astype(_F32), p1, p2, g1, g2)
    out = jnp.zeros((tm, LANES), _F32)
    for k, v in enumerate(vals):
        out = jnp.where(lane == k, v, out)
    route_ref[0] = out
    route_t_ref[0, 0] = out.T[:8, :]


def _router(x, sc, sh, rw, rb):
    bsz, s, d = x.shape
    tm = ROUTE_TM
    row = lambda b, i: (b, i, 0)
    mod = pl.BlockSpec((1, 1, d), lambda b, i: (b, 0, 0))
    tri = (np.arange(tm)[None, :] < np.arange(tm)[:, None]).astype(np.float32)
    return pl.pallas_call(
        _router_kernel,
        grid=(bsz, s // tm),
        in_specs=[pl.BlockSpec((1, tm, d), row), mod, mod,
                  pl.BlockSpec((d, 2 * LANES), lambda b, i: (0, 0)),
                  pl.BlockSpec((1, LANES), lambda b, i: (0, 0)),
                  pl.BlockSpec((tm, tm), lambda b, i: (0, 0))],
        out_specs=[pl.BlockSpec((1, tm, d), row),
                   pl.BlockSpec((1, tm, LANES), row),
                   pl.BlockSpec((1, 1, 8, tm), lambda b, i: (b, i, 0, 0)),
                   pl.BlockSpec((1, LANES), lambda b, i: (0, 0))],
        out_shape=[jax.ShapeDtypeStruct((bsz, s, d), _F32),
                   jax.ShapeDtypeStruct((bsz, s, LANES), _F32),
                   jax.ShapeDtypeStruct((bsz, s // tm, 8, tm), _F32),
                   jax.ShapeDtypeStruct((1, LANES), _F32)],
        scratch_shapes=[pltpu.VMEM((1, LANES), _F32)],
        compiler_params=_cparams(("arbitrary", "arbitrary")),
        name="l1_router",
    )(x, sc, sh, rw, rb, jnp.asarray(tri, _BF16))


def _row_tokens_kernel(d1_ref, d2_ref, lo_ref, hi_ref, o_ref):
    def clear_range(k, carry):
        def clear(r, c):
            o_ref[r] = 0
            return c
        return lax.fori_loop(lo_ref[k], hi_ref[k], clear, carry)
    lax.fori_loop(0, lo_ref.shape[0], clear_range, 0)

    def place(a, carry):
        o_ref[d1_ref[a]] = a
        o_ref[d2_ref[a]] = a
        return carry
    lax.fori_loop(0, d1_ref.shape[0], place, 0, unroll=8)


def _row_tokens(dest1, dest2, pad_lo, pad_hi, rows):
    smem = pl.BlockSpec(memory_space=pltpu.SMEM)
    return pl.pallas_call(
        _row_tokens_kernel,
        in_specs=[smem, smem, smem, smem],
        out_specs=smem,
        out_shape=jax.ShapeDtypeStruct((rows,), jnp.int32),
        name="l1_row_tokens",
    )(dest1, dest2, pad_lo, pad_hi)


GATHER_UNROLL = 8


def _row_gather(src_hbm, idx_ref, base, dst_ref, sem, n):
    def issue(r, carry):
        tok = idx_ref[base + r]
        pltpu.make_async_copy(src_hbm.at[pl.ds(tok, 1)], dst_ref.at[pl.ds(r, 1)], sem).start()
        return carry
    lax.fori_loop(0, n, issue, 0, unroll=GATHER_UNROLL)


def _row_gather_wait(src_hbm, dst_ref, sem, n):
    pltpu.make_async_copy(src_hbm.at[pl.ds(0, n)], dst_ref, sem).wait()


def _expert_kernel(blk_e_ref, n_used_ref, row_tok_ref, h_hbm, w1_ref, w3_ref, w2_ref, y_ref, xbuf, sems):
    i = pl.program_id(0)
    n_used = n_used_ref[0]
    rb = MOE_ROW_BLOCK
    slot = lax.rem(i, 2)

    def gather(block, s):
        for r in range(rb):
            pltpu.make_async_copy(h_hbm.at[pl.ds(row_tok_ref[block * rb + r], 1)], xbuf.at[s, pl.ds(r, 1)],
                                  sems.at[s]).start()

    @pl.when((i == 0) & (n_used > 0))
    def _():
        _row_gather(h_hbm, row_tok_ref, 0, xbuf.at[0], sems.at[0], rb)

    @pl.when(i + 1 < n_used)
    def _():
        gather(i + 1, 1 - slot)

    @pl.when(i < n_used)
    def _():
        _row_gather_wait(h_hbm, xbuf.at[slot], sems.at[slot], rb)
        y_ref[...] = _swiglu(xbuf[slot].astype(_BF16), w1_ref.at[0], w3_ref.at[0], w2_ref.at[0])

    @pl.when(i >= n_used)
    def _():
        y_ref[...] = jnp.zeros_like(y_ref)


def _experts(blk_e, n_used, row_tok, h_flat, w1, w3, w2):
    t, d = h_flat.shape
    n_blocks = blk_e.shape[0]
    rb = MOE_ROW_BLOCK
    ff = w1.shape[2]
    grid_spec = pltpu.PrefetchScalarGridSpec(
        num_scalar_prefetch=3,
        grid=(n_blocks,),
        in_specs=[pl.BlockSpec(memory_space=pl.ANY),
                  pl.BlockSpec((1, d, ff), lambda i, be, nu, rt: (be[i], 0, 0)),
                  pl.BlockSpec((1, d, ff), lambda i, be, nu, rt: (be[i], 0, 0)),
                  pl.BlockSpec((1, ff, d), lambda i, be, nu, rt: (be[i], 0, 0))],
        out_specs=pl.BlockSpec((rb, d), lambda i, be, nu, rt: (i, 0)),
        scratch_shapes=[pltpu.VMEM((2, rb, d), _F32), pltpu.SemaphoreType.DMA((2,))],
    )
    return pl.pallas_call(
        _expert_kernel,
        grid_spec=grid_spec,
        out_shape=jax.ShapeDtypeStruct((n_blocks * rb, d), _F32),
        compiler_params=_cparams(("arbitrary",)),
        name="l1_experts",
    )(blk_e, n_used, row_tok, h_flat, w1, w3, w2)


COMB_TM = 256


def _combine_kernel(d1_ref, d2_ref, y_hbm, x_ref, route_ref, g_ref, lg_ref, lb_ref, o_ref, y1buf, y2buf, sems):
    tm = COMB_TM
    i = pl.program_id(0) * pl.num_programs(1) + pl.program_id(1)
    n = pl.num_programs(0) * pl.num_programs(1)
    slot = lax.rem(i, 2)

    def gather(tile, s):
        for r in range(tm):
            pltpu.make_async_copy(y_hbm.at[pl.ds(d1_ref[tile * tm + r], 1)], y1buf.at[s, pl.ds(r, 1)],
                                  sems.at[0, s]).start(priority=0)
            pltpu.make_async_copy(y_hbm.at[pl.ds(d2_ref[tile * tm + r], 1)], y2buf.at[s, pl.ds(r, 1)],
                                  sems.at[1, s]).start(priority=1)

    @pl.when(i == 0)
    def _():
        gather(0, 0)

    @pl.when(i + 1 < n)
    def _():
        gather(i + 1, 1 - slot)

    _row_gather_wait(y_hbm, y1buf.at[slot], sems.at[0, slot], tm)
    _row_gather_wait(y_hbm, y2buf.at[slot], sems.at[1, slot], tm)
    route = route_ref[0]
    y = route[:, 4:5] * y1buf[slot] + route[:, 5:6] * y2buf[slot]
    r = DN_ALPHA * x_ref[0] + (1.0 + g_ref[0]) * y
    o_ref[0] = _layer_norm(r, lg_ref[...], lb_ref[...])


def _combine_ln(dest1, dest2, y, x, route, gate, ln_g, ln_b):
    bsz, s, d = x.shape
    tm = COMB_TM
    row = lambda b, i, d1, d2: (b, i, 0)
    vec = pl.BlockSpec((1, d), lambda b, i, d1, d2: (0, 0))
    grid_spec = pltpu.PrefetchScalarGridSpec(
        num_scalar_prefetch=2,
        grid=(bsz, s // tm),
        in_specs=[pl.BlockSpec(memory_space=pl.ANY),
                  pl.BlockSpec((1, tm, d), row),
                  pl.BlockSpec((1, tm, LANES), row),
                  pl.BlockSpec((1, 1, d), lambda b, i, d1, d2: (b, 0, 0)), vec, vec],
        out_specs=pl.BlockSpec((1, tm, d), row),
        scratch_shapes=[pltpu.VMEM((2, tm, d), _F32), pltpu.VMEM((2, tm, d), _F32),
                        pltpu.SemaphoreType.DMA((2, 2))],
    )
    return pl.pallas_call(
        _combine_kernel,
        grid_spec=grid_spec,
        out_shape=jax.ShapeDtypeStruct((bsz, s, d), _F32),
        compiler_params=_cparams(("arbitrary", "arbitrary")),
        name="l1_combine_ln",
    )(dest1, dest2, y, x, route, gate, ln_g.reshape(1, d), ln_b.reshape(1, d))


def _arrange_inproj_weight(w):
    d = w.shape[0]
    kv0 = 3 * SB_W + NSA_Q_W
    g0 = kv0 + 6 * NSA_KV_GROUPS * HEAD_DIM
    per_group = NSA_REP * 3
    pad = jnp.zeros((d, LANES - per_group), w.dtype)
    cols = [w[:, :SB_W] * QK_SCALE, w[:, SB_W:3 * SB_W], w[:, 3 * SB_W:kv0] * (QK_SCALE * LOG2E), w[:, kv0:g0]]
    for g in range(NSA_KV_GROUPS):
        cols += [w[:, g0 + g * per_group:g0 + (g + 1) * per_group], pad]
    return jnp.concatenate(cols, axis=1).astype(_BF16)


def _mods(mod, i, bsz, d):
    m = mod[i].reshape(bsz, 6, 1, d)
    return [m[:, k] for k in range(6)]


def kernel(x, c, ada_w, ada_b, ln_g, ln_b, mix_w_in, cmp_pos, cmp_w1, cmp_w2, mix_w_out, ffn_w1, ffn_w3, ffn_w2,
           conv_w_in, conv_taps, conv_w_out, router_w, router_b, exp_w1, exp_w3, exp_w2):
    bsz, s, d = x.shape
    t = bsz * s
    mod = _ada_mod(c, ada_w, ada_b)

    sh1, sc1, g1, sh2, sc2, g2 = _mods(mod, 0, bsz, d)
    sb, nq, kvc, sw, gates = _inproj(x, sc1, sh1, _arrange_inproj_weight(mix_w_in[0]))
    o_sb = _sb_attention(sb)
    pos = cmp_pos[0].reshape(2, 2, NSA_CMP_STRIDE * HEAD_DIM)
    w2rep = jnp.tile(cmp_w2[0], (1, 1, NSA_REP)).astype(_BF16)
    cmp_kv = _compress(kvc, pos, cmp_w1[0].astype(_BF16), w2rep)
    o_nsa = _nsa_attention(nq, cmp_kv, sw, gates)
    x = _l0_tail(o_sb, o_nsa, mix_w_out[0].astype(_BF16), x, g1, sc2, sh2, g2, ffn_w1[0].astype(_BF16),
                 ffn_w3[0].astype(_BF16), ffn_w2[0].astype(_BF16), ln_g[0], ln_b[0])

    sh1, sc1, g1, sh2, sc2, g2 = _mods(mod, 1, bsz, d)
    x = _conv_ln(x, sc1, sh1, g1, conv_w_in[0].astype(_BF16), conv_taps[0], conv_w_out[0].astype(_BF16),
                 ln_g[1, 0], ln_b[1, 0])
    rw = jnp.concatenate(_split_bf16(jnp.pad(router_w[0], ((0, 0), (0, LANES - N_EXPERTS)))), axis=1)
    rb = jnp.pad(router_b[0], (0, LANES - N_EXPERTS)).reshape(1, LANES)
    h2, route, route_t, counts = _router(x, sc2, sh2, rw, rb)

    field = lambda k: route_t[:, :, k, :].reshape(t).astype(jnp.int32)
    e1, e2, p1, p2 = field(0), field(1), field(2), field(3)
    cnt = counts[0, :N_EXPERTS].astype(jnp.int32)
    rb_rows = MOE_ROW_BLOCK
    padded = (cnt + rb_rows - 1) // rb_rows * rb_rows
    pends = jnp.cumsum(padded)
    pstarts = pends - padded
    dest1 = pstarts[e1] + p1
    dest2 = pstarts[e2] + p2
    n_blocks = -(-(t * 2) // rb_rows) + N_EXPERTS
    rows = n_blocks * rb_rows
    pad_lo = jnp.concatenate([pstarts + cnt, pends[-1:]]).astype(jnp.int32)
    pad_hi = jnp.concatenate([pends, jnp.full((1,), rows, jnp.int32)]).astype(jnp.int32)
    row_tok = _row_tokens(dest1, dest2, pad_lo, pad_hi, rows)
    blk_start = jnp.arange(n_blocks, dtype=jnp.int32) * rb_rows
    blk_e = jnp.minimum(jnp.sum(pends[None, :] <= blk_start[:, None], axis=1), N_EXPERTS - 1).astype(jnp.int32)
    n_used = (pends[-1:] // rb_rows).astype(jnp.int32)

    y = _experts(blk_e, n_used, row_tok, h2.reshape(t, d), exp_w1[0].astype(_BF16), exp_w3[0].astype(_BF16),
                 exp_w2[0].astype(_BF16))
    return _combine_ln(dest1, dest2, y, x, route, g2, ln_g[1, 1], ln_b[1, 1])
```

```python
import functools

import numpy as np
import jax
import jax.numpy as jnp
from jax import lax
from jax.experimental import pallas as pl
from jax.experimental.pallas import tpu as pltpu

HEAD_DIM = 64
SB_HEADS = 8
NSA_HEADS = 8
NSA_KV_GROUPS = 2
NSA_REP = NSA_HEADS // NSA_KV_GROUPS
NSA_CMP_LEN = 32
NSA_CMP_STRIDE = 16
NSA_SLC_BLOCK = 64
NSA_SLC_TOPK = 16
NSA_WINDOW = 512
SB_W = SB_HEADS * HEAD_DIM
NSA_Q_W = NSA_HEADS * HEAD_DIM
N_EXPERTS = 8
MOE_ROW_BLOCK = 512
DEPTH = 2
LN_EPS = 1e-5
DN_ALPHA = (2 * DEPTH) ** 0.25
NEG = -1e30
SEL_FORCE = 1e30
QK_SCALE = HEAD_DIM ** -0.5
LOG2E = 1.4426950408889634

LANES = 128
VMEM_LIMIT = 56 * 1024 * 1024

C_SB = 0
C_NQ = 3 * SB_W
C_CMP = C_NQ + NSA_Q_W
C_SW = C_CMP + 4 * HEAD_DIM
C_GATE = C_SW + 4 * 2 * HEAD_DIM
C_END = C_GATE + 2 * LANES

_F32 = jnp.float32
_BF16 = jnp.bfloat16


def _cparams(sem, vmem=VMEM_LIMIT):
    return pltpu.CompilerParams(dimension_semantics=sem, vmem_limit_bytes=vmem)


def _dot(a, b):
    return jnp.dot(a, b, preferred_element_type=_F32)


def _dot_t(a, b):
    return lax.dot_general(a, b, (((1,), (1,)), ((), ())), preferred_element_type=_F32)


def _split_bf16(x):
    hi = x.astype(_BF16)
    lo = (x - hi.astype(_F32)).astype(_BF16)
    return hi, lo


def _silu(x):
    return x / (1.0 + jnp.exp(-x))


def _layer_norm(r, g, b):
    mu = jnp.mean(r, axis=-1, keepdims=True)
    d = r - mu
    var = jnp.mean(d * d, axis=-1, keepdims=True)
    return d * lax.rsqrt(var + LN_EPS) * g + b


def _ada_kernel(c_ref, w_ref, b_ref, o_ref):
    cond = _silu(c_ref[...])
    o_ref[0] = jnp.dot(cond, w_ref[0], preferred_element_type=_F32,
                       precision=lax.Precision.HIGHEST) + b_ref[0]


def _ada_mod(c, ada_w, ada_b):
    depth, d, n = ada_w.shape
    bsz = c.shape[0]
    tn = n // 6
    return pl.pallas_call(
        _ada_kernel,
        grid=(depth, n // tn),
        in_specs=[pl.BlockSpec((bsz, d), lambda i, j: (0, 0)),
                  pl.BlockSpec((1, d, tn), lambda i, j: (i, 0, j)),
                  pl.BlockSpec((1, 1, tn), lambda i, j: (i, 0, j))],
        out_specs=pl.BlockSpec((1, bsz, tn), lambda i, j: (i, 0, j)),
        out_shape=jax.ShapeDtypeStruct((depth, bsz, n), _F32),
        compiler_params=_cparams(("parallel", "parallel")),
        name="ada_mod",
    )(c, ada_w, ada_b.reshape(depth, 1, n))


def _inproj_kernel(x_ref, sc_ref, sh_ref, w_ref, sb_ref, nq_ref, cmp_ref, sw_ref, gate_ref):
    h = (x_ref[0] * (1.0 + sc_ref[0]) + sh_ref[0]).astype(_BF16)
    sb_ref[0] = _dot(h, w_ref[:, C_SB:C_NQ]).astype(_BF16)
    nq_ref[0] = _dot(h, w_ref[:, C_NQ:C_CMP]).astype(_BF16)
    kvc = _dot(h, w_ref[:, C_CMP:C_SW])
    for j in range(4):
        cmp_ref[0, j] = kvc[:, j * HEAD_DIM:(j + 1) * HEAD_DIM]
    lane = lax.broadcasted_iota(jnp.int32, (1, LANES), 1)
    first = lane < HEAD_DIM
    for kind in range(4):
        a = _dot(h, w_ref[:, C_SW + kind * LANES:C_SW + (kind + 1) * LANES])
        r = pltpu.roll(a, HEAD_DIM, 1)
        g0 = jnp.where(first, a, r).astype(_BF16)
        g1 = jnp.where(first, r, a).astype(_BF16)
        sw_ref[0, :, (2 * kind) * 256:(2 * kind + 1) * 256] = jnp.concatenate([g0, g0], axis=1)
        sw_ref[0, :, (2 * kind + 1) * 256:(2 * kind + 2) * 256] = jnp.concatenate([g1, g1], axis=1)
    gate_ref[0] = _dot(h, w_ref[:, C_GATE:C_END])


def _inproj(x, sc, sh, w):
    bsz, s, d = x.shape
    tm = 512
    row = lambda b, i: (b, i, 0)
    return pl.pallas_call(
        _inproj_kernel,
        grid=(bsz, s // tm),
        in_specs=[pl.BlockSpec((1, tm, d), row),
                  pl.BlockSpec((1, 1, d), lambda b, i: (b, 0, 0)),
                  pl.BlockSpec((1, 1, d), lambda b, i: (b, 0, 0)),
                  pl.BlockSpec((d, C_END), lambda b, i: (0, 0))],
        out_specs=[pl.BlockSpec((1, tm, 3 * SB_W), row),
                   pl.BlockSpec((1, tm, NSA_Q_W), row),
                   pl.BlockSpec((1, 4, tm, HEAD_DIM), lambda b, i: (b, 0, i, 0)),
                   pl.BlockSpec((1, tm, 2048), row),
                   pl.BlockSpec((1, tm, 2 * LANES), row)],
        out_shape=[jax.ShapeDtypeStruct((bsz, s, 3 * SB_W), _BF16),
                   jax.ShapeDtypeStruct((bsz, s, NSA_Q_W), _BF16),
                   jax.ShapeDtypeStruct((bsz, 4, s, HEAD_DIM), _F32),
                   jax.ShapeDtypeStruct((bsz, s, 2048), _BF16),
                   jax.ShapeDtypeStruct((bsz, s, 2 * LANES), _F32)],
        compiler_params=_cparams(("parallel", "parallel")),
        name="l0_inproj",
    )(x, sc, sh, w)


SB_TQ = 1024
SB_TK = 256
SB_UNROLL = 2
SB_PAIRS = 1


def _sb_kernel(q_ref, k_ref, v_ref, tri_ref, o_ref, acc_ref, run_ref):
    tq, tk = SB_TQ, SB_TK
    qi = pl.program_id(2)
    lane = lax.broadcasted_iota(jnp.int32, (1, LANES), 1)
    first = lane < HEAD_DIM
    pairs = range(SB_PAIRS)
    lanes = [slice(p * LANES, (p + 1) * LANES) for p in pairs]
    q2 = []
    for p in pairs:
        q = q_ref[0, :, lanes[p]]
        zero = jnp.zeros_like(q)
        q2.append(jnp.concatenate([jnp.where(first, q, zero), jnp.where(first, zero, q)], axis=0))
    tri = tri_ref[...]
    acc_ref[...] = jnp.zeros_like(acc_ref)
    run_ref[...] = jnp.zeros_like(run_ref)

    def scores(qrows, key_off, p):
        return _dot_t(qrows, k_ref[0, pl.ds(key_off, tk), lanes[p]])

    def suffix(z, causal):
        nlf = jnp.maximum(z, 0.0) + jnp.log(1.0 + jnp.exp2(jnp.abs(z) * -LOG2E))
        if causal is not None:
            nlf = jnp.where(causal, nlf, 0.0)
        return z - nlf, _dot(nlf.astype(_BF16), tri), nlf[:, 0:1]

    def weigh(lb, ext, nlf0, run, causal, key_off, p):
        w = jnp.exp(lb + ext + jnp.concatenate([run] * (tk // LANES), axis=1))
        if causal is not None:
            w = jnp.where(causal, w, 0.0)
        return (_dot(w.astype(_BF16), v_ref[0, pl.ds(key_off, tk), lanes[p]]),
                run + jnp.broadcast_to(ext[:, 0:1] - nlf0, run.shape))

    n_diag = tq // tk
    diag = []
    for kd in reversed(range(n_diag)):
        r0 = kd * tk
        key_off = pl.multiple_of(qi * tq + r0, tk)
        row = r0 + lax.broadcasted_iota(jnp.int32, (tq - r0, tk), 0)
        col = r0 + lax.broadcasted_iota(jnp.int32, (tq - r0, tk), 1)
        causal = jnp.concatenate([col < row] * 2, axis=0)
        for p in pairs:
            z = scores(jnp.concatenate([q2[p][r0:tq], q2[p][tq + r0:2 * tq]], axis=0), key_off, p)
            diag.append((p, r0, key_off, causal, z))
    diag = [(p, r0, key_off, causal, suffix(z, causal)) for p, r0, key_off, causal, z in diag]
    for p, r0, key_off, causal, parts in diag:
        n = tq - r0
        base = p * 2 * tq
        run = jnp.concatenate([run_ref[base + r0:base + tq, :], run_ref[base + tq + r0:base + 2 * tq, :]], axis=0)
        pv, run = weigh(*parts, run, causal, key_off, p)
        for h in range(2):
            rows = slice(base + h * tq + r0, base + (h + 1) * tq)
            acc_ref[rows, :] += pv[h * n:(h + 1) * n]
            run_ref[rows, :] = run[h * n:(h + 1) * n]

    def below(jj, carry):
        offs = [pl.multiple_of((qi * n_diag - 1 - SB_UNROLL * jj - u) * tk, tk) for u in range(SB_UNROLL)]
        zs = [[scores(q2[p], off, p) for off in offs] for p in pairs]
        parts = [[suffix(z, None) for z in zs[p]] for p in pairs]
        for p in pairs:
            rows = slice(p * 2 * tq, (p + 1) * 2 * tq)
            run = run_ref[rows, :]
            acc = acc_ref[rows, :]
            for part, off in zip(parts[p], offs):
                pv, run = weigh(*part, run, None, off, p)
                acc = acc + pv
            acc_ref[rows, :] = acc
            run_ref[rows, :] = run
        return carry
    lax.fori_loop(0, qi * (n_diag // SB_UNROLL), below, 0)
    for p in pairs:
        base = p * 2 * tq
        o_ref[0, :, lanes[p]] = jnp.where(first, acc_ref[base:base + tq, :],
                                          acc_ref[base + tq:base + 2 * tq, :]).astype(o_ref.dtype)


def _sb_attention(sb):
    bsz, s, _ = sb.shape
    tq, tk = SB_TQ, SB_TK
    width = SB_PAIRS * LANES
    n_step = SB_W // width
    tri = -(np.arange(tk)[:, None] > np.arange(tk)[None, :]).astype(np.float32)
    return pl.pallas_call(
        _sb_kernel,
        grid=(bsz, n_step, s // tq),
        in_specs=[pl.BlockSpec((1, tq, width), lambda b, p, i: (b, i, p)),
                  pl.BlockSpec((1, s, width), lambda b, p, i: (b, 0, n_step + p)),
                  pl.BlockSpec((1, s, width), lambda b, p, i: (b, 0, 2 * n_step + p)),
                  pl.BlockSpec((tk, tk), lambda b, p, i: (0, 0))],
        out_specs=pl.BlockSpec((1, tq, width), lambda b, p, i: (b, i, p)),
        out_shape=jax.ShapeDtypeStruct((bsz, s, SB_W), _BF16),
        scratch_shapes=[pltpu.VMEM((SB_PAIRS * 2 * tq, LANES), _F32), pltpu.VMEM((SB_PAIRS * 2 * tq, LANES), _F32)],
        compiler_params=_cparams(("parallel", "parallel", "arbitrary")),
        name="sb_attention",
    )(sb, sb, sb, jnp.asarray(tri, _BF16))


def _compress_kernel(x_ref, pos_ref, w1_ref, w2_ref, o_ref):
    half = w1_ref.shape[1] // 2
    n_chunk = x_ref.shape[2] // NSA_CMP_STRIDE
    ch = jnp.concatenate([x_ref[0, 0, pl.ds(l, n_chunk, stride=NSA_CMP_STRIDE), :]
                          for l in range(NSA_CMP_STRIDE)], axis=1)
    top = _dot((ch + pos_ref[0, 0:1, :]).astype(_BF16), w1_ref[0, :half, :])
    bot = _dot((ch + pos_ref[0, 1:2, :]).astype(_BF16), w1_ref[0, half:, :])
    n = ch.shape[0]
    pre = top + pltpu.roll(bot, n - 1, 0)
    o_ref[0, 0] = _dot(_silu(pre).astype(_BF16), w2_ref[0]).astype(o_ref.dtype)


def _compress(kvc, pos, w1, w2rep):
    bsz, _, s, dh = kvc.shape
    n_chunk = s // NSA_CMP_STRIDE
    cw = NSA_CMP_STRIDE * dh
    hid = w1.shape[-1]
    return pl.pallas_call(
        _compress_kernel,
        grid=(bsz, 4),
        in_specs=[pl.BlockSpec((1, 1, s, dh), lambda b, j: (b, j, 0, 0)),
                  pl.BlockSpec((1, 2, cw), lambda b, j: (j // 2, 0, 0)),
                  pl.BlockSpec((1, 2 * cw, hid), lambda b, j: (j // 2, 0, 0)),
                  pl.BlockSpec((1, hid, 4 * dh), lambda b, j: (j // 2, 0, 0))],
        out_specs=pl.BlockSpec((1, 1, n_chunk, 4 * dh), lambda b, j: (b, j, 0, 0)),
        out_shape=jax.ShapeDtypeStruct((bsz, 4, n_chunk, 4 * dh), _BF16),
        compiler_params=_cparams(("parallel", "parallel")),
        name="nsa_compress",
    )(kvc, pos, w1, w2rep)


NSA_TQ = 128
NSA_TK = 512
WIN_SPAN = NSA_WINDOW + NSA_TQ


def _nsa_kernel(*refs):
    s_len = refs[3].shape[1]
    tiles_per_blk = NSA_TK // NSA_TQ
    n_blk = s_len // NSA_TK
    i = pl.program_id(1)
    for v in range(n_blk // 2):
        pl.when(i // tiles_per_blk == v)(functools.partial(_nsa_body, v + 1, n_blk - v, *refs))


def _nsa_body(n_blk_a, n_blk_b, qa_ref, qb_ref, cmp_ref, sw_ref, ga_ref, gb_ref, ov_ref, ex_ref, o_ref):
    tq, tk = NSA_TQ, NSA_TK
    s_len = sw_ref.shape[1]
    n_tiles = s_len // tq
    n_cmp = cmp_ref.shape[2]
    n_slc = s_len // NSA_SLC_BLOCK
    gw = NSA_REP * HEAD_DIM
    i = pl.program_id(1)
    head_of_lane = lax.broadcasted_iota(jnp.int32, (1, gw), 1) // HEAD_DIM

    def kv_lanes(kind, g):
        return slice((2 * kind + g) * gw, (2 * kind + g + 1) * gw)

    def unstack(o):
        out = jnp.zeros((tq, gw), _F32)
        for r in range(NSA_REP):
            out = jnp.where(head_of_lane == r, o[r * tq:(r + 1) * tq], out)
        return out

    def tile4(x):
        return jnp.concatenate([x] * NSA_REP, axis=0)

    class Tile:
        pass
    tiles = []
    for half, (q_ref, g_ref, t0, n_blk) in enumerate(((qa_ref, ga_ref, i * tq, n_blk_a),
                                                      (qb_ref, gb_ref, (n_tiles - 1 - i) * tq, n_blk_b))):
        for g in range(NSA_KV_GROUPS):
            t = Tile()
            q = q_ref[0, :, g * gw:(g + 1) * gw]
            zero = jnp.zeros_like(q)
            t.qs = jnp.concatenate([jnp.where(head_of_lane == r, q, zero) for r in range(NSA_REP)], axis=0)
            t.half, t.g, t.t0, t.n_blk, t.g_ref = half, g, t0, n_blk, g_ref
            t.t_row = t0 + lax.broadcasted_iota(jnp.int32, (tq, 1), 0)
            t.start = pl.multiple_of(jnp.maximum(t0 - NSA_WINDOW, 0), tq)
            tiles.append(t)

    for t in tiles:
        t.s_c = _dot_t(t.qs, cmp_ref[0, t.g])
        t.s_w = _dot_t(t.qs, sw_ref[0, pl.ds(t.start, WIN_SPAN), kv_lanes(2, t.g)])
        t.s_s = [_dot_t(t.qs, sw_ref[0, j * tk:(j + 1) * tk, kv_lanes(0, t.g)]) for j in range(t.n_blk)]

    cmp_last = lax.broadcasted_iota(jnp.int32, (1, n_cmp), 1) * NSA_CMP_STRIDE + (NSA_CMP_LEN - 1)
    for t in tiles:
        mask_c = tile4(cmp_last <= t.t_row)
        sm = jnp.where(mask_c, t.s_c, NEG)
        e_c = jnp.where(mask_c, jnp.exp2(sm - jnp.max(sm, axis=1, keepdims=True)), 0.0)
        l_c = jnp.sum(e_c, axis=1, keepdims=True)
        t.p_c = e_c / jnp.where(l_c == 0.0, 1.0, l_c)
    for t in tiles:
        t.o_cmp = unstack(_dot(t.p_c.astype(_BF16), cmp_ref[0, NSA_KV_GROUPS + t.g]))
        p_sum = t.p_c[0:tq] + t.p_c[tq:2 * tq] + t.p_c[2 * tq:3 * tq] + t.p_c[3 * tq:4 * tq]
        hi, lo = _split_bf16(p_sum)
        t.imp = _dot_t(ov_ref[...], jnp.concatenate([hi, lo], axis=1))[:n_slc]

    for t in tiles:
        diff = t.t_row - (t.start + lax.broadcasted_iota(jnp.int32, (1, WIN_SPAN), 1))
        bias_w = jnp.where((diff >= 0) & (diff < NSA_WINDOW), 0.0, NEG)
        sm = t.s_w + tile4(bias_w)
        e_w = jnp.exp2(sm - jnp.max(sm, axis=1, keepdims=True))
        t.l_w = jnp.sum(e_w, axis=1, keepdims=True)
        t.e_w = e_w.astype(_BF16)
    for t in tiles:
        t.o_win = unstack(_dot(t.e_w, sw_ref[0, pl.ds(t.start, WIN_SPAN), kv_lanes(3, t.g)]) / t.l_w)

    blk = lax.broadcasted_iota(jnp.int32, (n_slc, 1), 0)
    for t in tiles:
        cur = (t.t0 + lax.broadcasted_iota(jnp.int32, (1, tq), 1)) // NSA_SLC_BLOCK
        valid = blk <= cur
        forced = (blk == 0) | (blk == cur) | (blk == cur - 1)
        val = jnp.where(valid, jnp.where(forced, SEL_FORCE, t.imp), NEG)
        rank = jnp.zeros((n_slc, tq), _F32)
        for b in range(n_slc):
            vb = val[b:b + 1, :]
            ahead = (vb > val) | ((vb == val) & (blk > b))
            rank = rank + jnp.where(ahead, 1.0, 0.0)
        sel_t = jnp.where(rank < float(min(NSA_SLC_TOPK, n_slc)), 1.0, 0.0)
        sel_t = jnp.concatenate([sel_t, jnp.zeros((LANES - n_slc, tq), _F32)], axis=0)
        t.sel = sel_t.T.astype(_BF16)

    for t in tiles:
        masked = []
        for j in range(t.n_blk):
            chosen = _dot(t.sel, ex_ref[:, j * tk:(j + 1) * tk])
            kpos = j * tk + lax.broadcasted_iota(jnp.int32, (1, tk), 1)
            bias = jnp.where((chosen > 0.5) & (kpos <= t.t_row), 0.0, NEG)
            masked.append(t.s_s[j] + tile4(bias))
            m_j = masked[j].max(axis=1, keepdims=True)
            m = m_j if j == 0 else jnp.maximum(m, m_j)
        e_s = [jnp.exp2(sb - m) for sb in masked]
        t.l_s = sum(e.sum(axis=1, keepdims=True) for e in e_s)
        t.e_s = [e.astype(_BF16) for e in e_s]
    for t in tiles:
        acc = _dot(t.e_s[0], sw_ref[0, 0:tk, kv_lanes(1, t.g)])
        for j in range(1, t.n_blk):
            acc = acc + _dot(t.e_s[j], sw_ref[0, j * tk:(j + 1) * tk, kv_lanes(1, t.g)])
        t.o_slc = unstack(acc / t.l_s)

    for t in tiles:
        g = 1.0 / (1.0 + jnp.exp(-t.g_ref[0, :, t.g * LANES:(t.g + 1) * LANES]))
        out = jnp.zeros((tq, gw), _F32)
        for branch, o_b in enumerate((t.o_cmp, t.o_slc, t.o_win)):
            gm = jnp.zeros((tq, gw), _F32)
            for r in range(NSA_REP):
                c = 3 * r + branch
                gm = jnp.where(head_of_lane == r, g[:, c:c + 1], gm)
            out = out + gm * o_b
        o_ref[0, t.half, 0, :, t.g * gw:(t.g + 1) * gw] = out.astype(o_ref.dtype)


def _nsa_attention(nq, cmp_kv, sw, gates):
    bsz, s, _ = nq.shape
    tq = NSA_TQ
    n_cmp = cmp_kv.shape[2]
    n_slc = s // NSA_SLC_BLOCK
    gw = 4 * HEAD_DIM
    cmp_start = np.arange(n_cmp) * NSA_CMP_STRIDE
    slc_start = np.arange(LANES) * NSA_SLC_BLOCK
    ov = ((cmp_start[:, None] < slc_start[None, :] + NSA_SLC_BLOCK)
          & (cmp_start[:, None] + NSA_CMP_LEN > slc_start[None, :])
          & (np.arange(LANES)[None, :] < n_slc)).astype(np.float32)
    ov[n_cmp - 1:] = 0.0
    ov2 = np.concatenate([ov.T, ov.T], axis=1)
    ex = (np.arange(LANES)[:, None] == (np.arange(s)[None, :] // NSA_SLC_BLOCK)).astype(np.float32)
    n_tiles = s // tq
    n_pair = n_tiles // 2
    return pl.pallas_call(
        _nsa_kernel,
        grid=(bsz, n_pair),
        in_specs=[pl.BlockSpec((1, tq, NSA_Q_W), lambda b, i: (b, i, 0)),
                  pl.BlockSpec((1, tq, NSA_Q_W), lambda b, i: (b, n_tiles - 1 - i, 0)),
                  pl.BlockSpec((1, 2 * NSA_KV_GROUPS, n_cmp, gw), lambda b, i: (b, 0, 0, 0)),
                  pl.BlockSpec((1, s, sw.shape[2]), lambda b, i: (b, 0, 0)),
                  pl.BlockSpec((1, tq, NSA_KV_GROUPS * LANES), lambda b, i: (b, i, 0)),
                  pl.BlockSpec((1, tq, NSA_KV_GROUPS * LANES), lambda b, i: (b, n_tiles - 1 - i, 0)),
                  pl.BlockSpec((LANES, 2 * n_cmp), lambda b, i: (0, 0)),
                  pl.BlockSpec((LANES, s), lambda b, i: (0, 0))],
        out_specs=pl.BlockSpec((1, 2, 1, tq, NSA_Q_W), lambda b, i: (b, 0, i, 0, 0)),
        out_shape=jax.ShapeDtypeStruct((bsz, 2, n_pair, tq, NSA_Q_W), _BF16),
        compiler_params=_cparams(("parallel", "arbitrary")),
        name="nsa_attention",
    )(nq, nq, cmp_kv, sw, gates, gates, jnp.asarray(ov2, _BF16), jnp.asarray(ex, _BF16))


FFN_TF = 256
L0_TM = 512


def _swiglu(xb, w1, w3, w2):
    ff = w1.shape[1]
    acc = jnp.zeros((xb.shape[0], w2.shape[1]), _F32)
    for c in range(ff // FFN_TF):
        cs = slice(c * FFN_TF, (c + 1) * FFN_TF)
        a = _silu(_dot(xb, w1[:, cs])) * _dot(xb, w3[:, cs])
        acc = acc + _dot(a.astype(_BF16), w2[cs, :])
    return acc


def _l0_tail_kernel(sb_ref, nsa_ref, wa_ref, wb_ref, x_ref, g1_ref, sc_ref, sh_ref, g2_ref, w1_ref, w3_ref, w2_ref,
                    lg1_ref, lb1_ref, lg2_ref, lb2_ref, o_ref):
    n_sub = nsa_ref.shape[2]
    tiles = [nsa_ref[0, 0, k] for k in range(n_sub)]
    upper = pl.program_id(1) >= pl.num_programs(1) // 2
    o_nsa = jnp.where(upper, jnp.concatenate(tiles[::-1], axis=0), jnp.concatenate(tiles, axis=0))
    y = _dot(sb_ref[0], wa_ref[...]) + _dot(o_nsa, wb_ref[...])
    x1 = _layer_norm(DN_ALPHA * x_ref[0] + (1.0 + g1_ref[0]) * y, lg1_ref[...], lb1_ref[...])
    h = (x1 * (1.0 + sc_ref[0]) + sh_ref[0]).astype(_BF16)
    r = DN_ALPHA * x1 + (1.0 + g2_ref[0]) * _swiglu(h, w1_ref, w3_ref, w2_ref)
    o_ref[0] = _layer_norm(r, lg2_ref[...], lb2_ref[...])


def _l0_tail(o_sb, o_nsa5, w_out, x, g1, sc2, sh2, g2, w1, w3, w2, ln_g, ln_b):
    bsz, s, d = x.shape
    ff = w1.shape[1]
    tm = L0_TM
    n_sub = tm // NSA_TQ
    per_half = s // (2 * tm)
    row = lambda b, i: (b, i, 0)
    mod = pl.BlockSpec((1, 1, d), lambda b, i: (b, 0, 0))
    vec = pl.BlockSpec((1, d), lambda b, i: (0, 0))

    def nsa_index(b, i):
        half, j = i // per_half, i % per_half
        return b, half, jnp.where(half == 0, j, per_half - 1 - j), 0, 0
    return pl.pallas_call(
        _l0_tail_kernel,
        grid=(bsz, s // tm),
        in_specs=[pl.BlockSpec((1, tm, SB_W), row),
                  pl.BlockSpec((1, 1, n_sub, NSA_TQ, NSA_Q_W), nsa_index),
                  pl.BlockSpec((SB_W, d), lambda b, i: (0, 0)),
                  pl.BlockSpec((NSA_Q_W, d), lambda b, i: (1, 0)),
                  pl.BlockSpec((1, tm, d), row), mod, mod, mod, mod,
                  pl.BlockSpec((d, ff), lambda b, i: (0, 0)),
                  pl.BlockSpec((d, ff), lambda b, i: (0, 0)),
                  pl.BlockSpec((ff, d), lambda b, i: (0, 0)), vec, vec, vec, vec],
        out_specs=pl.BlockSpec((1, tm, d), row),
        out_shape=jax.ShapeDtypeStruct((bsz, s, d), _F32),
        compiler_params=_cparams(("parallel", "parallel")),
        name="l0_outproj_ffn_ln",
    )(o_sb, o_nsa5, w_out, w_out, x, g1, sc2, sh2, g2, w1, w3, w2,
      ln_g[0].reshape(1, d), ln_b[0].reshape(1, d), ln_g[1].reshape(1, d), ln_b[1].reshape(1, d))


def _conv_kernel(x_ref, sc_ref, sh_ref, g_ref, win_ref, taps_ref, wout_ref, lg_ref, lb_ref, o_ref, tail_ref):
    ts, d = x_ref.shape[1], x_ref.shape[2]

    @pl.when(pl.program_id(1) == 0)
    def _():
        tail_ref[...] = jnp.zeros_like(tail_ref)

    x = x_ref[0]
    h = (x * (1.0 + sc_ref[0]) + sh_ref[0]).astype(_BF16)
    z = _dot(h, win_ref[:, d:2 * d]) * _dot(h, win_ref[:, 2 * d:3 * d])
    gate_b = _dot(h, win_ref[:, 0:d])
    row = lax.broadcasted_iota(jnp.int32, (ts, 1), 0)
    prev1 = tail_ref[7:8, :]
    prev2 = tail_ref[6:7, :]
    z1 = jnp.where(row == 0, prev1, pltpu.roll(z, 1, 0))
    z2 = jnp.where(row == 0, prev2, jnp.where(row == 1, prev1, pltpu.roll(z, 2, 0)))
    zc = taps_ref[0:1, :] * z2 + taps_ref[1:2, :] * z1 + taps_ref[2:3, :] * z
    tail_ref[...] = z[ts - 8:ts, :]
    y = _dot((gate_b * zc).astype(_BF16), wout_ref[...])
    r = DN_ALPHA * x + (1.0 + g_ref[0]) * y
    o_ref[0] = _layer_norm(r, lg_ref[...], lb_ref[...])


def _conv_ln(x, sc, sh, gate, w_in, taps, w_out, ln_g, ln_b):
    bsz, s, d = x.shape
    ts = 512
    row = lambda b, i: (b, i, 0)
    mod = pl.BlockSpec((1, 1, d), lambda b, i: (b, 0, 0))
    vec = pl.BlockSpec((1, d), lambda b, i: (0, 0))
    return pl.pallas_call(
        _conv_kernel,
        grid=(bsz, s // ts),
        in_specs=[pl.BlockSpec((1, ts, d), row), mod, mod, mod,
                  pl.BlockSpec((d, 3 * d), lambda b, i: (0, 0)),
                  pl.BlockSpec((taps.shape[0], d), lambda b, i: (0, 0)),
                  pl.BlockSpec((d, d), lambda b, i: (0, 0)), vec, vec],
        out_specs=pl.BlockSpec((1, ts, d), row),
        out_shape=jax.ShapeDtypeStruct((bsz, s, d), _F32),
        scratch_shapes=[pltpu.VMEM((8, d), _F32)],
        compiler_params=_cparams(("arbitrary", "arbitrary")),
        name="l1_conv_ln",
    )(x, sc, sh, gate, w_in, taps, w_out, ln_g.reshape(1, d), ln_b.reshape(1, d))


ROUTE_TM = 512


def _router_kernel(x_ref, sc_ref, sh_ref, rw_ref, rb_ref, tri_ref, h_ref, route_ref, route_t_ref, cnt_ref, run_ref):
    tm = ROUTE_TM

    @pl.when((pl.program_id(0) == 0) & (pl.program_id(1) == 0))
    def _():
        run_ref[...] = jnp.zeros_like(run_ref)

    h = x_ref[0] * (1.0 + sc_ref[0]) + sh_ref[0]
    h_ref[0] = h
    lane = lax.broadcasted_iota(jnp.int32, (1, LANES), 1)
    h_hi, h_lo = _split_bf16(h)
    two = _dot(h_hi, rw_ref[...])
    logits = two[:, :LANES] + two[:, LANES:] + _dot(h_lo, rw_ref[:, :LANES]) + rb_ref[...]
    logits = jnp.where(lane < N_EXPERTS, logits, -3e38)
    m1 = jnp.max(logits, axis=1, keepdims=True)
    e1 = jnp.min(jnp.where(logits == m1, lane, LANES), axis=1, keepdims=True)
    rest = jnp.where(lane == e1, -3e38, logits)
    m2 = jnp.max(rest, axis=1, keepdims=True)
    e2 = jnp.min(jnp.where(rest == m2, lane, LANES), axis=1, keepdims=True)
    ex = jnp.exp(m2 - m1)
    g1 = 1.0 / (1.0 + ex)
    g2 = ex / (1.0 + ex)
    oh1 = jnp.where(lane == e1, 1.0, 0.0)
    oh2 = jnp.where(lane == e2, 1.0, 0.0)
    oh = oh1 + oh2
    before = run_ref[...] + _dot(tri_ref[...], oh.astype(_BF16))
    p1 = jnp.sum(oh1 * before, axis=1, keepdims=True)
    p2 = jnp.sum(oh2 * before, axis=1, keepdims=True)
    run_ref[...] += jnp.sum(oh, axis=0, keepdims=True)
    cnt_ref[...] = run_ref[...]
    vals = (e1.astype(_F32), e2.astype(_F32), p1, p2, g1, g2)
    out = jnp.zeros((tm, LANES), _F32)
    for k, v in enumerate(vals):
        out = jnp.where(lane == k, v, out)
    route_ref[0] = out
    route_t_ref[0, 0] = out.T[:8, :]


def _router(x, sc, sh, rw, rb):
    bsz, s, d = x.shape
    tm = ROUTE_TM
    row = lambda b, i: (b, i, 0)
    mod = pl.BlockSpec((1, 1, d), lambda b, i: (b, 0, 0))
    tri = (np.arange(tm)[None, :] < np.arange(tm)[:, None]).astype(np.float32)
    return pl.pallas_call(
        _router_kernel,
        grid=(bsz, s // tm),
        in_specs=[pl.BlockSpec((1, tm, d), row), mod, mod,
                  pl.BlockSpec((d, 2 * LANES), lambda b, i: (0, 0)),
                  pl.BlockSpec((1, LANES), lambda b, i: (0, 0)),
                  pl.BlockSpec((tm, tm), lambda b, i: (0, 0))],
        out_specs=[pl.BlockSpec((1, tm, d), row),
                   pl.BlockSpec((1, tm, LANES), row),
                   pl.BlockSpec((1, 1, 8, tm), lambda b, i: (b, i, 0, 0)),
                   pl.BlockSpec((1, LANES), lambda b, i: (0, 0))],
        out_shape=[jax.ShapeDtypeStruct((bsz, s, d), _F32),
                   jax.ShapeDtypeStruct((bsz, s, LANES), _F32),
                   jax.ShapeDtypeStruct((bsz, s // tm, 8, tm), _F32),
                   jax.ShapeDtypeStruct((1, LANES), _F32)],
        scratch_shapes=[pltpu.VMEM((1, LANES), _F32)],
        compiler_params=_cparams(("arbitrary", "arbitrary")),
        name="l1_router",
    )(x, sc, sh, rw, rb, jnp.asarray(tri, _BF16))


def _row_tokens_kernel(d1_ref, d2_ref, lo_ref, hi_ref, o_ref):
    def clear_range(k, carry):
        def clear(r, c):
            o_ref[r] = 0
            return c
        return lax.fori_loop(lo_ref[k], hi_ref[k], clear, carry)
    lax.fori_loop(0, lo_ref.shape[0], clear_range, 0)

    def place(a, carry):
        o_ref[d1_ref[a]] = a
        o_ref[d2_ref[a]] = a
        return carry
    lax.fori_loop(0, d1_ref.shape[0], place, 0, unroll=8)


def _row_tokens(dest1, dest2, pad_lo, pad_hi, rows):
    smem = pl.BlockSpec(memory_space=pltpu.SMEM)
    return pl.pallas_call(
        _row_tokens_kernel,
        in_specs=[smem, smem, smem, smem],
        out_specs=smem,
        out_shape=jax.ShapeDtypeStruct((rows,), jnp.int32),
        name="l1_row_tokens",
    )(dest1, dest2, pad_lo, pad_hi)


GATHER_UNROLL = 8


def _row_gather(src_hbm, idx_ref, base, dst_ref, sem, n):
    def issue(r, carry):
        tok = idx_ref[base + r]
        pltpu.make_async_copy(src_hbm.at[pl.ds(tok, 1)], dst_ref.at[pl.ds(r, 1)], sem).start()
        return carry
    lax.fori_loop(0, n, issue, 0, unroll=GATHER_UNROLL)


def _row_gather_wait(src_hbm, dst_ref, sem, n):
    pltpu.make_async_copy(src_hbm.at[pl.ds(0, n)], dst_ref, sem).wait()


def _expert_kernel(blk_e_ref, n_used_ref, row_tok_ref, h_hbm, w1_ref, w3_ref, w2_ref, y_ref, xbuf, sems):
    i = pl.program_id(0)
    n_used = n_used_ref[0]
    rb = MOE_ROW_BLOCK
    slot = lax.rem(i, 2)

    def gather(block, s):
        for r in range(rb):
            pltpu.make_async_copy(h_hbm.at[pl.ds(row_tok_ref[block * rb + r], 1)], xbuf.at[s, pl.ds(r, 1)],
                                  sems.at[s]).start()

    @pl.when((i == 0) & (n_used > 0))
    def _():
        _row_gather(h_hbm, row_tok_ref, 0, xbuf.at[0], sems.at[0], rb)

    for s in range(2):
        @pl.when((i + 1 < n_used) & (slot == 1 - s))
        def _():
            gather(i + 1, s)

    @pl.when(i < n_used)
    def _():
        _row_gather_wait(h_hbm, xbuf.at[slot], sems.at[slot], rb)
        y_ref[...] = _swiglu(xbuf[slot].astype(_BF16), w1_ref.at[0], w3_ref.at[0], w2_ref.at[0])

    @pl.when(i >= n_used)
    def _():
        y_ref[...] = jnp.zeros_like(y_ref)


def _experts(blk_e, n_used, row_tok, h_flat, w1, w3, w2):
    t, d = h_flat.shape
    n_blocks = blk_e.shape[0]
    rb = MOE_ROW_BLOCK
    ff = w1.shape[2]
    grid_spec = pltpu.PrefetchScalarGridSpec(
        num_scalar_prefetch=3,
        grid=(n_blocks,),
        in_specs=[pl.BlockSpec(memory_space=pl.ANY),
                  pl.BlockSpec((1, d, ff), lambda i, be, nu, rt: (be[i], 0, 0)),
                  pl.BlockSpec((1, d, ff), lambda i, be, nu, rt: (be[i], 0, 0)),
                  pl.BlockSpec((1, ff, d), lambda i, be, nu, rt: (be[i], 0, 0))],
        out_specs=pl.BlockSpec((rb, d), lambda i, be, nu, rt: (i, 0)),
        scratch_shapes=[pltpu.VMEM((2, rb, d), _F32), pltpu.SemaphoreType.DMA((2,))],
    )
    return pl.pallas_call(
        _expert_kernel,
        grid_spec=grid_spec,
        out_shape=jax.ShapeDtypeStruct((n_blocks * rb, d), _F32),
        compiler_params=_cparams(("arbitrary",)),
        name="l1_experts",
    )(blk_e, n_used, row_tok, h_flat, w1, w3, w2)


COMB_TM = 512


def _combine_kernel(d1_ref, d2_ref, y_hbm, x_ref, route_ref, g_ref, lg_ref, lb_ref, o_ref, y1buf, y2buf, sems):
    tm = COMB_TM
    i = pl.program_id(0) * pl.num_programs(1) + pl.program_id(1)
    n = pl.num_programs(0) * pl.num_programs(1)
    slot = lax.rem(i, 2)

    def gather(tile, s):
        for r in range(tm):
            pltpu.make_async_copy(y_hbm.at[pl.ds(d1_ref[tile * tm + r], 1)], y1buf.at[s, pl.ds(r, 1)],
                                  sems.at[0, s]).start(priority=0)
            pltpu.make_async_copy(y_hbm.at[pl.ds(d2_ref[tile * tm + r], 1)], y2buf.at[s, pl.ds(r, 1)],
                                  sems.at[1, s]).start(priority=1)

    @pl.when(i == 0)
    def _():
        gather(0, 0)

    for s in range(2):
        @pl.when((i + 1 < n) & (slot == 1 - s))
        def _():
            gather(i + 1, s)

    _row_gather_wait(y_hbm, y1buf.at[slot], sems.at[0, slot], tm)
    _row_gather_wait(y_hbm, y2buf.at[slot], sems.at[1, slot], tm)
    route = route_ref[0]
    y = route[:, 4:5] * y1buf[slot] + route[:, 5:6] * y2buf[slot]
    r = DN_ALPHA * x_ref[0] + (1.0 + g_ref[0]) * y
    o_ref[0] = _layer_norm(r, lg_ref[...], lb_ref[...])


def _combine_ln(dest1, dest2, y, x, route, gate, ln_g, ln_b):
    bsz, s, d = x.shape
    tm = COMB_TM
    row = lambda b, i, d1, d2: (b, i, 0)
    vec = pl.BlockSpec((1, d), lambda b, i, d1, d2: (0, 0))
    grid_spec = pltpu.PrefetchScalarGridSpec(
        num_scalar_prefetch=2,
        grid=(bsz, s // tm),
        in_specs=[pl.BlockSpec(memory_space=pl.ANY),
                  pl.BlockSpec((1, tm, d), row),
                  pl.BlockSpec((1, tm, LANES), row),
                  pl.BlockSpec((1, 1, d), lambda b, i, d1, d2: (b, 0, 0)), vec, vec],
        out_specs=pl.BlockSpec((1, tm, d), row),
        scratch_shapes=[pltpu.VMEM((2, tm, d), _F32), pltpu.VMEM((2, tm, d), _F32),
                        pltpu.SemaphoreType.DMA((2, 2))],
    )
    return pl.pallas_call(
        _combine_kernel,
        grid_spec=grid_spec,
        out_shape=jax.ShapeDtypeStruct((bsz, s, d), _F32),
        compiler_params=_cparams(("arbitrary", "arbitrary")),
        name="l1_combine_ln",
    )(dest1, dest2, y, x, route, gate, ln_g.reshape(1, d), ln_b.reshape(1, d))


def _arrange_inproj_weight(w):
    d = w.shape[0]
    kv0 = 3 * SB_W + NSA_Q_W
    g0 = kv0 + 6 * NSA_KV_GROUPS * HEAD_DIM
    per_group = NSA_REP * 3
    pad = jnp.zeros((d, LANES - per_group), w.dtype)
    cols = [w[:, :SB_W] * QK_SCALE, w[:, SB_W:3 * SB_W], w[:, 3 * SB_W:kv0] * (QK_SCALE * LOG2E), w[:, kv0:g0]]
    for g in range(NSA_KV_GROUPS):
        cols += [w[:, g0 + g * per_group:g0 + (g + 1) * per_group], pad]
    return jnp.concatenate(cols, axis=1).astype(_BF16)


def _mods(mod, i, bsz, d):
    m = mod[i].reshape(bsz, 6, 1, d)
    return [m[:, k] for k in range(6)]


def kernel(x, c, ada_w, ada_b, ln_g, ln_b, mix_w_in, cmp_pos, cmp_w1, cmp_w2, mix_w_out, ffn_w1, ffn_w3, ffn_w2,
           conv_w_in, conv_taps, conv_w_out, router_w, router_b, exp_w1, exp_w3, exp_w2):
    bsz, s, d = x.shape
    t = bsz * s
    mod = _ada_mod(c, ada_w, ada_b)

    sh1, sc1, g1, sh2, sc2, g2 = _mods(mod, 0, bsz, d)
    sb, nq, kvc, sw, gates = _inproj(x, sc1, sh1, _arrange_inproj_weight(mix_w_in[0]))
    o_sb = _sb_attention(sb)
    pos = cmp_pos[0].reshape(2, 2, NSA_CMP_STRIDE * HEAD_DIM)
    w2rep = jnp.tile(cmp_w2[0], (1, 1, NSA_REP)).astype(_BF16)
    cmp_kv = _compress(kvc, pos, cmp_w1[0].astype(_BF16), w2rep)
    o_nsa = _nsa_attention(nq, cmp_kv, sw, gates)
    x = _l0_tail(o_sb, o_nsa, mix_w_out[0].astype(_BF16), x, g1, sc2, sh2, g2, ffn_w1[0].astype(_BF16),
                 ffn_w3[0].astype(_BF16), ffn_w2[0].astype(_BF16), ln_g[0], ln_b[0])

    sh1, sc1, g1, sh2, sc2, g2 = _mods(mod, 1, bsz, d)
    x = _conv_ln(x, sc1, sh1, g1, conv_w_in[0].astype(_BF16), conv_taps[0], conv_w_out[0].astype(_BF16),
                 ln_g[1, 0], ln_b[1, 0])
    rw = jnp.concatenate(_split_bf16(jnp.pad(router_w[0], ((0, 0), (0, LANES - N_EXPERTS)))), axis=1)
    rb = jnp.pad(router_b[0], (0, LANES - N_EXPERTS)).reshape(1, LANES)
    h2, route, route_t, counts = _router(x, sc2, sh2, rw, rb)

    field = lambda k: route_t[:, :, k, :].reshape(t).astype(jnp.int32)
    e1, e2, p1, p2 = field(0), field(1), field(2), field(3)
    cnt = counts[0, :N_EXPERTS].astype(jnp.int32)
    rb_rows = MOE_ROW_BLOCK
    padded = (cnt + rb_rows - 1) // rb_rows * rb_rows
    pends = jnp.cumsum(padded)
    pstarts = pends - padded
    dest1 = pstarts[e1] + p1
    dest2 = pstarts[e2] + p2
    n_blocks = -(-(t * 2) // rb_rows) + N_EXPERTS
    rows = n_blocks * rb_rows
    pad_lo = jnp.concatenate([pstarts + cnt, pends[-1:]]).astype(jnp.int32)
    pad_hi = jnp.concatenate([pends, jnp.full((1,), rows, jnp.int32)]).astype(jnp.int32)
    row_tok = _row_tokens(dest1, dest2, pad_lo, pad_hi, rows)
    blk_start = jnp.arange(n_blocks, dtype=jnp.int32) * rb_rows
    blk_e = jnp.minimum(jnp.sum(pends[None, :] <= blk_start[:, None], axis=1), N_EXPERTS - 1).astype(jnp.int32)
    n_used = (pends[-1:] // rb_rows).astype(jnp.int32)

    y = _experts(blk_e, n_used, row_tok, h2.reshape(t, d), exp_w1[0].astype(_BF16), exp_w3[0].astype(_BF16),
                 exp_w2[0].astype(_BF16))
    return _combine_ln(dest1, dest2, y, x, route, g2, ln_g[1, 1], ln_b[1, 1])
```

```python
import functools

import numpy as np
import jax
import jax.numpy as jnp
from jax import lax
from jax.experimental import pallas as pl
from jax.experimental.pallas import tpu as pltpu

HEAD_DIM = 64
SB_HEADS = 8
NSA_HEADS = 8
NSA_KV_GROUPS = 2
NSA_REP = NSA_HEADS // NSA_KV_GROUPS
NSA_CMP_LEN = 32
NSA_CMP_STRIDE = 16
NSA_SLC_BLOCK = 64
NSA_SLC_TOPK = 16
NSA_WINDOW = 512
SB_W = SB_HEADS * HEAD_DIM
NSA_Q_W = NSA_HEADS * HEAD_DIM
N_EXPERTS = 8
MOE_ROW_BLOCK = 512
DEPTH = 2
LN_EPS = 1e-5
DN_ALPHA = (2 * DEPTH) ** 0.25
NEG = -1e30
SEL_FORCE = 1e30
QK_SCALE = HEAD_DIM ** -0.5
LOG2E = 1.4426950408889634

LANES = 128
VMEM_LIMIT = 56 * 1024 * 1024

C_SB = 0
C_NQ = 3 * SB_W
C_CMP = C_NQ + NSA_Q_W
C_SW = C_CMP + 4 * HEAD_DIM
C_GATE = C_SW + 4 * 2 * HEAD_DIM
C_END = C_GATE + 2 * LANES

_F32 = jnp.float32
_BF16 = jnp.bfloat16


def _cparams(sem, vmem=VMEM_LIMIT):
    return pltpu.CompilerParams(dimension_semantics=sem, vmem_limit_bytes=vmem)


def _dot(a, b):
    return jnp.dot(a, b, preferred_element_type=_F32)


def _dot_t(a, b):
    return lax.dot_general(a, b, (((1,), (1,)), ((), ())), preferred_element_type=_F32)


def _split_bf16(x):
    hi = x.astype(_BF16)
    lo = (x - hi.astype(_F32)).astype(_BF16)
    return hi, lo


def _silu(x):
    return x / (1.0 + jnp.exp(-x))


def _layer_norm(r, g, b):
    mu = jnp.mean(r, axis=-1, keepdims=True)
    d = r - mu
    var = jnp.mean(d * d, axis=-1, keepdims=True)
    return d * lax.rsqrt(var + LN_EPS) * g + b


def _ada_kernel(c_ref, w_ref, b_ref, o_ref):
    cond = _silu(c_ref[...])
    o_ref[0] = jnp.dot(cond, w_ref[0], preferred_element_type=_F32,
                       precision=lax.Precision.HIGHEST) + b_ref[0]


def _ada_mod(c, ada_w, ada_b):
    depth, d, n = ada_w.shape
    bsz = c.shape[0]
    tn = n // 6
    return pl.pallas_call(
        _ada_kernel,
        grid=(depth, n // tn),
        in_specs=[pl.BlockSpec((bsz, d), lambda i, j: (0, 0)),
                  pl.BlockSpec((1, d, tn), lambda i, j: (i, 0, j)),
                  pl.BlockSpec((1, 1, tn), lambda i, j: (i, 0, j))],
        out_specs=pl.BlockSpec((1, bsz, tn), lambda i, j: (i, 0, j)),
        out_shape=jax.ShapeDtypeStruct((depth, bsz, n), _F32),
        compiler_params=_cparams(("parallel", "parallel")),
        name="ada_mod",
    )(c, ada_w, ada_b.reshape(depth, 1, n))


def _inproj_kernel(x_ref, sc_ref, sh_ref, w_ref, sb_ref, nq_ref, cmp_ref, sw_ref, gate_ref):
    h = (x_ref[0] * (1.0 + sc_ref[0]) + sh_ref[0]).astype(_BF16)
    sb_ref[0] = _dot(h, w_ref[:, C_SB:C_NQ]).astype(_BF16)
    nq_ref[0] = _dot(h, w_ref[:, C_NQ:C_CMP]).astype(_BF16)
    kvc = _dot(h, w_ref[:, C_CMP:C_SW])
    for j in range(4):
        cmp_ref[0, j] = kvc[:, j * HEAD_DIM:(j + 1) * HEAD_DIM]
    lane = lax.broadcasted_iota(jnp.int32, (1, LANES), 1)
    first = lane < HEAD_DIM
    for kind in range(4):
        a = _dot(h, w_ref[:, C_SW + kind * LANES:C_SW + (kind + 1) * LANES])
        r = pltpu.roll(a, HEAD_DIM, 1)
        g0 = jnp.where(first, a, r).astype(_BF16)
        g1 = jnp.where(first, r, a).astype(_BF16)
        sw_ref[0, :, (2 * kind) * 256:(2 * kind + 1) * 256] = jnp.concatenate([g0, g0], axis=1)
        sw_ref[0, :, (2 * kind + 1) * 256:(2 * kind + 2) * 256] = jnp.concatenate([g1, g1], axis=1)
    gate_ref[0] = _dot(h, w_ref[:, C_GATE:C_END])


def _inproj(x, sc, sh, w):
    bsz, s, d = x.shape
    tm = 512
    row = lambda b, i: (b, i, 0)
    return pl.pallas_call(
        _inproj_kernel,
        grid=(bsz, s // tm),
        in_specs=[pl.BlockSpec((1, tm, d), row),
                  pl.BlockSpec((1, 1, d), lambda b, i: (b, 0, 0)),
                  pl.BlockSpec((1, 1, d), lambda b, i: (b, 0, 0)),
                  pl.BlockSpec((d, C_END), lambda b, i: (0, 0))],
        out_specs=[pl.BlockSpec((1, tm, 3 * SB_W), row),
                   pl.BlockSpec((1, tm, NSA_Q_W), row),
                   pl.BlockSpec((1, 4, tm, HEAD_DIM), lambda b, i: (b, 0, i, 0)),
                   pl.BlockSpec((1, tm, 2048), row),
                   pl.BlockSpec((1, tm, 2 * LANES), row)],
        out_shape=[jax.ShapeDtypeStruct((bsz, s, 3 * SB_W), _BF16),
                   jax.ShapeDtypeStruct((bsz, s, NSA_Q_W), _BF16),
                   jax.ShapeDtypeStruct((bsz, 4, s, HEAD_DIM), _F32),
                   jax.ShapeDtypeStruct((bsz, s, 2048), _BF16),
                   jax.ShapeDtypeStruct((bsz, s, 2 * LANES), _F32)],
        compiler_params=_cparams(("parallel", "parallel")),
        name="l0_inproj",
    )(x, sc, sh, w)


SB_TQ = 1024
SB_TK = 256
SB_UNROLL = 2
SB_PAIRS = 1


def _sb_kernel(q_ref, k_ref, v_ref, tri_ref, o_ref, acc_ref, run_ref):
    tq, tk = SB_TQ, SB_TK
    qi = pl.program_id(2)
    lane = lax.broadcasted_iota(jnp.int32, (1, LANES), 1)
    first = lane < HEAD_DIM
    pairs = range(SB_PAIRS)
    lanes = [slice(p * LANES, (p + 1) * LANES) for p in pairs]
    q2 = []
    for p in pairs:
        q = q_ref[0, :, lanes[p]]
        zero = jnp.zeros_like(q)
        q2.append(jnp.concatenate([jnp.where(first, q, zero), jnp.where(first, zero, q)], axis=0))
    tri = tri_ref[...]
    acc_ref[...] = jnp.zeros_like(acc_ref)
    run_ref[...] = jnp.zeros_like(run_ref)

    def scores(qrows, key_off, p):
        return _dot_t(qrows, k_ref[0, pl.ds(key_off, tk), lanes[p]])

    def suffix(z, causal):
        nlf = jnp.maximum(z, 0.0) + jnp.log(1.0 + jnp.exp2(jnp.abs(z) * -LOG2E))
        if causal is not None:
            nlf = jnp.where(causal, nlf, 0.0)
        return z - nlf, _dot(nlf.astype(_BF16), tri), nlf[:, 0:1]

    def weigh(lb, ext, nlf0, run, causal, key_off, p):
        w = jnp.exp(lb + ext + jnp.concatenate([run] * (tk // LANES), axis=1))
        if causal is not None:
            w = jnp.where(causal, w, 0.0)
        return (_dot(w.astype(_BF16), v_ref[0, pl.ds(key_off, tk), lanes[p]]),
                run + jnp.broadcast_to(ext[:, 0:1] - nlf0, run.shape))

    n_diag = tq // tk
    diag = []
    for kd in reversed(range(n_diag)):
        r0 = kd * tk
        key_off = pl.multiple_of(qi * tq + r0, tk)
        row = r0 + lax.broadcasted_iota(jnp.int32, (tq - r0, tk), 0)
        col = r0 + lax.broadcasted_iota(jnp.int32, (tq - r0, tk), 1)
        causal = jnp.concatenate([col < row] * 2, axis=0)
        for p in pairs:
            z = scores(jnp.concatenate([q2[p][r0:tq], q2[p][tq + r0:2 * tq]], axis=0), key_off, p)
            diag.append((p, r0, key_off, causal, z))
    diag = [(p, r0, key_off, causal, suffix(z, causal)) for p, r0, key_off, causal, z in diag]
    for p, r0, key_off, causal, parts in diag:
        n = tq - r0
        base = p * 2 * tq
        run = jnp.concatenate([run_ref[base + r0:base + tq, :], run_ref[base + tq + r0:base + 2 * tq, :]], axis=0)
        pv, run = weigh(*parts, run, causal, key_off, p)
        for h in range(2):
            rows = slice(base + h * tq + r0, base + (h + 1) * tq)
            acc_ref[rows, :] += pv[h * n:(h + 1) * n]
            run_ref[rows, :] = run[h * n:(h + 1) * n]

    def below(jj, carry):
        offs = [pl.multiple_of((qi * n_diag - 1 - SB_UNROLL * jj - u) * tk, tk) for u in range(SB_UNROLL)]
        zs = [[scores(q2[p], off, p) for off in offs] for p in pairs]
        parts = [[suffix(z, None) for z in zs[p]] for p in pairs]
        for p in pairs:
            rows = slice(p * 2 * tq, (p + 1) * 2 * tq)
            run = run_ref[rows, :]
            acc = acc_ref[rows, :]
            for part, off in zip(parts[p], offs):
                pv, run = weigh(*part, run, None, off, p)
                acc = acc + pv
            acc_ref[rows, :] = acc
            run_ref[rows, :] = run
        return carry
    lax.fori_loop(0, qi * (n_diag // SB_UNROLL), below, 0)
    for p in pairs:
        base = p * 2 * tq
        o_ref[0, :, lanes[p]] = jnp.where(first, acc_ref[base:base + tq, :],
                                          acc_ref[base + tq:base + 2 * tq, :]).astype(o_ref.dtype)


def _sb_attention(sb):
    bsz, s, _ = sb.shape
    tq, tk = SB_TQ, SB_TK
    width = SB_PAIRS * LANES
    n_step = SB_W // width
    tri = -(np.arange(tk)[:, None] > np.arange(tk)[None, :]).astype(np.float32)
    return pl.pallas_call(
        _sb_kernel,
        grid=(bsz, n_step, s // tq),
        in_specs=[pl.BlockSpec((1, tq, width), lambda b, p, i: (b, i, p)),
                  pl.BlockSpec((1, s, width), lambda b, p, i: (b, 0, n_step + p)),
                  pl.BlockSpec((1, s, width), lambda b, p, i: (b, 0, 2 * n_step + p)),
                  pl.BlockSpec((tk, tk), lambda b, p, i: (0, 0))],
        out_specs=pl.BlockSpec((1, tq, width), lambda b, p, i: (b, i, p)),
        out_shape=jax.ShapeDtypeStruct((bsz, s, SB_W), _BF16),
        scratch_shapes=[pltpu.VMEM((SB_PAIRS * 2 * tq, LANES), _F32), pltpu.VMEM((SB_PAIRS * 2 * tq, LANES), _F32)],
        compiler_params=_cparams(("parallel", "parallel", "arbitrary")),
        name="sb_attention",
    )(sb, sb, sb, jnp.asarray(tri, _BF16))


def _compress_kernel(x_ref, pos_ref, w1_ref, w2_ref, o_ref):
    half = w1_ref.shape[1] // 2
    n_chunk = x_ref.shape[2] // NSA_CMP_STRIDE
    ch = jnp.concatenate([x_ref[0, 0, pl.ds(l, n_chunk, stride=NSA_CMP_STRIDE), :]
                          for l in range(NSA_CMP_STRIDE)], axis=1)
    top = _dot((ch + pos_ref[0, 0:1, :]).astype(_BF16), w1_ref[0, :half, :])
    bot = _dot((ch + pos_ref[0, 1:2, :]).astype(_BF16), w1_ref[0, half:, :])
    n = ch.shape[0]
    pre = top + pltpu.roll(bot, n - 1, 0)
    o_ref[0, 0] = _dot(_silu(pre).astype(_BF16), w2_ref[0]).astype(o_ref.dtype)


def _compress(kvc, pos, w1, w2rep):
    bsz, _, s, dh = kvc.shape
    n_chunk = s // NSA_CMP_STRIDE
    cw = NSA_CMP_STRIDE * dh
    hid = w1.shape[-1]
    return pl.pallas_call(
        _compress_kernel,
        grid=(bsz, 4),
        in_specs=[pl.BlockSpec((1, 1, s, dh), lambda b, j: (b, j, 0, 0)),
                  pl.BlockSpec((1, 2, cw), lambda b, j: (j // 2, 0, 0)),
                  pl.BlockSpec((1, 2 * cw, hid), lambda b, j: (j // 2, 0, 0)),
                  pl.BlockSpec((1, hid, 4 * dh), lambda b, j: (j // 2, 0, 0))],
        out_specs=pl.BlockSpec((1, 1, n_chunk, 4 * dh), lambda b, j: (b, j, 0, 0)),
        out_shape=jax.ShapeDtypeStruct((bsz, 4, n_chunk, 4 * dh), _BF16),
        compiler_params=_cparams(("parallel", "parallel")),
        name="nsa_compress",
    )(kvc, pos, w1, w2rep)


NSA_TQ = 128
NSA_TK = 512
WIN_SPAN = NSA_WINDOW + NSA_TQ


def _nsa_kernel(*refs):
    s_len = refs[3].shape[1]
    tiles_per_blk = NSA_TK // NSA_TQ
    n_blk = s_len // NSA_TK
    i = pl.program_id(1)
    for v in range(n_blk // 2):
        pl.when(i // tiles_per_blk == v)(functools.partial(_nsa_body, v + 1, n_blk - v, *refs))


def _nsa_body(n_blk_a, n_blk_b, qa_ref, qb_ref, cmp_ref, sw_ref, ga_ref, gb_ref, ov_ref, ex_ref, o_ref):
    tq, tk = NSA_TQ, NSA_TK
    s_len = sw_ref.shape[1]
    n_tiles = s_len // tq
    n_cmp = cmp_ref.shape[2]
    n_slc = s_len // NSA_SLC_BLOCK
    gw = NSA_REP * HEAD_DIM
    i = pl.program_id(1)
    head_of_lane = lax.broadcasted_iota(jnp.int32, (1, gw), 1) // HEAD_DIM

    def kv_lanes(kind, g):
        return slice((2 * kind + g) * gw, (2 * kind + g + 1) * gw)

    def unstack(o):
        out = jnp.zeros((tq, gw), _F32)
        for r in range(NSA_REP):
            out = jnp.where(head_of_lane == r, o[r * tq:(r + 1) * tq], out)
        return out

    def tile4(x):
        return jnp.concatenate([x] * NSA_REP, axis=0)

    class Tile:
        pass
    tiles = []
    for half, (q_ref, g_ref, t0, n_blk) in enumerate(((qa_ref, ga_ref, i * tq, n_blk_a),
                                                      (qb_ref, gb_ref, (n_tiles - 1 - i) * tq, n_blk_b))):
        for g in range(NSA_KV_GROUPS):
            t = Tile()
            q = q_ref[0, :, g * gw:(g + 1) * gw]
            zero = jnp.zeros_like(q)
            t.qs = jnp.concatenate([jnp.where(head_of_lane == r, q, zero) for r in range(NSA_REP)], axis=0)
            t.half, t.g, t.t0, t.n_blk, t.g_ref = half, g, t0, n_blk, g_ref
            t.t_row = t0 + lax.broadcasted_iota(jnp.int32, (tq, 1), 0)
            t.start = pl.multiple_of(jnp.maximum(t0 - NSA_WINDOW, 0), tq)
            tiles.append(t)

    for t in tiles:
        t.s_c = _dot_t(t.qs, cmp_ref[0, t.g])
        t.s_w = _dot_t(t.qs, sw_ref[0, pl.ds(t.start, WIN_SPAN), kv_lanes(2, t.g)])
        t.s_s = [_dot_t(t.qs, sw_ref[0, j * tk:(j + 1) * tk, kv_lanes(0, t.g)]) for j in range(t.n_blk)]

    cmp_last = lax.broadcasted_iota(jnp.int32, (1, n_cmp), 1) * NSA_CMP_STRIDE + (NSA_CMP_LEN - 1)
    for t in tiles:
        mask_c = tile4(cmp_last <= t.t_row)
        sm = jnp.where(mask_c, t.s_c, NEG)
        e_c = jnp.where(mask_c, jnp.exp2(sm - jnp.max(sm, axis=1, keepdims=True)), 0.0)
        l_c = jnp.sum(e_c, axis=1, keepdims=True)
        t.p_c = e_c / jnp.where(l_c == 0.0, 1.0, l_c)
    for t in tiles:
        t.o_cmp = unstack(_dot(t.p_c.astype(_BF16), cmp_ref[0, NSA_KV_GROUPS + t.g]))
        p_sum = t.p_c[0:tq] + t.p_c[tq:2 * tq] + t.p_c[2 * tq:3 * tq] + t.p_c[3 * tq:4 * tq]
        hi, lo = _split_bf16(p_sum)
        t.imp = _dot_t(ov_ref[...], jnp.concatenate([hi, lo], axis=1))[:n_slc]

    for t in tiles:
        diff = t.t_row - (t.start + lax.broadcasted_iota(jnp.int32, (1, WIN_SPAN), 1))
        bias_w = jnp.where((diff >= 0) & (diff < NSA_WINDOW), 0.0, NEG)
        sm = t.s_w + tile4(bias_w)
        e_w = jnp.exp2(sm - jnp.max(sm, axis=1, keepdims=True))
        t.l_w = jnp.sum(e_w, axis=1, keepdims=True)
        t.e_w = e_w.astype(_BF16)
    for t in tiles:
        t.o_win = unstack(_dot(t.e_w, sw_ref[0, pl.ds(t.start, WIN_SPAN), kv_lanes(3, t.g)]) / t.l_w)

    blk = lax.broadcasted_iota(jnp.int32, (n_slc, 1), 0)
    for t in tiles:
        cur = (t.t0 + lax.broadcasted_iota(jnp.int32, (1, tq), 1)) // NSA_SLC_BLOCK
        valid = blk <= cur
        forced = (blk == 0) | (blk == cur) | (blk == cur - 1)
        val = jnp.where(valid, jnp.where(forced, SEL_FORCE, t.imp), NEG)
        rank = jnp.zeros((n_slc, tq), _F32)
        for b in range(n_slc):
            vb = val[b:b + 1, :]
            ahead = (vb > val) | ((vb == val) & (blk > b))
            rank = rank + jnp.where(ahead, 1.0, 0.0)
        sel_t = jnp.where(rank < float(min(NSA_SLC_TOPK, n_slc)), 1.0, 0.0)
        sel_t = jnp.concatenate([sel_t, jnp.zeros((LANES - n_slc, tq), _F32)], axis=0)
        t.sel = sel_t.T.astype(_BF16)

    for t in tiles:
        masked = []
        for j in range(t.n_blk):
            chosen = _dot(t.sel, ex_ref[:, j * tk:(j + 1) * tk])
            kpos = j * tk + lax.broadcasted_iota(jnp.int32, (1, tk), 1)
            bias = jnp.where((chosen > 0.5) & (kpos <= t.t_row), 0.0, NEG)
            masked.append(t.s_s[j] + tile4(bias))
            m_j = masked[j].max(axis=1, keepdims=True)
            m = m_j if j == 0 else jnp.maximum(m, m_j)
        e_s = [jnp.exp2(sb - m) for sb in masked]
        t.l_s = sum(e.sum(axis=1, keepdims=True) for e in e_s)
        t.e_s = [e.astype(_BF16) for e in e_s]
    for t in tiles:
        acc = _dot(t.e_s[0], sw_ref[0, 0:tk, kv_lanes(1, t.g)])
        for j in range(1, t.n_blk):
            acc = acc + _dot(t.e_s[j], sw_ref[0, j * tk:(j + 1) * tk, kv_lanes(1, t.g)])
        t.o_slc = unstack(acc / t.l_s)

    for t in tiles:
        g = 1.0 / (1.0 + jnp.exp(-t.g_ref[0, :, t.g * LANES:(t.g + 1) * LANES]))
        out = jnp.zeros((tq, gw), _F32)
        for branch, o_b in enumerate((t.o_cmp, t.o_slc, t.o_win)):
            gm = jnp.zeros((tq, gw), _F32)
            for r in range(NSA_REP):
                c = 3 * r + branch
                gm = jnp.where(head_of_lane == r, g[:, c:c + 1], gm)
            out = out + gm * o_b
        o_ref[0, t.half, 0, :, t.g * gw:(t.g + 1) * gw] = out.astype(o_ref.dtype)


def _nsa_attention(nq, cmp_kv, sw, gates):
    bsz, s, _ = nq.shape
    tq = NSA_TQ
    n_cmp = cmp_kv.shape[2]
    n_slc = s // NSA_SLC_BLOCK
    gw = 4 * HEAD_DIM
    cmp_start = np.arange(n_cmp) * NSA_CMP_STRIDE
    slc_start = np.arange(LANES) * NSA_SLC_BLOCK
    ov = ((cmp_start[:, None] < slc_start[None, :] + NSA_SLC_BLOCK)
          & (cmp_start[:, None] + NSA_CMP_LEN > slc_start[None, :])
          & (np.arange(LANES)[None, :] < n_slc)).astype(np.float32)
    ov[n_cmp - 1:] = 0.0
    ov2 = np.concatenate([ov.T, ov.T], axis=1)
    ex = (np.arange(LANES)[:, None] == (np.arange(s)[None, :] // NSA_SLC_BLOCK)).astype(np.float32)
    n_tiles = s // tq
    n_pair = n_tiles // 2
    return pl.pallas_call(
        _nsa_kernel,
        grid=(bsz, n_pair),
        in_specs=[pl.BlockSpec((1, tq, NSA_Q_W), lambda b, i: (b, i, 0)),
                  pl.BlockSpec((1, tq, NSA_Q_W), lambda b, i: (b, n_tiles - 1 - i, 0)),
                  pl.BlockSpec((1, 2 * NSA_KV_GROUPS, n_cmp, gw), lambda b, i: (b, 0, 0, 0)),
                  pl.BlockSpec((1, s, sw.shape[2]), lambda b, i: (b, 0, 0)),
                  pl.BlockSpec((1, tq, NSA_KV_GROUPS * LANES), lambda b, i: (b, i, 0)),
                  pl.BlockSpec((1, tq, NSA_KV_GROUPS * LANES), lambda b, i: (b, n_tiles - 1 - i, 0)),
                  pl.BlockSpec((LANES, 2 * n_cmp), lambda b, i: (0, 0)),
                  pl.BlockSpec((LANES, s), lambda b, i: (0, 0))],
        out_specs=pl.BlockSpec((1, 2, 1, tq, NSA_Q_W), lambda b, i: (b, 0, i, 0, 0)),
        out_shape=jax.ShapeDtypeStruct((bsz, 2, n_pair, tq, NSA_Q_W), _BF16),
        compiler_params=_cparams(("parallel", "arbitrary")),
        name="nsa_attention",
    )(nq, nq, cmp_kv, sw, gates, gates, jnp.asarray(ov2, _BF16), jnp.asarray(ex, _BF16))


FFN_TF = 256
L0_TM = 512


def _swiglu(xb, w1, w3, w2):
    ff = w1.shape[1]
    acc = jnp.zeros((xb.shape[0], w2.shape[1]), _F32)
    for c in range(ff // FFN_TF):
        cs = slice(c * FFN_TF, (c + 1) * FFN_TF)
        a = _silu(_dot(xb, w1[:, cs])) * _dot(xb, w3[:, cs])
        acc = acc + _dot(a.astype(_BF16), w2[cs, :])
    return acc


def _l0_tail_kernel(sb_ref, nsa_ref, wa_ref, wb_ref, x_ref, g1_ref, sc_ref, sh_ref, g2_ref, w1_ref, w3_ref, w2_ref,
                    lg1_ref, lb1_ref, lg2_ref, lb2_ref, e1_ref, e3_ref, e2_ref, o_ref, c1_ref, c3_ref, c2_ref):
    c1_ref[...] = e1_ref[...].astype(_BF16)
    c3_ref[...] = e3_ref[...].astype(_BF16)
    c2_ref[...] = e2_ref[...].astype(_BF16)
    n_sub = nsa_ref.shape[2]
    tiles = [nsa_ref[0, 0, k] for k in range(n_sub)]
    upper = pl.program_id(1) >= pl.num_programs(1) // 2
    o_nsa = jnp.where(upper, jnp.concatenate(tiles[::-1], axis=0), jnp.concatenate(tiles, axis=0))
    y = _dot(sb_ref[0], wa_ref[...]) + _dot(o_nsa, wb_ref[...])
    x1 = _layer_norm(DN_ALPHA * x_ref[0] + (1.0 + g1_ref[0]) * y, lg1_ref[...], lb1_ref[...])
    h = (x1 * (1.0 + sc_ref[0]) + sh_ref[0]).astype(_BF16)
    r = DN_ALPHA * x1 + (1.0 + g2_ref[0]) * _swiglu(h, w1_ref, w3_ref, w2_ref)
    o_ref[0] = _layer_norm(r, lg2_ref[...], lb2_ref[...])


def _l0_tail(o_sb, o_nsa5, w_out, x, g1, sc2, sh2, g2, w1, w3, w2, ln_g, ln_b, exp_w):
    bsz, s, d = x.shape
    ff = w1.shape[1]
    tm = L0_TM
    n_sub = tm // NSA_TQ
    per_half = s // (2 * tm)
    n_i = s // tm
    row = lambda b, i: (b, i, 0)
    mod = pl.BlockSpec((1, 1, d), lambda b, i: (b, 0, 0))
    vec = pl.BlockSpec((1, d), lambda b, i: (0, 0))
    flat = [w.reshape(-1, w.shape[-1]) for w in exp_w]
    side = [pl.BlockSpec((f.shape[0] // (bsz * n_i), f.shape[1]), lambda b, i: (b * n_i + i, 0)) for f in flat]

    def nsa_index(b, i):
        half, j = i // per_half, i % per_half
        return b, half, jnp.where(half == 0, j, per_half - 1 - j), 0, 0
    out = pl.pallas_call(
        _l0_tail_kernel,
        grid=(bsz, s // tm),
        in_specs=[pl.BlockSpec((1, tm, SB_W), row),
                  pl.BlockSpec((1, 1, n_sub, NSA_TQ, NSA_Q_W), nsa_index),
                  pl.BlockSpec((SB_W, d), lambda b, i: (0, 0)),
                  pl.BlockSpec((NSA_Q_W, d), lambda b, i: (1, 0)),
                  pl.BlockSpec((1, tm, d), row), mod, mod, mod, mod,
                  pl.BlockSpec((d, ff), lambda b, i: (0, 0)),
                  pl.BlockSpec((d, ff), lambda b, i: (0, 0)),
                  pl.BlockSpec((ff, d), lambda b, i: (0, 0)), vec, vec, vec, vec] + side,
        out_specs=[pl.BlockSpec((1, tm, d), row)] + side,
        out_shape=[jax.ShapeDtypeStruct((bsz, s, d), _F32)] + [jax.ShapeDtypeStruct(f.shape, _BF16) for f in flat],
        compiler_params=_cparams(("parallel", "parallel")),
        name="l0_outproj_ffn_ln",
    )(o_sb, o_nsa5, w_out, w_out, x, g1, sc2, sh2, g2, w1, w3, w2,
      ln_g[0].reshape(1, d), ln_b[0].reshape(1, d), ln_g[1].reshape(1, d), ln_b[1].reshape(1, d), *flat)
    return out[0], [c.reshape(w.shape) for c, w in zip(out[1:], exp_w)]


def _conv_kernel(x_ref, sc_ref, sh_ref, g_ref, win_ref, taps_ref, wout_ref, lg_ref, lb_ref, o_ref, tail_ref):
    ts, d = x_ref.shape[1], x_ref.shape[2]

    @pl.when(pl.program_id(1) == 0)
    def _():
        tail_ref[...] = jnp.zeros_like(tail_ref)

    x = x_ref[0]
    h = (x * (1.0 + sc_ref[0]) + sh_ref[0]).astype(_BF16)
    z = _dot(h, win_ref[:, d:2 * d]) * _dot(h, win_ref[:, 2 * d:3 * d])
    gate_b = _dot(h, win_ref[:, 0:d])
    row = lax.broadcasted_iota(jnp.int32, (ts, 1), 0)
    prev1 = tail_ref[7:8, :]
    prev2 = tail_ref[6:7, :]
    z1 = jnp.where(row == 0, prev1, pltpu.roll(z, 1, 0))
    z2 = jnp.where(row == 0, prev2, jnp.where(row == 1, prev1, pltpu.roll(z, 2, 0)))
    zc = taps_ref[0:1, :] * z2 + taps_ref[1:2, :] * z1 + taps_ref[2:3, :] * z
    tail_ref[...] = z[ts - 8:ts, :]
    y = _dot((gate_b * zc).astype(_BF16), wout_ref[...])
    r = DN_ALPHA * x + (1.0 + g_ref[0]) * y
    o_ref[0] = _layer_norm(r, lg_ref[...], lb_ref[...])


def _conv_ln(x, sc, sh, gate, w_in, taps, w_out, ln_g, ln_b):
    bsz, s, d = x.shape
    ts = 512
    row = lambda b, i: (b, i, 0)
    mod = pl.BlockSpec((1, 1, d), lambda b, i: (b, 0, 0))
    vec = pl.BlockSpec((1, d), lambda b, i: (0, 0))
    return pl.pallas_call(
        _conv_kernel,
        grid=(bsz, s // ts),
        in_specs=[pl.BlockSpec((1, ts, d), row), mod, mod, mod,
                  pl.BlockSpec((d, 3 * d), lambda b, i: (0, 0)),
                  pl.BlockSpec((taps.shape[0], d), lambda b, i: (0, 0)),
                  pl.BlockSpec((d, d), lambda b, i: (0, 0)), vec, vec],
        out_specs=pl.BlockSpec((1, ts, d), row),
        out_shape=jax.ShapeDtypeStruct((bsz, s, d), _F32),
        scratch_shapes=[pltpu.VMEM((8, d), _F32)],
        compiler_params=_cparams(("arbitrary", "arbitrary")),
        name="l1_conv_ln",
    )(x, sc, sh, gate, w_in, taps, w_out, ln_g.reshape(1, d), ln_b.reshape(1, d))


ROUTE_TM = 512


def _router_kernel(x_ref, sc_ref, sh_ref, rw_ref, rb_ref, tri_ref, h_ref, route_ref, route_t_ref, cnt_ref, run_ref):
    tm = ROUTE_TM

    @pl.when((pl.program_id(0) == 0) & (pl.program_id(1) == 0))
    def _():
        run_ref[...] = jnp.zeros_like(run_ref)

    h = x_ref[0] * (1.0 + sc_ref[0]) + sh_ref[0]
    h_ref[0] = h
    lane = lax.broadcasted_iota(jnp.int32, (1, LANES), 1)
    h_hi, h_lo = _split_bf16(h)
    two = _dot(h_hi, rw_ref[...])
    logits = two[:, :LANES] + two[:, LANES:] + _dot(h_lo, rw_ref[:, :LANES]) + rb_ref[...]
    logits = jnp.where(lane < N_EXPERTS, logits, -3e38)
    m1 = jnp.max(logits, axis=1, keepdims=True)
    e1 = jnp.min(jnp.where(logits == m1, lane, LANES), axis=1, keepdims=True)
    rest = jnp.where(lane == e1, -3e38, logits)
    m2 = jnp.max(rest, axis=1, keepdims=True)
    e2 = jnp.min(jnp.where(rest == m2, lane, LANES), axis=1, keepdims=True)
    ex = jnp.exp(m2 - m1)
    g1 = 1.0 / (1.0 + ex)
    g2 = ex / (1.0 + ex)
    oh1 = jnp.where(lane == e1, 1.0, 0.0)
    oh2 = jnp.where(lane == e2, 1.0, 0.0)
    oh = oh1 + oh2
    before = run_ref[...] + _dot(tri_ref[...], oh.astype(_BF16))
    p1 = jnp.sum(oh1 * before, axis=1, keepdims=True)
    p2 = jnp.sum(oh2 * before, axis=1, keepdims=True)
    run_ref[...] += jnp.sum(oh, axis=0, keepdims=True)
    cnt_ref[...] = run_ref[...]
    vals = (e1.astype(_F32), e2.astype(_F32), p1, p2, g1, g2)
    out = jnp.zeros((tm, LANES), _F32)
    for k, v in enumerate(vals):
        out = jnp.where(lane == k, v, out)
    route_ref[0] = out
    route_t_ref[0, 0] = out.T[:8, :]


def _router(x, sc, sh, rw, rb):
    bsz, s, d = x.shape
    tm = ROUTE_TM
    row = lambda b, i: (b, i, 0)
    mod = pl.BlockSpec((1, 1, d), lambda b, i: (b, 0, 0))
    tri = (np.arange(tm)[None, :] < np.arange(tm)[:, None]).astype(np.float32)
    return pl.pallas_call(
        _router_kernel,
        grid=(bsz, s // tm),
        in_specs=[pl.BlockSpec((1, tm, d), row), mod, mod,
                  pl.BlockSpec((d, 2 * LANES), lambda b, i: (0, 0)),
                  pl.BlockSpec((1, LANES), lambda b, i: (0, 0)),
                  pl.BlockSpec((tm, tm), lambda b, i: (0, 0))],
        out_specs=[pl.BlockSpec((1, tm, d), row),
                   pl.BlockSpec((1, tm, LANES), row),
                   pl.BlockSpec((1, 1, 8, tm), lambda b, i: (b, i, 0, 0)),
                   pl.BlockSpec((1, LANES), lambda b, i: (0, 0))],
        out_shape=[jax.ShapeDtypeStruct((bsz, s, d), _F32),
                   jax.ShapeDtypeStruct((bsz, s, LANES), _F32),
                   jax.ShapeDtypeStruct((bsz, s // tm, 8, tm), _F32),
                   jax.ShapeDtypeStruct((1, LANES), _F32)],
        scratch_shapes=[pltpu.VMEM((1, LANES), _F32)],
        compiler_params=_cparams(("arbitrary", "arbitrary")),
        name="l1_router",
    )(x, sc, sh, rw, rb, jnp.asarray(tri, _BF16))


def _row_tokens_kernel(d1_ref, d2_ref, lo_ref, hi_ref, o_ref):
    def clear_range(k, carry):
        def clear(r, c):
            o_ref[r] = 0
            return c
        return lax.fori_loop(lo_ref[k], hi_ref[k], clear, carry)
    lax.fori_loop(0, lo_ref.shape[0], clear_range, 0)

    def place(a, carry):
        o_ref[d1_ref[a]] = a
        o_ref[d2_ref[a]] = a
        return carry
    lax.fori_loop(0, d1_ref.shape[0], place, 0, unroll=8)


def _row_tokens(dest1, dest2, pad_lo, pad_hi, rows):
    smem = pl.BlockSpec(memory_space=pltpu.SMEM)
    return pl.pallas_call(
        _row_tokens_kernel,
        in_specs=[smem, smem, smem, smem],
        out_specs=smem,
        out_shape=jax.ShapeDtypeStruct((rows,), jnp.int32),
        name="l1_row_tokens",
    )(dest1, dest2, pad_lo, pad_hi)


GATHER_UNROLL = 8


def _row_gather(src_hbm, idx_ref, base, dst_ref, sem, n):
    def issue(r, carry):
        tok = idx_ref[base + r]
        pltpu.make_async_copy(src_hbm.at[pl.ds(tok, 1)], dst_ref.at[pl.ds(r, 1)], sem).start()
        return carry
    lax.fori_loop(0, n, issue, 0, unroll=GATHER_UNROLL)


def _row_gather_wait(src_hbm, dst_ref, sem, n):
    pltpu.make_async_copy(src_hbm.at[pl.ds(0, n)], dst_ref, sem).wait()


def _expert_kernel(blk_e_ref, n_used_ref, row_tok_ref, h_hbm, w1_ref, w3_ref, w2_ref, y_ref, xbuf, sems):
    i = pl.program_id(0)
    n_used = n_used_ref[0]
    rb = MOE_ROW_BLOCK
    slot = lax.rem(i, 2)

    def gather(block, s):
        for r in range(rb):
            pltpu.make_async_copy(h_hbm.at[pl.ds(row_tok_ref[block * rb + r], 1)], xbuf.at[s, pl.ds(r, 1)],
                                  sems.at[s]).start()

    @pl.when((i == 0) & (n_used > 0))
    def _():
        _row_gather(h_hbm, row_tok_ref, 0, xbuf.at[0], sems.at[0], rb)

    for s in range(2):
        @pl.when((i + 1 < n_used) & (slot == 1 - s))
        def _():
            gather(i + 1, s)

    @pl.when(i < n_used)
    def _():
        _row_gather_wait(h_hbm, xbuf.at[slot], sems.at[slot], rb)
        y_ref[...] = _swiglu(xbuf[slot].astype(_BF16), w1_ref.at[0], w3_ref.at[0], w2_ref.at[0])

    @pl.when(i >= n_used)
    def _():
        y_ref[...] = jnp.zeros_like(y_ref)


def _experts(blk_e, n_used, row_tok, h_flat, w1, w3, w2):
    t, d = h_flat.shape
    n_blocks = blk_e.shape[0]
    rb = MOE_ROW_BLOCK
    ff = w1.shape[2]
    grid_spec = pltpu.PrefetchScalarGridSpec(
        num_scalar_prefetch=3,
        grid=(n_blocks,),
        in_specs=[pl.BlockSpec(memory_space=pl.ANY),
                  pl.BlockSpec((1, d, ff), lambda i, be, nu, rt: (be[i], 0, 0)),
                  pl.BlockSpec((1, d, ff), lambda i, be, nu, rt: (be[i], 0, 0)),
                  pl.BlockSpec((1, ff, d), lambda i, be, nu, rt: (be[i], 0, 0))],
        out_specs=pl.BlockSpec((rb, d), lambda i, be, nu, rt: (i, 0)),
        scratch_shapes=[pltpu.VMEM((2, rb, d), _F32), pltpu.SemaphoreType.DMA((2,))],
    )
    return pl.pallas_call(
        _expert_kernel,
        grid_spec=grid_spec,
        out_shape=jax.ShapeDtypeStruct((n_blocks * rb, d), _F32),
        compiler_params=_cparams(("arbitrary",)),
        name="l1_experts",
    )(blk_e, n_used, row_tok, h_flat, w1, w3, w2)


COMB_TM = 512


def _combine_kernel(d1_ref, d2_ref, y_hbm, x_ref, route_ref, g_ref, lg_ref, lb_ref, o_ref, y1buf, y2buf, sems):
    tm = COMB_TM
    i = pl.program_id(0) * pl.num_programs(1) + pl.program_id(1)
    n = pl.num_programs(0) * pl.num_programs(1)
    slot = lax.rem(i, 2)

    def gather(tile, s):
        for r in range(tm):
            pltpu.make_async_copy(y_hbm.at[pl.ds(d1_ref[tile * tm + r], 1)], y1buf.at[s, pl.ds(r, 1)],
                                  sems.at[0, s]).start(priority=0)
            pltpu.make_async_copy(y_hbm.at[pl.ds(d2_ref[tile * tm + r], 1)], y2buf.at[s, pl.ds(r, 1)],
                                  sems.at[1, s]).start(priority=1)

    @pl.when(i == 0)
    def _():
        gather(0, 0)

    for s in range(2):
        @pl.when((i + 1 < n) & (slot == 1 - s))
        def _():
            gather(i + 1, s)

    _row_gather_wait(y_hbm, y1buf.at[slot], sems.at[0, slot], tm)
    _row_gather_wait(y_hbm, y2buf.at[slot], sems.at[1, slot], tm)
    route = route_ref[0]
    y = route[:, 4:5] * y1buf[slot] + route[:, 5:6] * y2buf[slot]
    r = DN_ALPHA * x_ref[0] + (1.0 + g_ref[0]) * y
    o_ref[0] = _layer_norm(r, lg_ref[...], lb_ref[...])


def _combine_ln(dest1, dest2, y, x, route, gate, ln_g, ln_b):
    bsz, s, d = x.shape
    tm = COMB_TM
    row = lambda b, i, d1, d2: (b, i, 0)
    vec = pl.BlockSpec((1, d), lambda b, i, d1, d2: (0, 0))
    grid_spec = pltpu.PrefetchScalarGridSpec(
        num_scalar_prefetch=2,
        grid=(bsz, s // tm),
        in_specs=[pl.BlockSpec(memory_space=pl.ANY),
                  pl.BlockSpec((1, tm, d), row),
                  pl.BlockSpec((1, tm, LANES), row),
                  pl.BlockSpec((1, 1, d), lambda b, i, d1, d2: (b, 0, 0)), vec, vec],
        out_specs=pl.BlockSpec((1, tm, d), row),
        scratch_shapes=[pltpu.VMEM((2, tm, d), _F32), pltpu.VMEM((2, tm, d), _F32),
                        pltpu.SemaphoreType.DMA((2, 2))],
    )
    return pl.pallas_call(
        _combine_kernel,
        grid_spec=grid_spec,
        out_shape=jax.ShapeDtypeStruct((bsz, s, d), _F32),
        compiler_params=_cparams(("arbitrary", "arbitrary")),
        name="l1_combine_ln",
    )(dest1, dest2, y, x, route, gate, ln_g.reshape(1, d), ln_b.reshape(1, d))


def _arrange_inproj_weight(w):
    d = w.shape[0]
    kv0 = 3 * SB_W + NSA_Q_W
    g0 = kv0 + 6 * NSA_KV_GROUPS * HEAD_DIM
    per_group = NSA_REP * 3
    pad = jnp.zeros((d, LANES - per_group), w.dtype)
    cols = [w[:, :SB_W] * QK_SCALE, w[:, SB_W:3 * SB_W], w[:, 3 * SB_W:kv0] * (QK_SCALE * LOG2E), w[:, kv0:g0]]
    for g in range(NSA_KV_GROUPS):
        cols += [w[:, g0 + g * per_group:g0 + (g + 1) * per_group], pad]
    return jnp.concatenate(cols, axis=1).astype(_BF16)


def _mods(mod, i, bsz, d):
    m = mod[i].reshape(bsz, 6, 1, d)
    return [m[:, k] for k in range(6)]


def kernel(x, c, ada_w, ada_b, ln_g, ln_b, mix_w_in, cmp_pos, cmp_w1, cmp_w2, mix_w_out, ffn_w1, ffn_w3, ffn_w2,
           conv_w_in, conv_taps, conv_w_out, router_w, router_b, exp_w1, exp_w3, exp_w2):
    bsz, s, d = x.shape
    t = bsz * s
    mod = _ada_mod(c, ada_w, ada_b)

    sh1, sc1, g1, sh2, sc2, g2 = _mods(mod, 0, bsz, d)
    sb, nq, kvc, sw, gates = _inproj(x, sc1, sh1, _arrange_inproj_weight(mix_w_in[0]))
    o_sb = _sb_attention(sb)
    pos = cmp_pos[0].reshape(2, 2, NSA_CMP_STRIDE * HEAD_DIM)
    w2rep = jnp.tile(cmp_w2[0], (1, 1, NSA_REP)).astype(_BF16)
    cmp_kv = _compress(kvc, pos, cmp_w1[0].astype(_BF16), w2rep)
    o_nsa = _nsa_attention(nq, cmp_kv, sw, gates)
    x, exp_bf16 = _l0_tail(o_sb, o_nsa, mix_w_out[0].astype(_BF16), x, g1, sc2, sh2, g2, ffn_w1[0].astype(_BF16),
                           ffn_w3[0].astype(_BF16), ffn_w2[0].astype(_BF16), ln_g[0], ln_b[0],
                           (exp_w1[0], exp_w3[0], exp_w2[0]))

    sh1, sc1, g1, sh2, sc2, g2 = _mods(mod, 1, bsz, d)
    x = _conv_ln(x, sc1, sh1, g1, conv_w_in[0].astype(_BF16), conv_taps[0], conv_w_out[0].astype(_BF16),
                 ln_g[1, 0], ln_b[1, 0])
    rw = jnp.concatenate(_split_bf16(jnp.pad(router_w[0], ((0, 0), (0, LANES - N_EXPERTS)))), axis=1)
    rb = jnp.pad(router_b[0], (0, LANES - N_EXPERTS)).reshape(1, LANES)
    h2, route, route_t, counts = _router(x, sc2, sh2, rw, rb)

    field = lambda k: route_t[:, :, k, :].reshape(t).astype(jnp.int32)
    e1, e2, p1, p2 = field(0), field(1), field(2), field(3)
    cnt = counts[0, :N_EXPERTS].astype(jnp.int32)
    rb_rows = MOE_ROW_BLOCK
    padded = (cnt + rb_rows - 1) // rb_rows * rb_rows
    pends = jnp.cumsum(padded)
    pstarts = pends - padded
    dest1 = pstarts[e1] + p1
    dest2 = pstarts[e2] + p2
    n_blocks = -(-(t * 2) // rb_rows) + N_EXPERTS
    rows = n_blocks * rb_rows
    pad_lo = jnp.concatenate([pstarts + cnt, pends[-1:]]).astype(jnp.int32)
    pad_hi = jnp.concatenate([pends, jnp.full((1,), rows, jnp.int32)]).astype(jnp.int32)
    row_tok = _row_tokens(dest1, dest2, pad_lo, pad_hi, rows)
    blk_start = jnp.arange(n_blocks, dtype=jnp.int32) * rb_rows
    blk_e = jnp.minimum(jnp.sum(pends[None, :] <= blk_start[:, None], axis=1), N_EXPERTS - 1).astype(jnp.int32)
    n_used = (pends[-1:] // rb_rows).astype(jnp.int32)

    y = _experts(blk_e, n_used, row_tok, h2.reshape(t, d), *exp_bf16)
    return _combine_ln(dest1, dest2, y, x, route, g2, ln_g[1, 1], ln_b[1, 1])
```

```python
import functools

import numpy as np
import jax
import jax.numpy as jnp
from jax import lax
from jax.experimental import pallas as pl
from jax.experimental.pallas import tpu as pltpu

HEAD_DIM = 64
SB_HEADS = 8
NSA_HEADS = 8
NSA_KV_GROUPS = 2
NSA_REP = NSA_HEADS // NSA_KV_GROUPS
NSA_CMP_LEN = 32
NSA_CMP_STRIDE = 16
NSA_SLC_BLOCK = 64
NSA_SLC_TOPK = 16
NSA_WINDOW = 512
SB_W = SB_HEADS * HEAD_DIM
NSA_Q_W = NSA_HEADS * HEAD_DIM
N_EXPERTS = 8
MOE_ROW_BLOCK = 512
DEPTH = 2
LN_EPS = 1e-5
DN_ALPHA = (2 * DEPTH) ** 0.25
NEG = -1e30
SEL_FORCE = 1e30
QK_SCALE = HEAD_DIM ** -0.5
LOG2E = 1.4426950408889634

LANES = 128
VMEM_LIMIT = 56 * 1024 * 1024

C_SB = 0
C_NQ = 3 * SB_W
C_CMP = C_NQ + NSA_Q_W
C_SW = C_CMP + 4 * HEAD_DIM
C_GATE = C_SW + 4 * 2 * HEAD_DIM
C_END = C_GATE + 2 * LANES

_F32 = jnp.float32
_BF16 = jnp.bfloat16


def _cparams(sem, vmem=VMEM_LIMIT):
    return pltpu.CompilerParams(dimension_semantics=sem, vmem_limit_bytes=vmem)


def _dot(a, b):
    return jnp.dot(a, b, preferred_element_type=_F32)


def _dot_t(a, b):
    return lax.dot_general(a, b, (((1,), (1,)), ((), ())), preferred_element_type=_F32)


def _split_bf16(x):
    hi = x.astype(_BF16)
    lo = (x - hi.astype(_F32)).astype(_BF16)
    return hi, lo


def _silu(x):
    return x / (1.0 + jnp.exp(-x))


def _layer_norm(r, g, b):
    mu = jnp.mean(r, axis=-1, keepdims=True)
    d = r - mu
    var = jnp.mean(d * d, axis=-1, keepdims=True)
    return d * lax.rsqrt(var + LN_EPS) * g + b


def _ada_kernel(c_ref, w_ref, b_ref, o_ref):
    cond = _silu(c_ref[...])
    o_ref[0] = jnp.dot(cond, w_ref[0], preferred_element_type=_F32,
                       precision=lax.Precision.HIGHEST) + b_ref[0]


def _ada_mod(c, ada_w, ada_b):
    depth, d, n = ada_w.shape
    bsz = c.shape[0]
    tn = n // 3
    return pl.pallas_call(
        _ada_kernel,
        grid=(depth, n // tn),
        in_specs=[pl.BlockSpec((bsz, d), lambda i, j: (0, 0)),
                  pl.BlockSpec((1, d, tn), lambda i, j: (i, 0, j)),
                  pl.BlockSpec((1, 1, tn), lambda i, j: (i, 0, j))],
        out_specs=pl.BlockSpec((1, bsz, tn), lambda i, j: (i, 0, j)),
        out_shape=jax.ShapeDtypeStruct((depth, bsz, n), _F32),
        compiler_params=_cparams(("parallel", "parallel")),
        name="ada_mod",
    )(c, ada_w, ada_b.reshape(depth, 1, n))


def _inproj_kernel(x_ref, sc_ref, sh_ref, w_ref, sb_ref, nq_ref, cmp_ref, sw_ref, gate_ref):
    h = (x_ref[0] * (1.0 + sc_ref[0]) + sh_ref[0]).astype(_BF16)
    sb_ref[0] = _dot(h, w_ref[:, C_SB:C_NQ]).astype(_BF16)
    nq_ref[0] = _dot(h, w_ref[:, C_NQ:C_CMP]).astype(_BF16)
    kvc = _dot(h, w_ref[:, C_CMP:C_SW])
    for j in range(4):
        cmp_ref[0, j] = kvc[:, j * HEAD_DIM:(j + 1) * HEAD_DIM]
    lane = lax.broadcasted_iota(jnp.int32, (1, LANES), 1)
    first = lane < HEAD_DIM
    for kind in range(4):
        a = _dot(h, w_ref[:, C_SW + kind * LANES:C_SW + (kind + 1) * LANES])
        r = pltpu.roll(a, HEAD_DIM, 1)
        g0 = jnp.where(first, a, r).astype(_BF16)
        g1 = jnp.where(first, r, a).astype(_BF16)
        sw_ref[0, :, (2 * kind) * 256:(2 * kind + 1) * 256] = jnp.concatenate([g0, g0], axis=1)
        sw_ref[0, :, (2 * kind + 1) * 256:(2 * kind + 2) * 256] = jnp.concatenate([g1, g1], axis=1)
    gate_ref[0] = _dot(h, w_ref[:, C_GATE:C_END])


def _inproj(x, sc, sh, w):
    bsz, s, d = x.shape
    tm = 512
    row = lambda b, i: (b, i, 0)
    return pl.pallas_call(
        _inproj_kernel,
        grid=(bsz, s // tm),
        in_specs=[pl.BlockSpec((1, tm, d), row),
                  pl.BlockSpec((1, 1, d), lambda b, i: (b, 0, 0)),
                  pl.BlockSpec((1, 1, d), lambda b, i: (b, 0, 0)),
                  pl.BlockSpec((d, C_END), lambda b, i: (0, 0))],
        out_specs=[pl.BlockSpec((1, tm, 3 * SB_W), row),
                   pl.BlockSpec((1, tm, NSA_Q_W), row),
                   pl.BlockSpec((1, 4, tm, HEAD_DIM), lambda b, i: (b, 0, i, 0)),
                   pl.BlockSpec((1, tm, 2048), row),
                   pl.BlockSpec((1, tm, 2 * LANES), row)],
        out_shape=[jax.ShapeDtypeStruct((bsz, s, 3 * SB_W), _BF16),
                   jax.ShapeDtypeStruct((bsz, s, NSA_Q_W), _BF16),
                   jax.ShapeDtypeStruct((bsz, 4, s, HEAD_DIM), _F32),
                   jax.ShapeDtypeStruct((bsz, s, 2048), _BF16),
                   jax.ShapeDtypeStruct((bsz, s, 2 * LANES), _F32)],
        compiler_params=_cparams(("parallel", "parallel")),
        name="l0_inproj",
    )(x, sc, sh, w)


SB_TQ = 1024
SB_TK = 256
SB_UNROLL = 2
SB_PAIRS = 1


def _sb_kernel(q_ref, k_ref, v_ref, tri_ref, o_ref, acc_ref, run_ref):
    tq, tk = SB_TQ, SB_TK
    qi = pl.program_id(2)
    lane = lax.broadcasted_iota(jnp.int32, (1, LANES), 1)
    first = lane < HEAD_DIM
    pairs = range(SB_PAIRS)
    lanes = [slice(p * LANES, (p + 1) * LANES) for p in pairs]
    q2 = []
    for p in pairs:
        q = q_ref[0, :, lanes[p]]
        zero = jnp.zeros_like(q)
        q2.append(jnp.concatenate([jnp.where(first, q, zero), jnp.where(first, zero, q)], axis=0))
    tri = tri_ref[...]
    acc_ref[...] = jnp.zeros_like(acc_ref)
    run_ref[...] = jnp.zeros_like(run_ref)

    def scores(qrows, key_off, p):
        return _dot_t(qrows, k_ref[0, pl.ds(key_off, tk), lanes[p]])

    def suffix(z, causal):
        nlf = jnp.maximum(z, 0.0) + jnp.log(1.0 + jnp.exp2(jnp.abs(z) * -LOG2E))
        if causal is not None:
            nlf = jnp.where(causal, nlf, 0.0)
        return z - nlf, _dot(nlf.astype(_BF16), tri), nlf[:, 0:1]

    def weigh(lb, ext, nlf0, run, causal, key_off, p):
        w = jnp.exp(lb + ext + jnp.concatenate([run] * (tk // LANES), axis=1))
        if causal is not None:
            w = jnp.where(causal, w, 0.0)
        return (_dot(w.astype(_BF16), v_ref[0, pl.ds(key_off, tk), lanes[p]]),
                run + jnp.broadcast_to(ext[:, 0:1] - nlf0, run.shape))

    n_diag = tq // tk
    diag = []
    for kd in reversed(range(n_diag)):
        r0 = kd * tk
        key_off = pl.multiple_of(qi * tq + r0, tk)
        row = r0 + lax.broadcasted_iota(jnp.int32, (tq - r0, tk), 0)
        col = r0 + lax.broadcasted_iota(jnp.int32, (tq - r0, tk), 1)
        causal = jnp.concatenate([col < row] * 2, axis=0)
        for p in pairs:
            z = scores(jnp.concatenate([q2[p][r0:tq], q2[p][tq + r0:2 * tq]], axis=0), key_off, p)
            diag.append((p, r0, key_off, causal, z))
    diag = [(p, r0, key_off, causal, suffix(z, causal)) for p, r0, key_off, causal, z in diag]
    for p, r0, key_off, causal, parts in diag:
        n = tq - r0
        base = p * 2 * tq
        run = jnp.concatenate([run_ref[base + r0:base + tq, :], run_ref[base + tq + r0:base + 2 * tq, :]], axis=0)
        pv, run = weigh(*parts, run, causal, key_off, p)
        for h in range(2):
            rows = slice(base + h * tq + r0, base + (h + 1) * tq)
            acc_ref[rows, :] += pv[h * n:(h + 1) * n]
            run_ref[rows, :] = run[h * n:(h + 1) * n]

    def below(jj, carry):
        offs = [pl.multiple_of((qi * n_diag - 1 - SB_UNROLL * jj - u) * tk, tk) for u in range(SB_UNROLL)]
        zs = [[scores(q2[p], off, p) for off in offs] for p in pairs]
        parts = [[suffix(z, None) for z in zs[p]] for p in pairs]
        for p in pairs:
            rows = slice(p * 2 * tq, (p + 1) * 2 * tq)
            run = run_ref[rows, :]
            acc = acc_ref[rows, :]
            for part, off in zip(parts[p], offs):
                pv, run = weigh(*part, run, None, off, p)
                acc = acc + pv
            acc_ref[rows, :] = acc
            run_ref[rows, :] = run
        return carry
    lax.fori_loop(0, qi * (n_diag // SB_UNROLL), below, 0)
    for p in pairs:
        base = p * 2 * tq
        o_ref[0, :, lanes[p]] = jnp.where(first, acc_ref[base:base + tq, :],
                                          acc_ref[base + tq:base + 2 * tq, :]).astype(o_ref.dtype)


def _sb_attention(sb):
    bsz, s, _ = sb.shape
    tq, tk = SB_TQ, SB_TK
    width = SB_PAIRS * LANES
    n_step = SB_W // width
    tri = -(np.arange(tk)[:, None] > np.arange(tk)[None, :]).astype(np.float32)
    return pl.pallas_call(
        _sb_kernel,
        grid=(bsz, n_step, s // tq),
        in_specs=[pl.BlockSpec((1, tq, width), lambda b, p, i: (b, i, p)),
                  pl.BlockSpec((1, s, width), lambda b, p, i: (b, 0, n_step + p)),
                  pl.BlockSpec((1, s, width), lambda b, p, i: (b, 0, 2 * n_step + p)),
                  pl.BlockSpec((tk, tk), lambda b, p, i: (0, 0))],
        out_specs=pl.BlockSpec((1, tq, width), lambda b, p, i: (b, i, p)),
        out_shape=jax.ShapeDtypeStruct((bsz, s, SB_W), _BF16),
        scratch_shapes=[pltpu.VMEM((SB_PAIRS * 2 * tq, LANES), _F32), pltpu.VMEM((SB_PAIRS * 2 * tq, LANES), _F32)],
        compiler_params=_cparams(("parallel", "parallel", "arbitrary")),
        name="sb_attention",
    )(sb, sb, sb, jnp.asarray(tri, _BF16))


def _compress_kernel(x_ref, pos_ref, w1_ref, w2_ref, o_ref):
    half = w1_ref.shape[1] // 2
    n_chunk = x_ref.shape[2] // NSA_CMP_STRIDE
    ch = jnp.concatenate([x_ref[0, 0, pl.ds(l, n_chunk, stride=NSA_CMP_STRIDE), :]
                          for l in range(NSA_CMP_STRIDE)], axis=1)
    top = _dot((ch + pos_ref[0, 0:1, :]).astype(_BF16), w1_ref[0, :half, :])
    bot = _dot((ch + pos_ref[0, 1:2, :]).astype(_BF16), w1_ref[0, half:, :])
    n = ch.shape[0]
    pre = top + pltpu.roll(bot, n - 1, 0)
    o_ref[0, 0] = _dot(_silu(pre).astype(_BF16), w2_ref[0]).astype(o_ref.dtype)


def _compress(kvc, pos, w1, w2rep):
    bsz, _, s, dh = kvc.shape
    n_chunk = s // NSA_CMP_STRIDE
    cw = NSA_CMP_STRIDE * dh
    hid = w1.shape[-1]
    return pl.pallas_call(
        _compress_kernel,
        grid=(bsz, 4),
        in_specs=[pl.BlockSpec((1, 1, s, dh), lambda b, j: (b, j, 0, 0)),
                  pl.BlockSpec((1, 2, cw), lambda b, j: (j // 2, 0, 0)),
                  pl.BlockSpec((1, 2 * cw, hid), lambda b, j: (j // 2, 0, 0)),
                  pl.BlockSpec((1, hid, 4 * dh), lambda b, j: (j // 2, 0, 0))],
        out_specs=pl.BlockSpec((1, 1, n_chunk, 4 * dh), lambda b, j: (b, j, 0, 0)),
        out_shape=jax.ShapeDtypeStruct((bsz, 4, n_chunk, 4 * dh), _BF16),
        compiler_params=_cparams(("parallel", "parallel")),
        name="nsa_compress",
    )(kvc, pos, w1, w2rep)


NSA_TQ = 128
NSA_TK = 512
WIN_SPAN = NSA_WINDOW + NSA_TQ


def _nsa_kernel(*refs):
    s_len = refs[3].shape[1]
    tiles_per_blk = NSA_TK // NSA_TQ
    n_blk = s_len // NSA_TK
    i = pl.program_id(1)
    for v in range(n_blk // 2):
        pl.when(i // tiles_per_blk == v)(functools.partial(_nsa_body, v + 1, n_blk - v, *refs))


def _nsa_body(n_blk_a, n_blk_b, qa_ref, qb_ref, cmp_ref, sw_ref, ga_ref, gb_ref, ov_ref, ex_ref, o_ref):
    tq, tk = NSA_TQ, NSA_TK
    s_len = sw_ref.shape[1]
    n_tiles = s_len // tq
    n_cmp = cmp_ref.shape[2]
    n_slc = s_len // NSA_SLC_BLOCK
    gw = NSA_REP * HEAD_DIM
    i = pl.program_id(1)
    head_of_lane = lax.broadcasted_iota(jnp.int32, (1, gw), 1) // HEAD_DIM

    def kv_lanes(kind, g):
        return slice((2 * kind + g) * gw, (2 * kind + g + 1) * gw)

    def unstack(o):
        out = jnp.zeros((tq, gw), _F32)
        for r in range(NSA_REP):
            out = jnp.where(head_of_lane == r, o[r * tq:(r + 1) * tq], out)
        return out

    def tile4(x):
        return jnp.concatenate([x] * NSA_REP, axis=0)

    class Tile:
        pass
    tiles = []
    for half, (q_ref, g_ref, t0, n_blk) in enumerate(((qa_ref, ga_ref, i * tq, n_blk_a),
                                                      (qb_ref, gb_ref, (n_tiles - 1 - i) * tq, n_blk_b))):
        for g in range(NSA_KV_GROUPS):
            t = Tile()
            q = q_ref[0, :, g * gw:(g + 1) * gw]
            zero = jnp.zeros_like(q)
            t.qs = jnp.concatenate([jnp.where(head_of_lane == r, q, zero) for r in range(NSA_REP)], axis=0)
            t.half, t.g, t.t0, t.n_blk, t.g_ref = half, g, t0, n_blk, g_ref
            t.t_row = t0 + lax.broadcasted_iota(jnp.int32, (tq, 1), 0)
            t.start = pl.multiple_of(jnp.maximum(t0 - NSA_WINDOW, 0), tq)
            tiles.append(t)

    for t in tiles:
        t.s_c = _dot_t(t.qs, cmp_ref[0, t.g])
        t.s_w = _dot_t(t.qs, sw_ref[0, pl.ds(t.start, WIN_SPAN), kv_lanes(2, t.g)])
        t.s_s = [_dot_t(t.qs, sw_ref[0, j * tk:(j + 1) * tk, kv_lanes(0, t.g)]) for j in range(t.n_blk)]

    cmp_last = lax.broadcasted_iota(jnp.int32, (1, n_cmp), 1) * NSA_CMP_STRIDE + (NSA_CMP_LEN - 1)
    for t in tiles:
        mask_c = tile4(cmp_last <= t.t_row)
        sm = jnp.where(mask_c, t.s_c, NEG)
        e_c = jnp.where(mask_c, jnp.exp2(sm - jnp.max(sm, axis=1, keepdims=True)), 0.0)
        l_c = jnp.sum(e_c, axis=1, keepdims=True)
        t.p_c = e_c / jnp.where(l_c == 0.0, 1.0, l_c)
    for t in tiles:
        t.o_cmp = unstack(_dot(t.p_c.astype(_BF16), cmp_ref[0, NSA_KV_GROUPS + t.g]))
        p_sum = t.p_c[0:tq] + t.p_c[tq:2 * tq] + t.p_c[2 * tq:3 * tq] + t.p_c[3 * tq:4 * tq]
        hi, lo = _split_bf16(p_sum)
        t.imp = _dot_t(ov_ref[...], jnp.concatenate([hi, lo], axis=1))[:n_slc]

    for t in tiles:
        diff = t.t_row - (t.start + lax.broadcasted_iota(jnp.int32, (1, WIN_SPAN), 1))
        bias_w = jnp.where((diff >= 0) & (diff < NSA_WINDOW), 0.0, NEG)
        sm = t.s_w + tile4(bias_w)
        e_w = jnp.exp2(sm - jnp.max(sm, axis=1, keepdims=True))
        t.l_w = jnp.sum(e_w, axis=1, keepdims=True)
        t.e_w = e_w.astype(_BF16)
    for t in tiles:
        t.o_win = unstack(_dot(t.e_w, sw_ref[0, pl.ds(t.start, WIN_SPAN), kv_lanes(3, t.g)]) / t.l_w)

    blk = lax.broadcasted_iota(jnp.int32, (n_slc, 1), 0)
    for t in tiles:
        cur = (t.t0 + lax.broadcasted_iota(jnp.int32, (1, tq), 1)) // NSA_SLC_BLOCK
        valid = blk <= cur
        forced = (blk == 0) | (blk == cur) | (blk == cur - 1)
        val = jnp.where(valid, jnp.where(forced, SEL_FORCE, t.imp), NEG)
        rank = jnp.zeros((n_slc, tq), _F32)
        for b in range(n_slc):
            vb = val[b:b + 1, :]
            ahead = (vb > val) | ((vb == val) & (blk > b))
            rank = rank + jnp.where(ahead, 1.0, 0.0)
        sel_t = jnp.where(rank < float(min(NSA_SLC_TOPK, n_slc)), 1.0, 0.0)
        sel_t = jnp.concatenate([sel_t, jnp.zeros((LANES - n_slc, tq), _F32)], axis=0)
        t.sel = sel_t.T.astype(_BF16)

    for t in tiles:
        masked = []
        for j in range(t.n_blk):
            chosen = _dot(t.sel, ex_ref[:, j * tk:(j + 1) * tk])
            kpos = j * tk + lax.broadcasted_iota(jnp.int32, (1, tk), 1)
            bias = jnp.where((chosen > 0.5) & (kpos <= t.t_row), 0.0, NEG)
            masked.append(t.s_s[j] + tile4(bias))
            m_j = masked[j].max(axis=1, keepdims=True)
            m = m_j if j == 0 else jnp.maximum(m, m_j)
        e_s = [jnp.exp2(sb - m) for sb in masked]
        t.l_s = sum(e.sum(axis=1, keepdims=True) for e in e_s)
        t.e_s = [e.astype(_BF16) for e in e_s]
    for t in tiles:
        acc = _dot(t.e_s[0], sw_ref[0, 0:tk, kv_lanes(1, t.g)])
        for j in range(1, t.n_blk):
            acc = acc + _dot(t.e_s[j], sw_ref[0, j * tk:(j + 1) * tk, kv_lanes(1, t.g)])
        t.o_slc = unstack(acc / t.l_s)

    for t in tiles:
        g = 1.0 / (1.0 + jnp.exp(-t.g_ref[0, :, t.g * LANES:(t.g + 1) * LANES]))
        out = jnp.zeros((tq, gw), _F32)
        for branch, o_b in enumerate((t.o_cmp, t.o_slc, t.o_win)):
            gm = jnp.zeros((tq, gw), _F32)
            for r in range(NSA_REP):
                c = 3 * r + branch
                gm = jnp.where(head_of_lane == r, g[:, c:c + 1], gm)
            out = out + gm * o_b
        o_ref[0, t.half, 0, :, t.g * gw:(t.g + 1) * gw] = out.astype(o_ref.dtype)


def _nsa_attention(nq, cmp_kv, sw, gates):
    bsz, s, _ = nq.shape
    tq = NSA_TQ
    n_cmp = cmp_kv.shape[2]
    n_slc = s // NSA_SLC_BLOCK
    gw = 4 * HEAD_DIM
    cmp_start = np.arange(n_cmp) * NSA_CMP_STRIDE
    slc_start = np.arange(LANES) * NSA_SLC_BLOCK
    ov = ((cmp_start[:, None] < slc_start[None, :] + NSA_SLC_BLOCK)
          & (cmp_start[:, None] + NSA_CMP_LEN > slc_start[None, :])
          & (np.arange(LANES)[None, :] < n_slc)).astype(np.float32)
    ov[n_cmp - 1:] = 0.0
    ov2 = np.concatenate([ov.T, ov.T], axis=1)
    ex = (np.arange(LANES)[:, None] == (np.arange(s)[None, :] // NSA_SLC_BLOCK)).astype(np.float32)
    n_tiles = s // tq
    n_pair = n_tiles // 2
    return pl.pallas_call(
        _nsa_kernel,
        grid=(bsz, n_pair),
        in_specs=[pl.BlockSpec((1, tq, NSA_Q_W), lambda b, i: (b, i, 0)),
                  pl.BlockSpec((1, tq, NSA_Q_W), lambda b, i: (b, n_tiles - 1 - i, 0)),
                  pl.BlockSpec((1, 2 * NSA_KV_GROUPS, n_cmp, gw), lambda b, i: (b, 0, 0, 0)),
                  pl.BlockSpec((1, s, sw.shape[2]), lambda b, i: (b, 0, 0)),
                  pl.BlockSpec((1, tq, NSA_KV_GROUPS * LANES), lambda b, i: (b, i, 0)),
                  pl.BlockSpec((1, tq, NSA_KV_GROUPS * LANES), lambda b, i: (b, n_tiles - 1 - i, 0)),
                  pl.BlockSpec((LANES, 2 * n_cmp), lambda b, i: (0, 0)),
                  pl.BlockSpec((LANES, s), lambda b, i: (0, 0))],
        out_specs=pl.BlockSpec((1, 2, 1, tq, NSA_Q_W), lambda b, i: (b, 0, i, 0, 0)),
        out_shape=jax.ShapeDtypeStruct((bsz, 2, n_pair, tq, NSA_Q_W), _BF16),
        compiler_params=_cparams(("parallel", "arbitrary")),
        name="nsa_attention",
    )(nq, nq, cmp_kv, sw, gates, gates, jnp.asarray(ov2, _BF16), jnp.asarray(ex, _BF16))


FFN_TF = 256
L0_TM = 512


def _swiglu(xb, w1, w3, w2):
    ff = w1.shape[1]
    acc = jnp.zeros((xb.shape[0], w2.shape[1]), _F32)
    for c in range(ff // FFN_TF):
        cs = slice(c * FFN_TF, (c + 1) * FFN_TF)
        a = _silu(_dot(xb, w1[:, cs])) * _dot(xb, w3[:, cs])
        acc = acc + _dot(a.astype(_BF16), w2[cs, :])
    return acc


def _l0_tail_kernel(sb_ref, nsa_ref, wa_ref, wb_ref, x_ref, g1_ref, sc_ref, sh_ref, g2_ref, w1_ref, w3_ref, w2_ref,
                    lg1_ref, lb1_ref, lg2_ref, lb2_ref, e1_ref, e3_ref, e2_ref, o_ref, c1_ref, c3_ref, c2_ref):
    c1_ref[...] = e1_ref[...].astype(_BF16)
    c3_ref[...] = e3_ref[...].astype(_BF16)
    c2_ref[...] = e2_ref[...].astype(_BF16)
    n_sub = nsa_ref.shape[2]
    tiles = [nsa_ref[0, 0, k] for k in range(n_sub)]
    upper = pl.program_id(1) >= pl.num_programs(1) // 2
    o_nsa = jnp.where(upper, jnp.concatenate(tiles[::-1], axis=0), jnp.concatenate(tiles, axis=0))
    y = _dot(sb_ref[0], wa_ref[...]) + _dot(o_nsa, wb_ref[...])
    x1 = _layer_norm(DN_ALPHA * x_ref[0] + (1.0 + g1_ref[0]) * y, lg1_ref[...], lb1_ref[...])
    h = (x1 * (1.0 + sc_ref[0]) + sh_ref[0]).astype(_BF16)
    r = DN_ALPHA * x1 + (1.0 + g2_ref[0]) * _swiglu(h, w1_ref, w3_ref, w2_ref)
    o_ref[0] = _layer_norm(r, lg2_ref[...], lb2_ref[...])


def _l0_tail(o_sb, o_nsa5, w_out, x, g1, sc2, sh2, g2, w1, w3, w2, ln_g, ln_b, exp_w):
    bsz, s, d = x.shape
    ff = w1.shape[1]
    tm = L0_TM
    n_sub = tm // NSA_TQ
    per_half = s // (2 * tm)
    n_i = s // tm
    row = lambda b, i: (b, i, 0)
    mod = pl.BlockSpec((1, 1, d), lambda b, i: (b, 0, 0))
    vec = pl.BlockSpec((1, d), lambda b, i: (0, 0))
    flat = [w.reshape(-1, w.shape[-1]) for w in exp_w]
    side = [pl.BlockSpec((f.shape[0] // (bsz * n_i), f.shape[1]), lambda b, i: (b * n_i + i, 0)) for f in flat]

    def nsa_index(b, i):
        half, j = i // per_half, i % per_half
        return b, half, jnp.where(half == 0, j, per_half - 1 - j), 0, 0
    out = pl.pallas_call(
        _l0_tail_kernel,
        grid=(bsz, s // tm),
        in_specs=[pl.BlockSpec((1, tm, SB_W), row),
                  pl.BlockSpec((1, 1, n_sub, NSA_TQ, NSA_Q_W), nsa_index),
                  pl.BlockSpec((SB_W, d), lambda b, i: (0, 0)),
                  pl.BlockSpec((NSA_Q_W, d), lambda b, i: (1, 0)),
                  pl.BlockSpec((1, tm, d), row), mod, mod, mod, mod,
                  pl.BlockSpec((d, ff), lambda b, i: (0, 0)),
                  pl.BlockSpec((d, ff), lambda b, i: (0, 0)),
                  pl.BlockSpec((ff, d), lambda b, i: (0, 0)), vec, vec, vec, vec] + side,
        out_specs=[pl.BlockSpec((1, tm, d), row)] + side,
        out_shape=[jax.ShapeDtypeStruct((bsz, s, d), _F32)] + [jax.ShapeDtypeStruct(f.shape, _BF16) for f in flat],
        compiler_params=_cparams(("parallel", "parallel")),
        name="l0_outproj_ffn_ln",
    )(o_sb, o_nsa5, w_out, w_out, x, g1, sc2, sh2, g2, w1, w3, w2,
      ln_g[0].reshape(1, d), ln_b[0].reshape(1, d), ln_g[1].reshape(1, d), ln_b[1].reshape(1, d), *flat)
    return out[0], [c.reshape(w.shape) for c, w in zip(out[1:], exp_w)]


def _conv_kernel(x_ref, sc_ref, sh_ref, g_ref, win_ref, taps_ref, wout_ref, lg_ref, lb_ref,
                 sc2_ref, sh2_ref, rw_ref, rb_ref, tri_ref,
                 o_ref, h_ref, route_ref, route_t_ref, cnt_ref, tail_ref, run_ref):
    ts, d = x_ref.shape[1], x_ref.shape[2]

    @pl.when(pl.program_id(1) == 0)
    def _():
        tail_ref[...] = jnp.zeros_like(tail_ref)

    x = x_ref[0]
    h = (x * (1.0 + sc_ref[0]) + sh_ref[0]).astype(_BF16)
    z = _dot(h, win_ref[:, d:2 * d]) * _dot(h, win_ref[:, 2 * d:3 * d])
    gate_b = _dot(h, win_ref[:, 0:d])
    row = lax.broadcasted_iota(jnp.int32, (ts, 1), 0)
    prev1 = tail_ref[7:8, :]
    prev2 = tail_ref[6:7, :]
    z1 = jnp.where(row == 0, prev1, pltpu.roll(z, 1, 0))
    z2 = jnp.where(row == 0, prev2, jnp.where(row == 1, prev1, pltpu.roll(z, 2, 0)))
    zc = taps_ref[0:1, :] * z2 + taps_ref[1:2, :] * z1 + taps_ref[2:3, :] * z
    tail_ref[...] = z[ts - 8:ts, :]
    y = _dot((gate_b * zc).astype(_BF16), wout_ref[...])
    r = DN_ALPHA * x + (1.0 + g_ref[0]) * y
    x3 = _layer_norm(r, lg_ref[...], lb_ref[...])
    o_ref[0] = x3
    _route_tile(x3, sc2_ref, sh2_ref, rw_ref, rb_ref, tri_ref, h_ref, route_ref, route_t_ref, cnt_ref, run_ref)


def _conv_route(x, sc, sh, gate, w_in, taps, w_out, ln_g, ln_b, sc2, sh2, rw, rb):
    bsz, s, d = x.shape
    ts = ROUTE_TM
    row = lambda b, i: (b, i, 0)
    mod = pl.BlockSpec((1, 1, d), lambda b, i: (b, 0, 0))
    vec = pl.BlockSpec((1, d), lambda b, i: (0, 0))
    tri = (np.arange(ts)[None, :] < np.arange(ts)[:, None]).astype(np.float32)
    return pl.pallas_call(
        _conv_kernel,
        grid=(bsz, s // ts),
        in_specs=[pl.BlockSpec((1, ts, d), row), mod, mod, mod,
                  pl.BlockSpec((d, 3 * d), lambda b, i: (0, 0)),
                  pl.BlockSpec((taps.shape[0], d), lambda b, i: (0, 0)),
                  pl.BlockSpec((d, d), lambda b, i: (0, 0)), vec, vec, mod, mod,
                  pl.BlockSpec((d, 2 * LANES), lambda b, i: (0, 0)),
                  pl.BlockSpec((1, LANES), lambda b, i: (0, 0)),
                  pl.BlockSpec((ts, ts), lambda b, i: (0, 0))],
        out_specs=[pl.BlockSpec((1, ts, d), row),
                   pl.BlockSpec((1, ts, d), row),
                   pl.BlockSpec((1, ts, LANES), row),
                   pl.BlockSpec((1, 1, 8, ts), lambda b, i: (b, i, 0, 0)),
                   pl.BlockSpec((1, LANES), lambda b, i: (0, 0))],
        out_shape=[jax.ShapeDtypeStruct((bsz, s, d), _F32),
                   jax.ShapeDtypeStruct((bsz, s, d), _F32),
                   jax.ShapeDtypeStruct((bsz, s, LANES), _F32),
                   jax.ShapeDtypeStruct((bsz, s // ts, 8, ts), _F32),
                   jax.ShapeDtypeStruct((1, LANES), _F32)],
        scratch_shapes=[pltpu.VMEM((8, d), _F32), pltpu.VMEM((1, LANES), _F32)],
        compiler_params=_cparams(("arbitrary", "arbitrary")),
        name="l1_conv_route",
    )(x, sc, sh, gate, w_in, taps, w_out, ln_g.reshape(1, d), ln_b.reshape(1, d), sc2, sh2, rw, rb,
      jnp.asarray(tri, _BF16))


ROUTE_TM = 512


def _route_tile(x3, sc_ref, sh_ref, rw_ref, rb_ref, tri_ref, h_ref, route_ref, route_t_ref, cnt_ref, run_ref):
    tm = ROUTE_TM

    @pl.when((pl.program_id(0) == 0) & (pl.program_id(1) == 0))
    def _():
        run_ref[...] = jnp.zeros_like(run_ref)

    h = x3 * (1.0 + sc_ref[0]) + sh_ref[0]
    h_ref[0] = h
    lane = lax.broadcasted_iota(jnp.int32, (1, LANES), 1)
    h_hi, h_lo = _split_bf16(h)
    two = _dot(h_hi, rw_ref[...])
    logits = two[:, :LANES] + two[:, LANES:] + _dot(h_lo, rw_ref[:, :LANES]) + rb_ref[...]
    logits = jnp.where(lane < N_EXPERTS, logits, -3e38)
    m1 = jnp.max(logits, axis=1, keepdims=True)
    e1 = jnp.min(jnp.where(logits == m1, lane, LANES), axis=1, keepdims=True)
    rest = jnp.where(lane == e1, -3e38, logits)
    m2 = jnp.max(rest, axis=1, keepdims=True)
    e2 = jnp.min(jnp.where(rest == m2, lane, LANES), axis=1, keepdims=True)
    ex = jnp.exp(m2 - m1)
    g1 = 1.0 / (1.0 + ex)
    g2 = ex / (1.0 + ex)
    oh1 = jnp.where(lane == e1, 1.0, 0.0)
    oh2 = jnp.where(lane == e2, 1.0, 0.0)
    oh = oh1 + oh2
    before = run_ref[...] + _dot(tri_ref[...], oh.astype(_BF16))
    p1 = jnp.sum(oh1 * before, axis=1, keepdims=True)
    p2 = jnp.sum(oh2 * before, axis=1, keepdims=True)
    run_ref[...] += jnp.sum(oh, axis=0, keepdims=True)
    cnt_ref[...] = run_ref[...]
    vals = (e1.astype(_F32), e2.astype(_F32), p1, p2, g1, g2)
    out = jnp.zeros((tm, LANES), _F32)
    for k, v in enumerate(vals):
        out = jnp.where(lane == k, v, out)
    route_ref[0] = out
    route_t_ref[0, 0] = out.T[:8, :]


def _row_tokens_kernel(d1_ref, d2_ref, lo_ref, hi_ref, o_ref):
    def clear_range(k, carry):
        def clear(r, c):
            o_ref[r] = 0
            return c
        return lax.fori_loop(lo_ref[k], hi_ref[k], clear, carry)
    lax.fori_loop(0, lo_ref.shape[0], clear_range, 0)

    def place(a, carry):
        o_ref[d1_ref[a]] = a
        o_ref[d2_ref[a]] = a
        return carry
    lax.fori_loop(0, d1_ref.shape[0], place, 0, unroll=8)


def _row_tokens(dest1, dest2, pad_lo, pad_hi, rows):
    smem = pl.BlockSpec(memory_space=pltpu.SMEM)
    return pl.pallas_call(
        _row_tokens_kernel,
        in_specs=[smem, smem, smem, smem],
        out_specs=smem,
        out_shape=jax.ShapeDtypeStruct((rows,), jnp.int32),
        name="l1_row_tokens",
    )(dest1, dest2, pad_lo, pad_hi)


GATHER_UNROLL = 8


def _row_gather(src_hbm, idx_ref, base, dst_ref, sem, n):
    def issue(r, carry):
        tok = idx_ref[base + r]
        pltpu.make_async_copy(src_hbm.at[pl.ds(tok, 1)], dst_ref.at[pl.ds(r, 1)], sem).start()
        return carry
    lax.fori_loop(0, n, issue, 0, unroll=GATHER_UNROLL)


def _row_gather_wait(src_hbm, dst_ref, sem, n):
    pltpu.make_async_copy(src_hbm.at[pl.ds(0, n)], dst_ref, sem).wait()


def _expert_kernel(blk_e_ref, n_used_ref, row_tok_ref, h_hbm, w1_ref, w3_ref, w2_ref, y_ref, xbuf, sems):
    i = pl.program_id(0)
    n_used = n_used_ref[0]
    rb = MOE_ROW_BLOCK
    slot = lax.rem(i, 2)

    def gather(block, s):
        for r in range(rb):
            pltpu.make_async_copy(h_hbm.at[pl.ds(row_tok_ref[block * rb + r], 1)], xbuf.at[s, pl.ds(r, 1)],
                                  sems.at[s]).start()

    @pl.when((i == 0) & (n_used > 0))
    def _():
        _row_gather(h_hbm, row_tok_ref, 0, xbuf.at[0], sems.at[0], rb)

    for s in range(2):
        @pl.when((i + 1 < n_used) & (slot == 1 - s))
        def _():
            gather(i + 1, s)

    @pl.when(i < n_used)
    def _():
        _row_gather_wait(h_hbm, xbuf.at[slot], sems.at[slot], rb)
        y_ref[...] = _swiglu(xbuf[slot].astype(_BF16), w1_ref.at[0], w3_ref.at[0], w2_ref.at[0])

    @pl.when(i >= n_used)
    def _():
        y_ref[...] = jnp.zeros_like(y_ref)


def _experts(blk_e, n_used, row_tok, h_flat, w1, w3, w2):
    t, d = h_flat.shape
    n_blocks = blk_e.shape[0]
    rb = MOE_ROW_BLOCK
    ff = w1.shape[2]
    grid_spec = pltpu.PrefetchScalarGridSpec(
        num_scalar_prefetch=3,
        grid=(n_blocks,),
        in_specs=[pl.BlockSpec(memory_space=pl.ANY),
                  pl.BlockSpec((1, d, ff), lambda i, be, nu, rt: (be[i], 0, 0)),
                  pl.BlockSpec((1, d, ff), lambda i, be, nu, rt: (be[i], 0, 0)),
                  pl.BlockSpec((1, ff, d), lambda i, be, nu, rt: (be[i], 0, 0))],
        out_specs=pl.BlockSpec((rb, d), lambda i, be, nu, rt: (i, 0)),
        scratch_shapes=[pltpu.VMEM((2, rb, d), _F32), pltpu.SemaphoreType.DMA((2,))],
    )
    return pl.pallas_call(
        _expert_kernel,
        grid_spec=grid_spec,
        out_shape=jax.ShapeDtypeStruct((n_blocks * rb, d), _F32),
        compiler_params=_cparams(("arbitrary",)),
        name="l1_experts",
    )(blk_e, n_used, row_tok, h_flat, w1, w3, w2)


COMB_TM = 512


def _combine_kernel(d1_ref, d2_ref, y_hbm, x_ref, route_ref, g_ref, lg_ref, lb_ref, o_ref, y1buf, y2buf, sems):
    tm = COMB_TM
    i = pl.program_id(0) * pl.num_programs(1) + pl.program_id(1)
    n = pl.num_programs(0) * pl.num_programs(1)
    slot = lax.rem(i, 2)

    def gather(tile, s):
        for r in range(tm):
            pltpu.make_async_copy(y_hbm.at[pl.ds(d1_ref[tile * tm + r], 1)], y1buf.at[s, pl.ds(r, 1)],
                                  sems.at[0, s]).start(priority=0)
            pltpu.make_async_copy(y_hbm.at[pl.ds(d2_ref[tile * tm + r], 1)], y2buf.at[s, pl.ds(r, 1)],
                                  sems.at[1, s]).start(priority=1)

    @pl.when(i == 0)
    def _():
        gather(0, 0)

    for s in range(2):
        @pl.when((i + 1 < n) & (slot == 1 - s))
        def _():
            gather(i + 1, s)

    _row_gather_wait(y_hbm, y1buf.at[slot], sems.at[0, slot], tm)
    _row_gather_wait(y_hbm, y2buf.at[slot], sems.at[1, slot], tm)
    route = route_ref[0]
    y = route[:, 4:5] * y1buf[slot] + route[:, 5:6] * y2buf[slot]
    r = DN_ALPHA * x_ref[0] + (1.0 + g_ref[0]) * y
    o_ref[0] = _layer_norm(r, lg_ref[...], lb_ref[...])


def _combine_ln(dest1, dest2, y, x, route, gate, ln_g, ln_b):
    bsz, s, d = x.shape
    tm = COMB_TM
    row = lambda b, i, d1, d2: (b, i, 0)
    vec = pl.BlockSpec((1, d), lambda b, i, d1, d2: (0, 0))
    grid_spec = pltpu.PrefetchScalarGridSpec(
        num_scalar_prefetch=2,
        grid=(bsz, s // tm),
        in_specs=[pl.BlockSpec(memory_space=pl.ANY),
                  pl.BlockSpec((1, tm, d), row),
                  pl.BlockSpec((1, tm, LANES), row),
                  pl.BlockSpec((1, 1, d), lambda b, i, d1, d2: (b, 0, 0)), vec, vec],
        out_specs=pl.BlockSpec((1, tm, d), row),
        scratch_shapes=[pltpu.VMEM((2, tm, d), _F32), pltpu.VMEM((2, tm, d), _F32),
                        pltpu.SemaphoreType.DMA((2, 2))],
    )
    return pl.pallas_call(
        _combine_kernel,
        grid_spec=grid_spec,
        out_shape=jax.ShapeDtypeStruct((bsz, s, d), _F32),
        compiler_params=_cparams(("arbitrary", "arbitrary")),
        name="l1_combine_ln",
    )(dest1, dest2, y, x, route, gate, ln_g.reshape(1, d), ln_b.reshape(1, d))


def _arrange_inproj_weight(w):
    d = w.shape[0]
    kv0 = 3 * SB_W + NSA_Q_W
    g0 = kv0 + 6 * NSA_KV_GROUPS * HEAD_DIM
    per_group = NSA_REP * 3
    pad = jnp.zeros((d, LANES - per_group), w.dtype)
    cols = [w[:, :SB_W] * QK_SCALE, w[:, SB_W:3 * SB_W], w[:, 3 * SB_W:kv0] * (QK_SCALE * LOG2E), w[:, kv0:g0]]
    for g in range(NSA_KV_GROUPS):
        cols += [w[:, g0 + g * per_group:g0 + (g + 1) * per_group], pad]
    return jnp.concatenate(cols, axis=1).astype(_BF16)


def _mods(mod, i, bsz, d):
    m = mod[i].reshape(bsz, 6, 1, d)
    return [m[:, k] for k in range(6)]


def kernel(x, c, ada_w, ada_b, ln_g, ln_b, mix_w_in, cmp_pos, cmp_w1, cmp_w2, mix_w_out, ffn_w1, ffn_w3, ffn_w2,
           conv_w_in, conv_taps, conv_w_out, router_w, router_b, exp_w1, exp_w3, exp_w2):
    bsz, s, d = x.shape
    t = bsz * s
    mod = _ada_mod(c, ada_w, ada_b)

    sh1, sc1, g1, sh2, sc2, g2 = _mods(mod, 0, bsz, d)
    sb, nq, kvc, sw, gates = _inproj(x, sc1, sh1, _arrange_inproj_weight(mix_w_in[0]))
    o_sb = _sb_attention(sb)
    pos = cmp_pos[0].reshape(2, 2, NSA_CMP_STRIDE * HEAD_DIM)
    w2rep = jnp.tile(cmp_w2[0], (1, 1, NSA_REP)).astype(_BF16)
    cmp_kv = _compress(kvc, pos, cmp_w1[0].astype(_BF16), w2rep)
    o_nsa = _nsa_attention(nq, cmp_kv, sw, gates)
    x, exp_bf16 = _l0_tail(o_sb, o_nsa, mix_w_out[0].astype(_BF16), x, g1, sc2, sh2, g2, ffn_w1[0].astype(_BF16),
                           ffn_w3[0].astype(_BF16), ffn_w2[0].astype(_BF16), ln_g[0], ln_b[0],
                           (exp_w1[0], exp_w3[0], exp_w2[0]))

    sh1, sc1, g1, sh2, sc2, g2 = _mods(mod, 1, bsz, d)
    rw = jnp.concatenate(_split_bf16(jnp.pad(router_w[0], ((0, 0), (0, LANES - N_EXPERTS)))), axis=1)
    rb = jnp.pad(router_b[0], (0, LANES - N_EXPERTS)).reshape(1, LANES)
    x, h2, route, route_t, counts = _conv_route(
        x, sc1, sh1, g1, conv_w_in[0].astype(_BF16), conv_taps[0], conv_w_out[0].astype(_BF16),
        ln_g[1, 0], ln_b[1, 0], sc2, sh2, rw, rb)

    field = lambda k: route_t[:, :, k, :].reshape(t).astype(jnp.int32)
    e1, e2, p1, p2 = field(0), field(1), field(2), field(3)
    cnt = counts[0, :N_EXPERTS].astype(jnp.int32)
    rb_rows = MOE_ROW_BLOCK
    padded = (cnt + rb_rows - 1) // rb_rows * rb_rows
    pends = jnp.cumsum(padded)
    pstarts = pends - padded
    dest1 = pstarts[e1] + p1
    dest2 = pstarts[e2] + p2
    n_blocks = -(-(t * 2) // rb_rows) + N_EXPERTS
    rows = n_blocks * rb_rows
    pad_lo = jnp.concatenate([pstarts + cnt, pends[-1:]]).astype(jnp.int32)
    pad_hi = jnp.concatenate([pends, jnp.full((1,), rows, jnp.int32)]).astype(jnp.int32)
    row_tok = _row_tokens(dest1, dest2, pad_lo, pad_hi, rows)
    blk_start = jnp.arange(n_blocks, dtype=jnp.int32) * rb_rows
    blk_e = jnp.minimum(jnp.sum(pends[None, :] <= blk_start[:, None], axis=1), N_EXPERTS - 1).astype(jnp.int32)
    n_used = (pends[-1:] // rb_rows).astype(jnp.int32)

    y = _experts(blk_e, n_used, row_tok, h2.reshape(t, d), *exp_bf16)
    return _combine_ln(dest1, dest2, y, x, route, g2, ln_g[1, 1], ln_b[1, 1])
```
